```python
import math
import jax, jax.numpy as jnp
from jax import lax
import numpy as np

D_MODEL = 2048
BATCH = 16
SEQ = 256
DEPTH = 2
DEC_BATCH = 4
DEC_SEQ = 1024
PAST_LEN = 256

GRID_W = 64
D_A = D_MODEL // 2
D_B = D_MODEL // 2
HGRN_EXPAND = 128
H_A = D_A // HGRN_EXPAND
DK_A = HGRN_EXPAND
DV_A = D_A // H_A
CHUNK = 32
HY_ORDER = 2
HY_EMB = 33
HY_BANDS = (HY_EMB - 1) // 2
HY_HID = 64
HY_SIN_W = 1.0
D_FF = 5632
N_MOD = 6 * D_MODEL
N_PROJ = 5 * D_A + 3 * D_B + 2 * D_MODEL
PROJ_SPLITS = (D_A, 2 * D_A, 3 * D_A, 4 * D_A, 5 * D_A, 5 * D_A + 3 * D_B, 5 * D_A + 3 * D_B + D_MODEL)
EPS = 1e-6

kernel_name = 'hgrn2_hyena_gated_prefix_diffusion_step'


def _rmsnorm(x, g):
    xf = x.astype(jnp.float32)
    y = xf * lax.rsqrt(jnp.mean(xf * xf, axis=-1, keepdims=True) + EPS)
    return y.astype(x.dtype) * g


def _dwconv_seq(x, w, b):
    L = x.shape[1]
    xp = jnp.pad(x, ((0, 0), (1, 1), (0, 0)))
    return xp[:, :L] * w[0] + xp[:, 1:L + 1] * w[1] + xp[:, 2:] * w[2] + b


def _dwconv_grid(x, w, b):
    bsz, L, C = x.shape
    rows = L // GRID_W
    xg = x.reshape(bsz, rows, GRID_W, C)
    y = lax.conv_general_dilated(xg, w[:, :, None, :], (1, 1), 'SAME',
                                 dimension_numbers=('NHWC', 'HWIO', 'NHWC'),
                                 feature_group_count=C)
    return y.reshape(bsz, L, C) + b


def _chunk_scan(q, k, v, logf, s0):
    bsz, L, H, _ = q.shape
    dv = v.shape[-1]
    n = L // CHUNK

    def to_chunks(t):
        return t.reshape(bsz, n, CHUNK, H, t.shape[-1]).transpose(1, 0, 3, 2, 4)

    tri = jnp.tril(jnp.ones((CHUNK, CHUNK), dtype=bool))[:, :, None]

    def step(s, inp):
        qc, kc, vc, lc = inp
        b = jnp.cumsum(lc, axis=2)
        o_inter = jnp.einsum('bhtd,bhde->bhte', qc * jnp.exp(b), s)
        diff = b[:, :, :, None, :] - b[:, :, None, :, :]
        decay = jnp.exp(jnp.where(tri, diff, -jnp.inf))
        att = jnp.einsum('bhtd,bhsd,bhtsd->bhts', qc, kc, decay)
        o = o_inter + jnp.einsum('bhts,bhse->bhte', att, vc)
        b_last = b[:, :, -1:, :]
        s_new = jnp.exp(b_last[:, :, 0, :])[..., None] * s + jnp.einsum(
            'bhsd,bhse->bhde', kc * jnp.exp(b_last - b), vc)
        return s_new, o

    s_fin, o = lax.scan(step, s0.astype(jnp.float32),
                        (to_chunks(q), to_chunks(k), to_chunks(v), to_chunks(logf)))
    o = o.transpose(1, 0, 3, 2, 4).reshape(bsz, L, H, dv)
    return o, s_fin


def _hgrn_mixer(q_raw, fa_raw, fb_raw, i_raw, g_raw, lb, norm_w, s0):
    bsz, L, _ = q_raw.shape
    f32 = jnp.float32
    q = jax.nn.silu(q_raw.astype(f32)).reshape(bsz, L, H_A, DK_A)
    v = i_raw.astype(f32).reshape(bsz, L, H_A, DV_A)
    o_sum = None
    states = []
    for d, (f_raw, rev) in enumerate(((fa_raw, False), (fb_raw, True))):
        z = f_raw.astype(f32)
        lbd = lb[d].astype(f32)
        logf = jnp.logaddexp(jnp.log(lbd), jnp.log1p(-lbd) + jax.nn.log_sigmoid(z))
        k = (1.0 - lbd) * jax.nn.sigmoid(-z)
        logf = logf.reshape(bsz, L, H_A, DK_A)
        k = k.reshape(bsz, L, H_A, DK_A)
        if rev:
            o_d, s_d = _chunk_scan(q[:, ::-1], k[:, ::-1], v[:, ::-1], logf[:, ::-1], s0[:, d])
            o_d = o_d[:, ::-1]
        else:
            o_d, s_d = _chunk_scan(q, k, v, logf, s0[:, d])
        o_sum = o_d if o_sum is None else o_sum + o_d
        states.append(s_d)
    o = _rmsnorm(o_sum, norm_w.astype(f32)).reshape(bsz, L, D_A)
    o = o * jax.nn.silu(g_raw.astype(f32))
    return o.astype(q_raw.dtype), jnp.stack(states, axis=1)


def _hyena_filters(L, w1, b1, w2, b2, w3, decay):
    f32 = jnp.float32
    t = jnp.arange(L, dtype=f32)
    t01 = t / max(L - 1, 1)
    bands = jnp.linspace(1e-4, HY_BANDS - 1, HY_BANDS, dtype=f32)
    ang = (2.0 * math.pi / L) * t[:, None] * bands[None, :]
    feats = jnp.concatenate([t01[:, None], jnp.cos(ang), -jnp.sin(ang)], axis=-1)
    h = jnp.sin(HY_SIN_W * (feats @ w1.astype(f32) + b1.astype(f32)))
    h = jnp.sin(HY_SIN_W * (h @ w2.astype(f32) + b2.astype(f32)))
    h = (h @ w3.astype(f32)).reshape(L, HY_ORDER, 2, D_B)
    window = jnp.exp(-t01[:, None, None] * jnp.abs(decay.astype(f32))[None])
    h = h * window[:, :, None, :]
    hf, hb = h[:, :, 0], h[:, :, 1]
    zero = jnp.zeros_like(hf[:1])
    k2 = jnp.concatenate([hf[:1] + hb[:1], hf[1:], zero, hb[1:][::-1]], axis=0)
    k2 = k2 / (jnp.sum(jnp.abs(k2), axis=0, keepdims=True) + EPS)
    return jnp.fft.rfft(k2, axis=0)


def _fftconv(z, kf, bias):
    L = z.shape[1]
    zf = jnp.fft.rfft(z, n=2 * L, axis=1)
    y = jnp.fft.irfft(zf * kf[None], n=2 * L, axis=1)[:, :L]
    return y + z * bias


def _hyena_mixer(u, conv_w, conv_b, w1, b1, w2, b2, w3, decay, bias):
    L = u.shape[1]
    uc = _dwconv_seq(u, conv_w, conv_b).astype(jnp.float32)
    v, x1, x2 = jnp.split(uc, 3, axis=-1)
    kf = _hyena_filters(L, w1, b1, w2, b2, w3, decay)
    bias = bias.astype(jnp.float32)
    z = x1 * _fftconv(v, kf[:, 0], bias[0])
    y = x2 * _fftconv(z, kf[:, 1], bias[1])
    return y.astype(u.dtype)


def _layer(x, cond, s0, lb, grid, lp):
    mod = jax.nn.silu(cond) @ lp['w_mod'] + lp['b_mod']
    sh1, sc1, gt1, sh2, sc2, gt2 = [m[:, None, :] for m in jnp.split(mod, 6, axis=-1)]
    h = _rmsnorm(x, lp['g_pre_mix']) * (1.0 + sc1) + sh1
    proj = h @ lp['w_in']
    q_raw, fa_raw, fb_raw, i_raw, g_raw, hy_in, gate_a, gate_b = jnp.split(proj, PROJ_SPLITS, axis=-1)
    o_a, s_fin = _hgrn_mixer(q_raw, fa_raw, fb_raw, i_raw, g_raw, lb, lp['hgrn_norm'], s0)
    o_b = _hyena_mixer(hy_in, lp['hy_conv_w'], lp['hy_conv_b'], lp['hy_w1'], lp['hy_b1'],
                       lp['hy_w2'], lp['hy_b2'], lp['hy_w3'], lp['hy_decay'], lp['hy_bias'])
    merged = (jax.nn.sigmoid(gate_a) * (o_a @ lp['w_branch_a'])
              + jax.nn.sigmoid(gate_b) * (o_b @ lp['w_branch_b']))
    x = x + gt1 * _rmsnorm(merged @ lp['w_out'], lp['g_post_mix'])
    h = _rmsnorm(x, lp['g_pre_ffn']) * (1.0 + sc2) + sh2
    u = h @ lp['ffn_w_up']
    if grid:
        u = _dwconv_grid(u, lp['ffn_conv_w'], lp['ffn_conv_b'])
    else:
        u = _dwconv_seq(u, lp['ffn_conv_w'][1], lp['ffn_conv_b'])
    a, vv = jnp.split(u, 2, axis=-1)
    x = x + gt2 * _rmsnorm((jax.nn.silu(a) * vv) @ lp['ffn_w_down'], lp['g_post_ffn'])
    return x, s_fin


def setup_inputs(seed: int = 0) -> dict:
    key = jax.random.key(seed)
    ks = jax.random.split(key, 32)
    f32 = jnp.float32
    D = D_MODEL

    def nrm(k, shape, scale):
        return jax.random.normal(k, shape, f32) * scale

    decay_base = jnp.asarray(np.linspace(math.log(1e2) / 1.5, math.log(1e2) / 0.3, D_B), dtype=f32)
    return {
        'x_prompt': nrm(ks[0], (BATCH, SEQ, D), 1.0),
        'x_sample': nrm(ks[1], (DEC_BATCH, DEC_SEQ, D), 1.0),
        'state_hgrn': nrm(ks[2], (DEC_BATCH, DEPTH, 2, H_A, DK_A, DV_A), 0.5),
        'c': nrm(ks[3], (DEC_BATCH, D), 1.0),
        'c_ctx': nrm(ks[4], (D,), 1.0),
        'w_mod': nrm(ks[5], (DEPTH, D, N_MOD), 0.5 * D ** -0.5),
        'b_mod': nrm(ks[6], (DEPTH, N_MOD), 0.01),
        'g_pre_mix': 1.0 + nrm(ks[7], (DEPTH, D), 0.05),
        'g_post_mix': 1.0 + nrm(ks[8], (DEPTH, D), 0.05),
        'g_pre_ffn': 1.0 + nrm(ks[9], (DEPTH, D), 0.05),
        'g_post_ffn': 1.0 + nrm(ks[10], (DEPTH, D), 0.05),
        'w_in': nrm(ks[11], (DEPTH, D, N_PROJ), D ** -0.5),
        'hgrn_lower_bounds': nrm(ks[12], (DEPTH, 2, D_A), 0.1),
        'hgrn_norm': 1.0 + nrm(ks[13], (DEPTH, DV_A), 0.05),
        'hy_conv_w': nrm(ks[14], (DEPTH, 3, 3 * D_B), 0.5),
        'hy_conv_b': nrm(ks[15], (DEPTH, 3 * D_B), 0.01),
        'hy_w1': nrm(ks[16], (DEPTH, HY_EMB, HY_HID), HY_EMB ** -0.5),
        'hy_b1': nrm(ks[17], (DEPTH, HY_HID), 0.1),
        'hy_w2': nrm(ks[18], (DEPTH, HY_HID, HY_HID), HY_HID ** -0.5),
        'hy_b2': nrm(ks[19], (DEPTH, HY_HID), 0.1),
        'hy_w3': nrm(ks[20], (DEPTH, HY_HID, HY_ORDER * 2 * D_B), HY_HID ** -0.5),
        'hy_decay': decay_base * (1.0 + nrm(ks[21], (DEPTH, HY_ORDER, D_B), 0.1)),
        'hy_bias': nrm(ks[22], (DEPTH, HY_ORDER, D_B), 0.5),
        'w_branch_a': nrm(ks[23], (DEPTH, D_A, D), D_A ** -0.5),
        'w_branch_b': nrm(ks[24], (DEPTH, D_B, D), D_B ** -0.5),
        'w_out': nrm(ks[25], (DEPTH, D, D), D ** -0.5),
        'ffn_w_up': nrm(ks[26], (DEPTH, D, 2 * D_FF), D ** -0.5),
        'ffn_conv_w': nrm(ks[27], (DEPTH, 3, 3, 2 * D_FF), 1.0 / 3.0),
        'ffn_conv_b': nrm(ks[28], (DEPTH, 2 * D_FF), 0.01),
        'ffn_w_down': nrm(ks[29], (DEPTH, D_FF, D), D_FF ** -0.5),
    }


def reference(x_prompt, x_sample, state_hgrn, c, c_ctx, w_mod, b_mod, g_pre_mix, g_post_mix,
              g_pre_ffn, g_post_ffn, w_in, hgrn_lower_bounds, hgrn_norm, hy_conv_w, hy_conv_b,
              hy_w1, hy_b1, hy_w2, hy_b2, hy_w3, hy_decay, hy_bias, w_branch_a, w_branch_b,
              w_out, ffn_w_up, ffn_conv_w, ffn_conv_b, ffn_w_down):
    p_lb = jax.nn.softmax(hgrn_lower_bounds.astype(jnp.float32), axis=0)
    cs = jnp.cumsum(p_lb, axis=0)
    lbs = cs - cs[:1]

    y_prompt = x_prompt
    y_sample = x_sample
    s0_ctx = jnp.zeros((x_prompt.shape[0], 2, H_A, DK_A, DV_A), jnp.float32)
    cond_ctx = c_ctx[None, :]
    new_states = []
    for l in range(DEPTH):
        lp = {
            'w_mod': w_mod[l], 'b_mod': b_mod[l],
            'g_pre_mix': g_pre_mix[l], 'g_post_mix': g_post_mix[l],
            'g_pre_ffn': g_pre_ffn[l], 'g_post_ffn': g_post_ffn[l],
            'w_in': w_in[l], 'hgrn_norm': hgrn_norm[l],
            'hy_conv_w': hy_conv_w[l], 'hy_conv_b': hy_conv_b[l],
            'hy_w1': hy_w1[l], 'hy_b1': hy_b1[l], 'hy_w2': hy_w2[l], 'hy_b2': hy_b2[l],
            'hy_w3': hy_w3[l], 'hy_decay': hy_decay[l], 'hy_bias': hy_bias[l],
            'w_branch_a': w_branch_a[l], 'w_branch_b': w_branch_b[l], 'w_out': w_out[l],
            'ffn_w_up': ffn_w_up[l], 'ffn_conv_w': ffn_conv_w[l], 'ffn_conv_b': ffn_conv_b[l],
            'ffn_w_down': ffn_w_down[l],
        }
        y_prompt, st_ctx = _layer(y_prompt, cond_ctx, s0_ctx, lbs[l], False, lp)
        new_states.append(st_ctx)
        y_sample, _ = _layer(y_sample, c, state_hgrn[:, l], lbs[l], True, lp)
    new_state_hgrn = jnp.stack(new_states, axis=1).astype(x_prompt.dtype)
    return (y_prompt, y_sample, new_state_hgrn)
```

```python
import functools
import math

import numpy as np
import jax
import jax.numpy as jnp
from jax import lax
from jax.experimental import pallas as pl
from jax.experimental.pallas import tpu as pltpu

F32 = jnp.float32
BF16 = jnp.bfloat16

D_MODEL = 2048
BATCH = 16
SEQ = 256
DEPTH = 2
DEC_BATCH = 4
DEC_SEQ = 1024
GRID_W = 64
D_A = D_MODEL // 2
D_B = D_MODEL // 2
HGRN_EXPAND = 128
H_A = D_A // HGRN_EXPAND
DK_A = HGRN_EXPAND
DV_A = D_A // H_A
HY_ORDER = 2
HY_EMB = 33
HY_BANDS = (HY_EMB - 1) // 2
HY_HID = 64
HY_SIN_W = 1.0
D_FF = 5632
N_MOD = 6 * D_MODEL
N_PROJ = 5 * D_A + 3 * D_B + 2 * D_MODEL
EPS = 1e-6

N_CTX = BATCH * SEQ
N_LAT = DEC_BATCH * DEC_SEQ
N_TOK = N_CTX + N_LAT
TOK_BLOCK = 1024
N_TOK_BLOCKS = N_TOK // TOK_BLOCK
N_CTX_BLOCKS = N_CTX // TOK_BLOCK

COL_Q, COL_FA, COL_FB, COL_I, COL_G = 0, D_A, 2 * D_A, 3 * D_A, 4 * D_A
COL_HY = 5 * D_A
COL_GATE_A = 5 * D_A + 3 * D_B
COL_GATE_B = COL_GATE_A + D_MODEL

LANES = 128
SCAN_CHUNK = 32
HY_CB = 256
VMEM_LIMIT = 56 * 1024 * 1024


def _params(n_axes):
    return pltpu.CompilerParams(dimension_semantics=("arbitrary",) * n_axes,
                                vmem_limit_bytes=VMEM_LIMIT)


def _silu(x):
    return x * jax.nn.sigmoid(x)


def _rms(x):
    return x * lax.rsqrt(jnp.mean(x * x, axis=-1, keepdims=True) + EPS)


def _mod_spec(layer, chunk, tm):
    return pl.BlockSpec((None, None, 1, D_MODEL),
                        lambda i, *_: (layer, (i * tm) // TOK_BLOCK, 0, chunk))


def _gain_spec(layer):
    return pl.BlockSpec((None, 1, D_MODEL), lambda *_: (layer, 0, 0))


def _mod_kernel(cond_ref, w_ref, b_ref, o_ref):
    s = _silu(cond_ref[...]).astype(BF16)
    o_ref[...] = jnp.dot(s, w_ref[...].astype(BF16), preferred_element_type=F32) + b_ref[...]


def _modulation(cond, w_mod, b_mod):
    tn = 1024
    return pl.pallas_call(
        _mod_kernel,
        grid=(DEPTH, N_MOD // tn),
        in_specs=[pl.BlockSpec((N_TOK_BLOCKS, D_MODEL), lambda l, j: (0, 0)),
                  pl.BlockSpec((None, D_MODEL, tn), lambda l, j: (l, 0, j)),
                  pl.BlockSpec((None, 1, tn), lambda l, j: (l, 0, j))],
        out_specs=pl.BlockSpec((None, N_TOK_BLOCKS, tn), lambda l, j: (l, 0, j)),
        out_shape=jax.ShapeDtypeStruct((DEPTH, N_TOK_BLOCKS, N_MOD), F32),
        compiler_params=_params(2),
        name="modulation",
    )(cond, w_mod, b_mod.reshape(DEPTH, 1, N_MOD))


def _prenorm_kernel(x_ref, g_ref, sh_ref, sc_ref, h_ref):
    h = (_rms(x_ref[...]) * g_ref[...]) * (1.0 + sc_ref[...]) + sh_ref[...]
    h_ref[...] = h.astype(BF16)


def _prenorm(x, g, mod4, layer):
    tm = 512
    return pl.pallas_call(
        _prenorm_kernel,
        grid=(N_TOK // tm,),
        in_specs=[pl.BlockSpec((tm, D_MODEL), lambda i: (i, 0)),
                  _gain_spec(layer), _mod_spec(layer, 0, tm), _mod_spec(layer, 1, tm)],
        out_specs=pl.BlockSpec((tm, D_MODEL), lambda i: (i, 0)),
        out_shape=jax.ShapeDtypeStruct((N_TOK, D_MODEL), BF16),
        compiler_params=_params(1),
        name="prenorm",
    )(x, g, mod4, mod4)


def _inproj_kernel(h_ref, w_ref, o_ref, wbf_ref):
    @pl.when(pl.program_id(1) == 0)
    def _():
        wbf_ref[...] = w_ref[...].astype(BF16)

    o_ref[...] = jnp.dot(h_ref[...], wbf_ref[...], preferred_element_type=F32)


def _in_proj(h, w_in, layer):
    tm, tn = 1024, 1024
    return pl.pallas_call(
        _inproj_kernel,
        grid=(N_PROJ // tn, N_TOK // tm),
        in_specs=[pl.BlockSpec((tm, D_MODEL), lambda j, i: (i, 0)),
                  pl.BlockSpec((None, D_MODEL, tn), lambda j, i: (layer, 0, j))],
        out_specs=pl.BlockSpec((tm, tn), lambda j, i: (i, j)),
        out_shape=jax.ShapeDtypeStruct((N_TOK, N_PROJ), F32),
        scratch_shapes=[pltpu.VMEM((D_MODEL, tn), BF16)],
        compiler_params=_params(2),
        name="in_proj",
    )(h, w_in)


def _scan_levels(chunk):
    m, levels = chunk // 2, []
    while m >= 1:
        levels.append(m)
        m //= 2
    return tuple(levels)


@functools.lru_cache(maxsize=None)
def _scan_tables(chunk):
    levels = _scan_levels(chunk)
    amats, rowsels = [], []
    for rev in (False, True):
        blocks, sels = [], []
        for m in levels:
            a = np.zeros((chunk, chunk), np.float32)
            sel = np.zeros((chunk,), np.float32)
            for t in range(chunk):
                mid = 2 * m * (t // (2 * m)) + m - 1
                upper = t > mid
                if not rev:
                    if upper:
                        a[t, mid + 1:t + 1] = 1.0
                    else:
                        a[t, t + 1:mid + 1] = 1.0
                    sel[t] = 1.0 if upper else 0.0
                else:
                    if upper:
                        a[t, mid + 1:t] = 1.0
                    else:
                        a[t, t:mid + 1] = 1.0
                    sel[t] = 0.0 if upper else 1.0
            blocks.append(a)
            sels.append(np.repeat(sel[:, None], LANES, axis=1))
        idx = np.arange(chunk)
        if not rev:
            inter = (idx[None, :] <= idx[:, None]).astype(np.float32)
            carry = (idx[None, :] > idx[:, None]).astype(np.float32)
        else:
            inter = (idx[None, :] >= idx[:, None]).astype(np.float32)
            carry = (idx[None, :] < idx[:, None]).astype(np.float32)
        blocks += [inter, carry, np.ones((16, chunk), np.float32)]
        amats.append(np.concatenate(blocks, axis=0))
        rowsels.append(np.stack(sels))
    pair = []
    idx = np.arange(chunk)
    for m in levels:
        pair.append((idx[:, None] // (2 * m) == idx[None, :] // (2 * m)).astype(np.float32))
    pair.append(np.eye(chunk, dtype=np.float32))
    return np.stack(amats), np.stack(rowsels), np.stack(pair)


def _dot_exact_rhs(a_bf16, x):
    x1 = x.astype(BF16)
    r1 = x - x1.astype(F32)
    x2 = r1.astype(BF16)
    x3 = (r1 - x2.astype(F32)).astype(BF16)
    n = x.shape[1]
    y = jnp.dot(a_bf16, jnp.concatenate([x1, x2, x3], axis=1), preferred_element_type=F32)
    return y[:, :n] + y[:, n:2 * n] + y[:, 2 * n:]


_NT = (((1,), (1,)), ((), ()))
_TN = (((0,), (0,)), ((), ()))


def _hgrn_kernel(*refs, seq_len, chunk, has_s0):
    if has_s0:
        (q_ref, fa_ref, fb_ref, i_ref, g_ref, lb_ref, nw_ref, amat_ref, rowsel_ref, pair_ref, s0_ref,
         o_ref, sfin_ref, of_scr, ob_scr, st_scr) = refs
    else:
        (q_ref, fa_ref, fb_ref, i_ref, g_ref, lb_ref, nw_ref, amat_ref, rowsel_ref, pair_ref,
         o_ref, sfin_ref, of_scr, ob_scr, st_scr) = refs
    n_chunks = seq_len // chunk
    n_lev = len(_scan_levels(chunk))

    for d in range(2):
        if has_s0:
            st_scr[d] = s0_ref[d].T
        else:
            st_scr[d] = jnp.zeros((DV_A, DK_A), F32)

    def chunk_step(d, f_ref, row0, o_scr):
        rows = pl.ds(row0, chunk)
        z = f_ref[rows, :]
        q = _silu(q_ref[rows, :])
        v = i_ref[rows, :]
        lbd = lb_ref[d:d + 1, :]
        log_sig = jnp.minimum(z, 0.0) - jnp.log1p(jnp.exp(-jnp.abs(z)))
        a1 = jnp.log(lbd)
        a2 = jnp.log1p(-lbd) + log_sig
        logf = jnp.maximum(a1, a2) + jnp.log1p(jnp.exp(-jnp.abs(a1 - a2)))
        k = (1.0 - lbd) * jax.nn.sigmoid(-z)

        w = jnp.exp(_dot_exact_rhs(amat_ref[d], logf))
        att = lax.dot_general(q.astype(BF16), k.astype(BF16), _NT,
                              preferred_element_type=F32) * pair_ref[n_lev]
        for lv in range(n_lev):
            wl = w[lv * chunk:(lv + 1) * chunk]
            sel = rowsel_ref[d, lv]
            qs = (q * wl * sel).astype(BF16)
            ks = (k * wl * (1.0 - sel)).astype(BF16)
            att = att + lax.dot_general(qs, ks, _NT, preferred_element_type=F32) * pair_ref[lv]
        wq = w[n_lev * chunk:(n_lev + 1) * chunk]
        wk = w[(n_lev + 1) * chunk:(n_lev + 2) * chunk]
        wall = w[(n_lev + 2) * chunk:(n_lev + 2) * chunk + 1]
        st = st_scr[d]
        vb = v.astype(BF16)
        o = lax.dot_general((q * wq).astype(BF16), st.astype(BF16), _NT, preferred_element_type=F32)
        o = o + jnp.dot(att.astype(BF16), vb, preferred_element_type=F32)
        o_scr[rows, :] = o
        dst = lax.dot_general(vb, (k * wk).astype(BF16), _TN, preferred_element_type=F32)
        st_scr[d] = st * wall + dst

    def body(j, carry):
        chunk_step(0, fa_ref, pl.multiple_of(j * chunk, chunk), of_scr)
        chunk_step(1, fb_ref, pl.multiple_of((n_chunks - 1 - j) * chunk, chunk), ob_scr)
        return carry

    lax.fori_loop(0, n_chunks, body, 0)

    o = of_scr[...] + ob_scr[...]
    y = _rms(o) * nw_ref[...]
    o_ref[...] = (y * _silu(g_ref[...])).astype(BF16)
    for d in range(2):
        sfin_ref[d] = st_scr[d].T


def _hgrn_mixer(proj, lb, norm_w, s0, layer, *, row_block0, batch, seq_len):
    chunk = SCAN_CHUNK
    amat, rowsel, pair = _scan_tables(chunk)
    amat = jnp.asarray(amat).astype(BF16)
    rowsel = jnp.asarray(rowsel)
    pair = jnp.asarray(pair)
    has_s0 = s0 is not None

    def col_spec(col):
        return pl.BlockSpec((seq_len, LANES), lambda b, h: (row_block0 + b, col // LANES + h))

    def const_spec(a):
        nd = a.ndim
        return pl.BlockSpec(a.shape, lambda b, h: (0,) * nd)

    in_specs = [col_spec(COL_Q), col_spec(COL_FA), col_spec(COL_FB), col_spec(COL_I), col_spec(COL_G),
                pl.BlockSpec((2, LANES), lambda b, h: (0, h)),
                pl.BlockSpec((1, LANES), lambda b, h: (0, 0)),
                const_spec(amat), const_spec(rowsel), const_spec(pair)]
    args = [proj, proj, proj, proj, proj, lb, norm_w.reshape(1, DV_A), amat, rowsel, pair]
    if has_s0:
        in_specs.append(pl.BlockSpec((None, None, 2, None, DK_A, DV_A),
                                     lambda b, h: (b, layer, 0, h, 0, 0)))
        args.append(s0)
    kern = functools.partial(_hgrn_kernel, seq_len=seq_len, chunk=chunk, has_s0=has_s0)
    return pl.pallas_call(
        kern,
        grid=(batch, H_A),
        in_specs=in_specs,
        out_specs=[pl.BlockSpec((seq_len, LANES), lambda b, h: (b, h)),
                   pl.BlockSpec((None, 2, None, DK_A, DV_A), lambda b, h: (b, 0, h, 0, 0))],
        out_shape=[jax.ShapeDtypeStruct((batch * seq_len, D_A), BF16),
                   jax.ShapeDtypeStruct((batch, 2, H_A, DK_A, DV_A), F32)],
        scratch_shapes=[pltpu.VMEM((seq_len, DV_A), F32), pltpu.VMEM((seq_len, DV_A), F32),
                        pltpu.VMEM((2, DV_A, DK_A), F32)],
        compiler_params=_params(2),
        name="hgrn_mixer",
    )(*args)


@functools.lru_cache(maxsize=None)
def _hyena_tables(seq_len):
    n = seq_len
    t = np.arange(n, dtype=np.float64)
    t01 = t / max(n - 1, 1)
    bands = np.linspace(1e-4, HY_BANDS - 1, HY_BANDS)
    ang = (2.0 * math.pi / n) * t[:, None] * bands[None, :]
    feats = np.concatenate([t01[:, None], np.cos(ang), -np.sin(ang)], axis=-1)
    feats = np.pad(feats, ((0, 0), (0, LANES - HY_EMB))).astype(np.float32)
    ft = np.outer(np.arange(n), np.arange(n)) % (2 * n)
    fc = np.cos(math.pi * ft / n)
    fs = -np.sin(math.pi * ft / n)
    fs[0, :] = 1.0 - 2.0 * (np.arange(n) % 2)
    return feats, fc.astype(np.float32), fs.astype(np.float32), fs.T.astype(np.float32).copy()


def _hyena_filter_kernel(feats_ref, w1_ref, b1_ref, w2_ref, b2_ref, w3_ref0, w3_ref1, w3_ref2, w3_ref3,
                         decay_ref, a_ref, bm_ref, *, seq_len):
    hp = lax.Precision.HIGHEST
    h = jnp.sin(HY_SIN_W * (jnp.dot(feats_ref[...], w1_ref[...], precision=hp,
                                    preferred_element_type=F32) + b1_ref[...]))
    h = jnp.sin(HY_SIN_W * (jnp.dot(h, w2_ref[...], precision=hp,
                                    preferred_element_type=F32) + b2_ref[...]))
    w3 = ((w3_ref0, w3_ref1), (w3_ref2, w3_ref3))
    cb = a_ref.shape[-1]
    row = lax.broadcasted_iota(jnp.int32, (seq_len, cb), 0)
    t01 = row.astype(F32) / float(max(seq_len - 1, 1))
    for o in range(HY_ORDER):
        win = jnp.exp(-t01 * jnp.abs(decay_ref[o:o + 1, :]))
        hf = jnp.dot(h, w3[o][0][...], precision=hp, preferred_element_type=F32) * win
        hb = jnp.dot(h, w3[o][1][...], precision=hp, preferred_element_type=F32) * win
        mag = jnp.where(row == 0, jnp.abs(hf + hb), jnp.abs(hf) + jnp.abs(hb))
        inv = 1.0 / (jnp.sum(mag, axis=0, keepdims=True) + EPS)
        a_ref[o] = (hf + hb) * inv
        bm_ref[o] = (hf - hb) * inv


def _hyena_filters(seq_len, w1, b1, w2, b2, w3, decay):
    feats = jnp.asarray(_hyena_tables(seq_len)[0])
    w1p = jnp.pad(w1, ((0, LANES - HY_EMB), (0, 0)))
    cb = HY_CB
    nb = D_B // cb

    def w3_spec(k):
        return pl.BlockSpec((HY_HID, cb), lambda c: (0, k * nb + c))

    full2 = lambda a: pl.BlockSpec(a.shape, lambda c: (0, 0))
    b1r, b2r = b1.reshape(1, HY_HID), b2.reshape(1, HY_HID)
    out_sds = jax.ShapeDtypeStruct((HY_ORDER, seq_len, D_B), F32)
    return pl.pallas_call(
        functools.partial(_hyena_filter_kernel, seq_len=seq_len),
        grid=(nb,),
        in_specs=[full2(feats), full2(w1p), full2(b1r), full2(w2), full2(b2r),
                  w3_spec(0), w3_spec(1), w3_spec(2), w3_spec(3),
                  pl.BlockSpec((HY_ORDER, cb), lambda c: (0, c))],
        out_specs=[pl.BlockSpec((HY_ORDER, seq_len, cb), lambda c: (0, 0, c))] * 2,
        out_shape=[out_sds, out_sds],
        compiler_params=_params(1),
        name="hyena_filter",
    )(feats, w1p, b1r, w2, b2r, w3, w3, w3, w3, decay)


def _dft(f_hi_ref, f_lo_ref, x, passes):
    xh = x.astype(BF16)
    acc = jnp.dot(f_hi_ref[...], xh, preferred_element_type=F32)
    if passes >= 2:
        xl = (x - xh.astype(F32)).astype(BF16)
        acc = acc + jnp.dot(f_hi_ref[...], xl, preferred_element_type=F32)
    if passes >= 3:
        acc = acc + jnp.dot(f_lo_ref[...], xh, preferred_element_type=F32)
    return acc


def _hyena_conv_kernel(v_ref, x1_ref, x2_ref, cwv_ref, cw1_ref, cw2_ref, cbv_ref, cb1_ref, cb2_ref,
                       a_ref, bm_ref, bias_ref, fc_hi, fc_lo, fs_hi, fs_lo, fst_hi, fst_lo,
                       o_ref, p_scr, *, seq_len, data_passes):
    n = seq_len
    cb = o_ref.shape[-1]
    row = lax.broadcasted_iota(jnp.int32, (n, cb), 0)
    scale = 1.0 / (2.0 * n)

    @pl.when(pl.program_id(1) == 0)
    def _():
        wts = jnp.where(row == 0, scale, 2.0 * scale)
        sgn = (1 - 2 * (row & 1)).astype(F32)
        for o in range(HY_ORDER):
            a = a_ref[o]
            kre = _dft(fc_hi, fc_lo, a, 3)
            kim = _dft(fs_hi, fs_lo, bm_ref[o], 3)
            knyq = jnp.sum(a * sgn, axis=0, keepdims=True)
            p_scr[3 * o + 0] = kre * wts
            p_scr[3 * o + 1] = jnp.where(row == 0, 0.0, kim * wts)
            p_scr[3 * o + 2] = jnp.where(row == 0, knyq * scale, kre * wts)

    def dwconv(x_ref, w_ref, b_ref):
        x = x_ref[...]
        xm = jnp.where(row == 0, 0.0, pltpu.roll(x, 1, axis=0))
        xp = jnp.where(row == n - 1, 0.0, pltpu.roll(x, n - 1, axis=0))
        return xm * w_ref[0:1, :] + x * w_ref[1:2, :] + xp * w_ref[2:3, :] + b_ref[...]

    def fftconv(z, o):
        zt = _dft(fc_hi, fc_lo, z, data_passes)
        zb = _dft(fs_hi, fs_lo, z, data_passes)
        p1, p3, p4 = p_scr[3 * o], p_scr[3 * o + 1], p_scr[3 * o + 2]
        yt = zt * p1 - zb * p3
        yb = zt * p3 + zb * p4
        y = _dft(fc_hi, fc_lo, yt, data_passes) + _dft(fst_hi, fst_lo, yb, data_passes)
        return y + z * bias_ref[o:o + 1, :]

    v = dwconv(v_ref, cwv_ref, cbv_ref)
    x1 = dwconv(x1_ref, cw1_ref, cb1_ref)
    z2 = x1 * fftconv(v, 0)
    x2 = dwconv(x2_ref, cw2_ref, cb2_ref)
    o_ref[...] = (x2 * fftconv(z2, 1)).astype(BF16)


def _hyena_mixer(proj, conv_w, conv_b, a_filt, bm_filt, bias, *, row_block0, batch, seq_len,
                 data_passes):
    _, fc, fs, fst = _hyena_tables(seq_len)
    mats = []
    for m in (fc, fs, fst):
        m = jnp.asarray(m)
        hi = m.astype(BF16)
        mats += [hi, (m - hi.astype(F32)).astype(BF16)]
    cb = HY_CB
    nb = D_B // cb

    def col_spec(k):
        return pl.BlockSpec((seq_len, cb), lambda c, b: (row_block0 + b, (COL_HY + k * D_B) // cb + c))

    def cw_spec(k):
        return pl.BlockSpec((3, cb), lambda c, b: (0, k * nb + c))

    def cb_spec(k):
        return pl.BlockSpec((1, cb), lambda c, b: (0, k * nb + c))

    filt_spec = pl.BlockSpec((HY_ORDER, seq_len, cb), lambda c, b: (0, 0, c))
    mat_spec = pl.BlockSpec((seq_len, seq_len), lambda c, b: (0, 0))
    conv_b2 = conv_b.reshape(1, 3 * D_B)
    return pl.pallas_call(
        functools.partial(_hyena_conv_kernel, seq_len=seq_len, data_passes=data_passes),
        grid=(nb, batch),
        in_specs=[col_spec(0), col_spec(1), col_spec(2), cw_spec(0), cw_spec(1), cw_spec(2),
                  cb_spec(0), cb_spec(1), cb_spec(2), filt_spec, filt_spec,
                  pl.BlockSpec((HY_ORDER, cb), lambda c, b: (0, c))] + [mat_spec] * 6,
        out_specs=pl.BlockSpec((seq_len, cb), lambda c, b: (b, c)),
        out_shape=jax.ShapeDtypeStruct((batch * seq_len, D_B), BF16),
        scratch_shapes=[pltpu.VMEM((3 * HY_ORDER, seq_len, cb), F32)],
        compiler_params=_params(2),
        name="hyena_conv",
    )(proj, proj, proj, conv_w, conv_w, conv_w, conv_b2, conv_b2, conv_b2, a_filt, bm_filt, bias, *mats)


def _merge_kernel(oa_ref, ob_ref, ga_ref, gb_ref, wa_ref, wb_ref, o_ref, wabf, wbbf):
    @pl.when(pl.program_id(1) == 0)
    def _():
        wabf[...] = wa_ref[...].astype(BF16)
        wbbf[...] = wb_ref[...].astype(BF16)

    ya = jnp.dot(oa_ref[...], wabf[...], preferred_element_type=F32)
    yb = jnp.dot(ob_ref[...], wbbf[...], preferred_element_type=F32)
    o_ref[...] = (jax.nn.sigmoid(ga_ref[...]) * ya + jax.nn.sigmoid(gb_ref[...]) * yb).astype(BF16)


def _merge(oa, ob, proj, w_a, w_b, layer):
    tm, tn = 1024, 512
    nbn = D_MODEL // tn
    return pl.pallas_call(
        _merge_kernel,
        grid=(nbn, N_TOK // tm),
        in_specs=[pl.BlockSpec((tm, D_A), lambda j, i: (i, 0)),
                  pl.BlockSpec((tm, D_B), lambda j, i: (i, 0)),
                  pl.BlockSpec((tm, tn), lambda j, i: (i, COL_GATE_A // tn + j)),
                  pl.BlockSpec((tm, tn), lambda j, i: (i, COL_GATE_B // tn + j)),
                  pl.BlockSpec((None, D_A, tn), lambda j, i: (layer, 0, j)),
                  pl.BlockSpec((None, D_B, tn), lambda j, i: (layer, 0, j))],
        out_specs=pl.BlockSpec((tm, tn), lambda j, i: (i, j)),
        out_shape=jax.ShapeDtypeStruct((N_TOK, D_MODEL), BF16),
        scratch_shapes=[pltpu.VMEM((D_A, tn), BF16), pltpu.VMEM((D_B, tn), BF16)],
        compiler_params=_params(2),
        name="merge",
    )(oa, ob, proj, proj, w_a, w_b)


def _resid_kernel(*refs, has_next):
    if has_next:
        (a_ref, w_ref, x_ref, gpost_ref, gate_ref, gnext_ref, sh_ref, sc_ref,
         xo_ref, h_ref, acc_ref) = refs
    else:
        a_ref, w_ref, x_ref, gpost_ref, gate_ref, xo_ref, acc_ref = refs
    k = pl.program_id(1)

    @pl.when(k == 0)
    def _():
        acc_ref[...] = jnp.zeros_like(acc_ref)

    acc_ref[...] += jnp.dot(a_ref[...], w_ref[...].astype(BF16), preferred_element_type=F32)

    @pl.when(k == pl.num_programs(1) - 1)
    def _():
        xn = x_ref[...] + gate_ref[...] * (_rms(acc_ref[...]) * gpost_ref[...])
        xo_ref[...] = xn
        if has_next:
            h = (_rms(xn) * gnext_ref[...]) * (1.0 + sc_ref[...]) + sh_ref[...]
            h_ref[...] = h.astype(BF16)


def _proj_residual(a, w, x, g_post, mod4, layer, gate_chunk, nxt):
    tm, tk = 512, 512
    kdim = a.shape[1]
    has_next = nxt is not None
    row_spec = pl.BlockSpec((tm, D_MODEL), lambda i, k: (i, 0))
    in_specs = [pl.BlockSpec((tm, tk), lambda i, k: (i, k)),
                pl.BlockSpec((None, tk, D_MODEL), lambda i, k: (layer, k, 0)),
                row_spec,
                _gain_spec(layer), _mod_spec(layer, gate_chunk, tm)]
    args = [a, w, x, g_post, mod4]
    out_specs = [row_spec]
    out_shape = [jax.ShapeDtypeStruct((N_TOK, D_MODEL), F32)]
    if has_next:
        g_next, nl, sh_chunk, sc_chunk = nxt
        in_specs += [_gain_spec(nl), _mod_spec(nl, sh_chunk, tm), _mod_spec(nl, sc_chunk, tm)]
        args += [g_next, mod4, mod4]
        out_specs.append(row_spec)
        out_shape.append(jax.ShapeDtypeStruct((N_TOK, D_MODEL), BF16))
    res = pl.pallas_call(
        functools.partial(_resid_kernel, has_next=has_next),
        grid=(N_TOK // tm, kdim // tk),
        in_specs=in_specs,
        out_specs=out_specs,
        out_shape=out_shape,
        scratch_shapes=[pltpu.VMEM((tm, D_MODEL), F32)],
        compiler_params=_params(2),
        name="proj_residual",
    )(*args)
    return res if has_next else (res[0], None)


def _ffn_up_kernel(h_ref, wa_ref, wv_ref, cwa_ref, cwv_ref, cba_ref, cbv_ref, o_ref, wabf, wvbf):
    i = pl.program_id(1)

    @pl.when(i == 0)
    def _():
        wabf[...] = wa_ref[...].astype(BF16)
        wvbf[...] = wv_ref[...].astype(BF16)

    h = h_ref[...]
    ua = jnp.dot(h, wabf[...], preferred_element_type=F32)
    uv = jnp.dot(h, wvbf[...], preferred_element_type=F32)
    tm = ua.shape[0]
    t = lax.broadcasted_iota(jnp.int32, ua.shape, 0)

    def shifted(u, first, last):
        um = jnp.where(first, 0.0, pltpu.roll(u, 1, axis=0))
        up = jnp.where(last, 0.0, pltpu.roll(u, tm - 1, axis=0))
        return um, up

    def seq_conv(u, cw_ref, cb_ref):
        pos = t & (SEQ - 1)
        um, up = shifted(u, pos == 0, pos == SEQ - 1)
        return um * cw_ref[3:4, :] + u * cw_ref[4:5, :] + up * cw_ref[5:6, :] + cb_ref[...]

    def grid_conv(u, cw_ref, cb_ref):
        col = t & (GRID_W - 1)
        um, up = shifted(u, col == 0, col == GRID_W - 1)
        acc = cb_ref[...]
        for dr in (-1, 0, 1):
            r = 3 * (dr + 1)
            s = um * cw_ref[r:r + 1, :] + u * cw_ref[r + 1:r + 2, :] + up * cw_ref[r + 2:r + 3, :]
            if dr == -1:
                s = jnp.where(t < GRID_W, 0.0, pltpu.roll(s, GRID_W, axis=0))
            elif dr == 1:
                s = jnp.where(t >= tm - GRID_W, 0.0, pltpu.roll(s, tm - GRID_W, axis=0))
            acc = acc + s
        return acc

    @pl.when(i < N_CTX_BLOCKS)
    def _():
        a = seq_conv(ua, cwa_ref, cba_ref)
        vv = seq_conv(uv, cwv_ref, cbv_ref)
        o_ref[...] = (_silu(a) * vv).astype(BF16)

    @pl.when(i >= N_CTX_BLOCKS)
    def _():
        a = grid_conv(ua, cwa_ref, cba_ref)
        vv = grid_conv(uv, cwv_ref, cbv_ref)
        o_ref[...] = (_silu(a) * vv).astype(BF16)


def _ffn_up(h, w_up, conv_w, conv_b, layer):
    tm, tn = TOK_BLOCK, 512
    nbn = D_FF // tn
    conv_w9 = conv_w.reshape(DEPTH, 9, 2 * D_FF)
    conv_b2 = conv_b.reshape(DEPTH, 1, 2 * D_FF)
    return pl.pallas_call(
        _ffn_up_kernel,
        grid=(nbn, N_TOK // tm),
        in_specs=[pl.BlockSpec((tm, D_MODEL), lambda j, i: (i, 0)),
                  pl.BlockSpec((None, D_MODEL, tn), lambda j, i: (layer, 0, j)),
                  pl.BlockSpec((None, D_MODEL, tn), lambda j, i: (layer, 0, nbn + j)),
                  pl.BlockSpec((None, 9, tn), lambda j, i: (layer, 0, j)),
                  pl.BlockSpec((None, 9, tn), lambda j, i: (layer, 0, nbn + j)),
                  pl.BlockSpec((None, 1, tn), lambda j, i: (layer, 0, j)),
                  pl.BlockSpec((None, 1, tn), lambda j, i: (layer, 0, nbn + j))],
        out_specs=pl.BlockSpec((tm, tn), lambda j, i: (i, j)),
        out_shape=jax.ShapeDtypeStruct((N_TOK, D_FF), BF16),
        scratch_shapes=[pltpu.VMEM((D_MODEL, tn), BF16), pltpu.VMEM((D_MODEL, tn), BF16)],
        compiler_params=_params(2),
        name="ffn_up",
    )(h, w_up, w_up, conv_w9, conv_w9, conv_b2, conv_b2)


def kernel(x_prompt, x_sample, state_hgrn, c, c_ctx, w_mod, b_mod, g_pre_mix, g_post_mix, g_pre_ffn,
           g_post_ffn, w_in, hgrn_lower_bounds, hgrn_norm, hy_conv_w, hy_conv_b, hy_w1, hy_b1, hy_w2,
           hy_b2, hy_w3, hy_decay, hy_bias, w_branch_a, w_branch_b, w_out, ffn_w_up, ffn_conv_w,
           ffn_conv_b, ffn_w_down):
    p_lb = jax.nn.softmax(hgrn_lower_bounds.astype(F32), axis=0)
    cs = jnp.cumsum(p_lb, axis=0)
    lbs = cs - cs[:1]

    x = jnp.concatenate([x_prompt.reshape(N_CTX, D_MODEL), x_sample.reshape(N_LAT, D_MODEL)], axis=0)
    cond = jnp.concatenate([jnp.broadcast_to(c_ctx[None, :], (N_CTX_BLOCKS, D_MODEL)), c], axis=0)
    mod4 = _modulation(cond, w_mod, b_mod).reshape(DEPTH, N_TOK_BLOCKS, 1, N_MOD)

    g_pre_mix, g_post_mix, g_pre_ffn, g_post_ffn = (
        g.reshape(DEPTH, 1, D_MODEL) for g in (g_pre_mix, g_post_mix, g_pre_ffn, g_post_ffn))

    h = _prenorm(x, g_pre_mix, mod4, 0)
    new_states = []
    for l in range(DEPTH):
        proj = _in_proj(h, w_in, l)
        oa_ctx, s_ctx = _hgrn_mixer(proj, lbs[l], hgrn_norm[l], None, l,
                                    row_block0=0, batch=BATCH, seq_len=SEQ)
        oa_lat, _ = _hgrn_mixer(proj, lbs[l], hgrn_norm[l], state_hgrn, l,
                                row_block0=N_CTX // DEC_SEQ, batch=DEC_BATCH, seq_len=DEC_SEQ)
        new_states.append(s_ctx)
        ob = []
        for row_block0, batch, seq_len in ((0, BATCH, SEQ), (N_CTX // DEC_SEQ, DEC_BATCH, DEC_SEQ)):
            a_filt, bm_filt = _hyena_filters(seq_len, hy_w1[l], hy_b1[l], hy_w2[l], hy_b2[l], hy_w3[l],
                                             hy_decay[l])
            ob.append(_hyena_mixer(proj, hy_conv_w[l], hy_conv_b[l], a_filt, bm_filt, hy_bias[l],
                                   row_block0=row_block0, batch=batch, seq_len=seq_len, data_passes=1))
        oa = jnp.concatenate([oa_ctx, oa_lat], axis=0)
        ob = jnp.concatenate(ob, axis=0)
        merged = _merge(oa, ob, proj, w_branch_a, w_branch_b, l)
        x, h2 = _proj_residual(merged, w_out, x, g_post_mix, mod4, l, 2, (g_pre_ffn, l, 3, 4))
        g = _ffn_up(h2, ffn_w_up, ffn_conv_w, ffn_conv_b, l)
        nxt = (g_pre_mix, l + 1, 0, 1) if l + 1 < DEPTH else None
        x, h = _proj_residual(g, ffn_w_down, x, g_post_ffn, mod4, l, 5, nxt)

    y_prompt = x[:N_CTX].reshape(BATCH, SEQ, D_MODEL)
    y_sample = x[N_CTX:].reshape(DEC_BATCH, DEC_SEQ, D_MODEL)
    new_state = jnp.stack(new_states, axis=1).astype(x_prompt.dtype)
    return (y_prompt, y_sample, new_state)
```

```python
import functools
import math

import numpy as np
import jax
import jax.numpy as jnp
from jax import lax
from jax.experimental import pallas as pl
from jax.experimental.pallas import tpu as pltpu

F32 = jnp.float32
BF16 = jnp.bfloat16

D_MODEL = 2048
BATCH = 16
SEQ = 256
DEPTH = 2
DEC_BATCH = 4
DEC_SEQ = 1024
GRID_W = 64
D_A = D_MODEL // 2
D_B = D_MODEL // 2
HGRN_EXPAND = 128
H_A = D_A // HGRN_EXPAND
DK_A = HGRN_EXPAND
DV_A = D_A // H_A
HY_ORDER = 2
HY_EMB = 33
HY_BANDS = (HY_EMB - 1) // 2
HY_HID = 64
HY_SIN_W = 1.0
D_FF = 5632
N_MOD = 6 * D_MODEL
N_PROJ = 5 * D_A + 3 * D_B + 2 * D_MODEL
EPS = 1e-6

N_CTX = BATCH * SEQ
N_LAT = DEC_BATCH * DEC_SEQ
N_TOK = N_CTX + N_LAT
TOK_BLOCK = 1024
N_TOK_BLOCKS = N_TOK // TOK_BLOCK
N_CTX_BLOCKS = N_CTX // TOK_BLOCK

COL_Q, COL_FA, COL_FB, COL_I, COL_G = 0, D_A, 2 * D_A, 3 * D_A, 4 * D_A
COL_HY = 5 * D_A
COL_GATE_A = 5 * D_A + 3 * D_B
COL_GATE_B = COL_GATE_A + D_MODEL

LANES = 128
SCAN_CHUNK = 32
SCAN_UNROLL = 4
HY_CB = 256
FFN_COL_GROUP = 256
EPILOGUE_ROWS = 128
VMEM_LIMIT = 56 * 1024 * 1024


def _params(n_axes):
    return pltpu.CompilerParams(dimension_semantics=("arbitrary",) * n_axes,
                                vmem_limit_bytes=VMEM_LIMIT)


def _silu(x):
    return x * jax.nn.sigmoid(x)


def _rms(x):
    return x * lax.rsqrt(jnp.mean(x * x, axis=-1, keepdims=True) + EPS)


def _mod_spec(layer, chunk, tm):
    return pl.BlockSpec((None, None, 1, D_MODEL),
                        lambda i, *_: (layer, (i * tm) // TOK_BLOCK, 0, chunk))


def _gain_spec(layer):
    return pl.BlockSpec((None, 1, D_MODEL), lambda *_: (layer, 0, 0))


def _mod_kernel(cond_ref, w_ref, b_ref, o_ref):
    s = _silu(cond_ref[...]).astype(BF16)
    o_ref[...] = jnp.dot(s, w_ref[...].astype(BF16), preferred_element_type=F32) + b_ref[...]


def _modulation(cond, w_mod, b_mod):
    tn = 1024
    return pl.pallas_call(
        _mod_kernel,
        grid=(DEPTH, N_MOD // tn),
        in_specs=[pl.BlockSpec((N_TOK_BLOCKS, D_MODEL), lambda l, j: (0, 0)),
                  pl.BlockSpec((None, D_MODEL, tn), lambda l, j: (l, 0, j)),
                  pl.BlockSpec((None, 1, tn), lambda l, j: (l, 0, j))],
        out_specs=pl.BlockSpec((None, N_TOK_BLOCKS, tn), lambda l, j: (l, 0, j)),
        out_shape=jax.ShapeDtypeStruct((DEPTH, N_TOK_BLOCKS, N_MOD), F32),
        compiler_params=_params(2),
        name="modulation",
    )(cond, w_mod, b_mod.reshape(DEPTH, 1, N_MOD))


def _prenorm_kernel(x_ref, g_ref, sh_ref, sc_ref, h_ref):
    h = (_rms(x_ref[...]) * g_ref[...]) * (1.0 + sc_ref[...]) + sh_ref[...]
    h_ref[...] = h.astype(BF16)


def _prenorm(x, g, mod4, layer):
    tm = 512
    return pl.pallas_call(
        _prenorm_kernel,
        grid=(N_TOK // tm,),
        in_specs=[pl.BlockSpec((tm, D_MODEL), lambda i: (i, 0)),
                  _gain_spec(layer), _mod_spec(layer, 0, tm), _mod_spec(layer, 1, tm)],
        out_specs=pl.BlockSpec((tm, D_MODEL), lambda i: (i, 0)),
        out_shape=jax.ShapeDtypeStruct((N_TOK, D_MODEL), BF16),
        compiler_params=_params(1),
        name="prenorm",
    )(x, g, mod4, mod4)


def _inproj_kernel(h_ref, w_ref, o_ref, wbf_ref):
    @pl.when(pl.program_id(1) == 0)
    def _():
        wbf_ref[...] = w_ref[...].astype(BF16)

    o_ref[...] = jnp.dot(h_ref[...], wbf_ref[...], preferred_element_type=F32)


def _in_proj(h, w_in, layer):
    tm, tn = 1024, 1024
    return pl.pallas_call(
        _inproj_kernel,
        grid=(N_PROJ // tn, N_TOK // tm),
        in_specs=[pl.BlockSpec((tm, D_MODEL), lambda j, i: (i, 0)),
                  pl.BlockSpec((None, D_MODEL, tn), lambda j, i: (layer, 0, j))],
        out_specs=pl.BlockSpec((tm, tn), lambda j, i: (i, j)),
        out_shape=jax.ShapeDtypeStruct((N_TOK, N_PROJ), F32),
        scratch_shapes=[pltpu.VMEM((D_MODEL, tn), BF16)],
        compiler_params=_params(2),
        name="in_proj",
    )(h, w_in)


def _scan_levels(chunk):
    m, levels = chunk // 2, []
    while m >= 1:
        levels.append(m)
        m //= 2
    return tuple(levels)


@functools.lru_cache(maxsize=None)
def _scan_tables(chunk):
    levels = _scan_levels(chunk)
    amats, rowsels, pairs = [], [], []
    for rev in (False, True):
        blocks, sels = [], []
        for m in levels:
            a = np.zeros((chunk, chunk), np.float32)
            sel = np.zeros((chunk,), np.float32)
            for t in range(chunk):
                mid = 2 * m * (t // (2 * m)) + m - 1
                upper = t > mid
                if not rev:
                    if upper:
                        a[t, mid + 1:t + 1] = 1.0
                    else:
                        a[t, t + 1:mid + 1] = 1.0
                    sel[t] = 1.0 if upper else 0.0
                else:
                    if upper:
                        a[t, mid + 1:t] = 1.0
                    else:
                        a[t, t:mid + 1] = 1.0
                    sel[t] = 0.0 if upper else 1.0
            blocks.append(a)
            sels.append(np.repeat(sel[:, None], LANES, axis=1))
        idx = np.arange(chunk)
        if not rev:
            inter = (idx[None, :] <= idx[:, None]).astype(np.float32)
            carry = (idx[None, :] > idx[:, None]).astype(np.float32)
        else:
            inter = (idx[None, :] >= idx[:, None]).astype(np.float32)
            carry = (idx[None, :] < idx[:, None]).astype(np.float32)
        blocks += [inter, carry, np.ones((16, chunk), np.float32)]
        amats.append(np.concatenate(blocks, axis=0))
        rowsels.append(np.stack(sels))
        masks = []
        for m, sel in zip(levels, sels):
            same = idx[:, None] // (2 * m) == idx[None, :] // (2 * m)
            masks.append((same & (sel[:, :1] > 0.5) & (sel[:, :1].T < 0.5)).astype(np.float32))
        masks.append(np.eye(chunk, dtype=np.float32))
        pairs.append(np.stack(masks))
    return np.stack(amats), np.stack(rowsels), np.stack(pairs)


def _dot_exact_rhs(a_bf16, x):
    x1 = x.astype(BF16)
    x2 = (x - x1.astype(F32)).astype(BF16)
    n = x.shape[1]
    y = jnp.dot(a_bf16, jnp.concatenate([x1, x2], axis=1), preferred_element_type=F32)
    return y[:, :n] + y[:, n:]


_NT = (((1,), (1,)), ((), ()))
_TN = (((0,), (0,)), ((), ()))


def _hgrn_kernel(*refs, seq_len, chunk, has_s0):
    if has_s0:
        (q_ref, fa_ref, fb_ref, i_ref, g_ref, lb_ref, nw_ref, amat_ref, rowsel_ref, pair_ref, s0_ref,
         o_ref, sfin_ref, of_scr, ob_scr, st_scr) = refs
    else:
        (q_ref, fa_ref, fb_ref, i_ref, g_ref, lb_ref, nw_ref, amat_ref, rowsel_ref, pair_ref,
         o_ref, sfin_ref, of_scr, ob_scr, st_scr) = refs
    n_chunks = seq_len // chunk
    n_lev = len(_scan_levels(chunk))

    for d in range(2):
        if has_s0:
            st_scr[d] = s0_ref[d].T
        else:
            st_scr[d] = jnp.zeros((DV_A, DK_A), F32)

    f_refs = (fa_ref, fb_ref)
    o_scrs = (of_scr, ob_scr)

    def gates(d, rows):
        z = f_refs[d][rows, :]
        q = _silu(q_ref[rows, :])
        lbd = lb_ref[d:d + 1, :]
        e = jnp.exp(-jnp.abs(z))
        den = 1.0 + e
        a1 = jnp.log(lbd)
        a2 = jnp.log1p(-lbd) + (jnp.minimum(z, 0.0) - jnp.log(den))
        logf = jnp.maximum(a1, a2) + jnp.log(1.0 + jnp.exp(-jnp.abs(a1 - a2)))
        k = (1.0 - lbd) * (jnp.where(z >= 0.0, e, 1.0) / den)
        return q, k, logf

    def body(jj, carry):
        chains = []
        for d in range(2):
            for u in range(SCAN_UNROLL):
                j = jj * SCAN_UNROLL + u
                if d == 1:
                    j = n_chunks - 1 - j
                chains.append((d, pl.ds(pl.multiple_of(j * chunk, chunk), chunk)))
        qkf = [gates(d, rows) for d, rows in chains]
        ws = [jnp.exp(_dot_exact_rhs(amat_ref[d], logf))
              for (d, _), (_, _, logf) in zip(chains, qkf)]
        vbs = [i_ref[rows, :].astype(BF16) for _, rows in chains]
        atts, dsts = [], []
        for (d, _), (q, k, _), w, vb in zip(chains, qkf, ws, vbs):
            att = lax.dot_general(q.astype(BF16), k.astype(BF16), _NT,
                                  preferred_element_type=F32) * pair_ref[d, n_lev]
            for lv in range(n_lev):
                x = (jnp.where(rowsel_ref[d, lv] > 0.5, q, k)
                     * w[lv * chunk:(lv + 1) * chunk]).astype(BF16)
                att = att + lax.dot_general(x, x, _NT, preferred_element_type=F32) * pair_ref[d, lv]
            atts.append(att.astype(BF16))
            wk = w[(n_lev + 1) * chunk:(n_lev + 2) * chunk]
            dsts.append(lax.dot_general(vb, (k * wk).astype(BF16), _TN, preferred_element_type=F32))
        for d in range(2):
            st = st_scr[d]
            for u in range(SCAN_UNROLL):
                c = d * SCAN_UNROLL + u
                q, w = qkf[c][0], ws[c]
                wq = w[n_lev * chunk:(n_lev + 1) * chunk]
                wall = w[(n_lev + 2) * chunk:(n_lev + 2) * chunk + 1]
                o = lax.dot_general((q * wq).astype(BF16), st.astype(BF16), _NT,
                                    preferred_element_type=F32)
                o = o + jnp.dot(atts[c], vbs[c], preferred_element_type=F32)
                o_scrs[d][chains[c][1], :] = o
                st = st * wall + dsts[c]
            st_scr[d] = st
        return carry

    lax.fori_loop(0, n_chunks // SCAN_UNROLL, body, 0)

    o = of_scr[...] + ob_scr[...]
    y = _rms(o) * nw_ref[...]
    o_ref[...] = (y * _silu(g_ref[...])).astype(BF16)
    for d in range(2):
        sfin_ref[d] = st_scr[d].T


def _hgrn_mixer(proj, lb, norm_w, s0, layer, *, row_block0, batch, seq_len):
    chunk = SCAN_CHUNK
    amat, rowsel, pair = _scan_tables(chunk)
    amat = jnp.asarray(amat).astype(BF16)
    rowsel = jnp.asarray(rowsel)
    pair = jnp.asarray(pair)
    has_s0 = s0 is not None

    def col_spec(col):
        return pl.BlockSpec((seq_len, LANES), lambda b, h: (row_block0 + b, col // LANES + h))

    def const_spec(a):
        nd = a.ndim
        return pl.BlockSpec(a.shape, lambda b, h: (0,) * nd)

    in_specs = [col_spec(COL_Q), col_spec(COL_FA), col_spec(COL_FB), col_spec(COL_I), col_spec(COL_G),
                pl.BlockSpec((2, LANES), lambda b, h: (0, h)),
                pl.BlockSpec((1, LANES), lambda b, h: (0, 0)),
                const_spec(amat), const_spec(rowsel), const_spec(pair)]
    args = [proj, proj, proj, proj, proj, lb, norm_w.reshape(1, DV_A), amat, rowsel, pair]
    if has_s0:
        in_specs.append(pl.BlockSpec((None, None, 2, None, DK_A, DV_A),
                                     lambda b, h: (b, layer, 0, h, 0, 0)))
        args.append(s0)
    kern = functools.partial(_hgrn_kernel, seq_len=seq_len, chunk=chunk, has_s0=has_s0)
    return pl.pallas_call(
        kern,
        grid=(batch, H_A),
        in_specs=in_specs,
        out_specs=[pl.BlockSpec((seq_len, LANES), lambda b, h: (b, h)),
                   pl.BlockSpec((None, 2, None, DK_A, DV_A), lambda b, h: (b, 0, h, 0, 0))],
        out_shape=[jax.ShapeDtypeStruct((batch * seq_len, D_A), BF16),
                   jax.ShapeDtypeStruct((batch, 2, H_A, DK_A, DV_A), F32)],
        scratch_shapes=[pltpu.VMEM((seq_len, DV_A), F32), pltpu.VMEM((seq_len, DV_A), F32),
                        pltpu.VMEM((2, DV_A, DK_A), F32)],
        compiler_params=_params(2),
        name="hgrn_mixer",
    )(*args)


@functools.lru_cache(maxsize=None)
def _hyena_tables(seq_len):
    n = seq_len
    t = np.arange(n, dtype=np.float64)
    t01 = t / max(n - 1, 1)
    bands = np.linspace(1e-4, HY_BANDS - 1, HY_BANDS)
    ang = (2.0 * math.pi / n) * t[:, None] * bands[None, :]
    feats = np.concatenate([t01[:, None], np.cos(ang), -np.sin(ang)], axis=-1)
    feats = np.pad(feats, ((0, 0), (0, LANES - HY_EMB))).astype(np.float32)
    ft = np.outer(np.arange(n), np.arange(n)) % (2 * n)
    fc = np.cos(math.pi * ft / n)
    fs = -np.sin(math.pi * ft / n)
    fs[0, :] = 1.0 - 2.0 * (np.arange(n) % 2)
    return feats, fc.astype(np.float32), fs.astype(np.float32), fs.T.astype(np.float32).copy()


def _hyena_filter_kernel(feats_ref, w1_ref, b1_ref, w2_ref, b2_ref, w3_ref0, w3_ref1, w3_ref2, w3_ref3,
                         decay_ref, a_ref, bm_ref, *, seq_len):
    hp = lax.Precision.HIGHEST
    h = jnp.sin(HY_SIN_W * (jnp.dot(feats_ref[...], w1_ref[...], precision=hp,
                                    preferred_element_type=F32) + b1_ref[...]))
    h = jnp.sin(HY_SIN_W * (jnp.dot(h, w2_ref[...], precision=hp,
                                    preferred_element_type=F32) + b2_ref[...]))
    w3 = ((w3_ref0, w3_ref1), (w3_ref2, w3_ref3))
    cb = a_ref.shape[-1]
    row = lax.broadcasted_iota(jnp.int32, (seq_len, cb), 0)
    t01 = row.astype(F32) / float(max(seq_len - 1, 1))
    for o in range(HY_ORDER):
        win = jnp.exp(-t01 * jnp.abs(decay_ref[o:o + 1, :]))
        hf = jnp.dot(h, w3[o][0][...], precision=hp, preferred_element_type=F32) * win
        hb = jnp.dot(h, w3[o][1][...], precision=hp, preferred_element_type=F32) * win
        mag = jnp.where(row == 0, jnp.abs(hf + hb), jnp.abs(hf) + jnp.abs(hb))
        inv = 1.0 / (jnp.sum(mag, axis=0, keepdims=True) + EPS)
        a_ref[o] = (hf + hb) * inv
        bm_ref[o] = (hf - hb) * inv


def _hyena_filters(seq_len, w1, b1, w2, b2, w3, decay):
    feats = jnp.asarray(_hyena_tables(seq_len)[0])
    w1p = jnp.pad(w1, ((0, LANES - HY_EMB), (0, 0)))
    cb = HY_CB
    nb = D_B // cb

    def w3_spec(k):
        return pl.BlockSpec((HY_HID, cb), lambda c: (0, k * nb + c))

    full2 = lambda a: pl.BlockSpec(a.shape, lambda c: (0, 0))
    b1r, b2r = b1.reshape(1, HY_HID), b2.reshape(1, HY_HID)
    out_sds = jax.ShapeDtypeStruct((HY_ORDER, seq_len, D_B), F32)
    return pl.pallas_call(
        functools.partial(_hyena_filter_kernel, seq_len=seq_len),
        grid=(nb,),
        in_specs=[full2(feats), full2(w1p), full2(b1r), full2(w2), full2(b2r),
                  w3_spec(0), w3_spec(1), w3_spec(2), w3_spec(3),
                  pl.BlockSpec((HY_ORDER, cb), lambda c: (0, c))],
        out_specs=[pl.BlockSpec((HY_ORDER, seq_len, cb), lambda c: (0, 0, c))] * 2,
        out_shape=[out_sds, out_sds],
        compiler_params=_params(1),
        name="hyena_filter",
    )(feats, w1p, b1r, w2, b2r, w3, w3, w3, w3, decay)


def _dft(f_hi_ref, f_lo_ref, x, passes):
    xh = x.astype(BF16)
    acc = jnp.dot(f_hi_ref[...], xh, preferred_element_type=F32)
    if passes >= 2:
        xl = (x - xh.astype(F32)).astype(BF16)
        acc = acc + jnp.dot(f_hi_ref[...], xl, preferred_element_type=F32)
    if passes >= 3:
        acc = acc + jnp.dot(f_lo_ref[...], xh, preferred_element_type=F32)
    return acc


def _hyena_conv_kernel(v_ref, x1_ref, x2_ref, cwv_ref, cw1_ref, cw2_ref, cbv_ref, cb1_ref, cb2_ref,
                       a_ref, bm_ref, bias_ref, fc_hi, fc_lo, fs_hi, fs_lo, fst_hi, fst_lo,
                       o_ref, p_scr, *, seq_len, data_passes):
    n = seq_len
    cb = o_ref.shape[-1]
    row = lax.broadcasted_iota(jnp.int32, (n, cb), 0)
    scale = 1.0 / (2.0 * n)

    @pl.when(pl.program_id(1) == 0)
    def _():
        wts = jnp.where(row == 0, scale, 2.0 * scale)
        sgn = (1 - 2 * (row & 1)).astype(F32)
        for o in range(HY_ORDER):
            a = a_ref[o]
            kre = _dft(fc_hi, fc_lo, a, 3)
            kim = _dft(fs_hi, fs_lo, bm_ref[o], 3)
            knyq = jnp.sum(a * sgn, axis=0, keepdims=True)
            p_scr[3 * o + 0] = kre * wts
            p_scr[3 * o + 1] = jnp.where(row == 0, 0.0, kim * wts)
            p_scr[3 * o + 2] = jnp.where(row == 0, knyq * scale, kre * wts)

    def dwconv(x_ref, w_ref, b_ref):
        x = x_ref[...]
        xm = jnp.where(row == 0, 0.0, pltpu.roll(x, 1, axis=0))
        xp = jnp.where(row == n - 1, 0.0, pltpu.roll(x, n - 1, axis=0))
        return xm * w_ref[0:1, :] + x * w_ref[1:2, :] + xp * w_ref[2:3, :] + b_ref[...]

    def fftconv(z, o):
        zt = _dft(fc_hi, fc_lo, z, data_passes)
        zb = _dft(fs_hi, fs_lo, z, data_passes)
        p1, p3, p4 = p_scr[3 * o], p_scr[3 * o + 1], p_scr[3 * o + 2]
        yt = zt * p1 - zb * p3
        yb = zt * p3 + zb * p4
        y = _dft(fc_hi, fc_lo, yt, data_passes) + _dft(fst_hi, fst_lo, yb, data_passes)
        return y + z * bias_ref[o:o + 1, :]

    v = dwconv(v_ref, cwv_ref, cbv_ref)
    x1 = dwconv(x1_ref, cw1_ref, cb1_ref)
    z2 = x1 * fftconv(v, 0)
    x2 = dwconv(x2_ref, cw2_ref, cb2_ref)
    o_ref[...] = (x2 * fftconv(z2, 1)).astype(BF16)


def _hyena_mixer(proj, conv_w, conv_b, a_filt, bm_filt, bias, *, row_block0, batch, seq_len,
                 data_passes):
    _, fc, fs, fst = _hyena_tables(seq_len)
    mats = []
    for m in (fc, fs, fst):
        m = jnp.asarray(m)
        hi = m.astype(BF16)
        mats += [hi, (m - hi.astype(F32)).astype(BF16)]
    cb = HY_CB
    nb = D_B // cb

    def col_spec(k):
        return pl.BlockSpec((seq_len, cb), lambda c, b: (row_block0 + b, (COL_HY + k * D_B) // cb + c))

    def cw_spec(k):
        return pl.BlockSpec((3, cb), lambda c, b: (0, k * nb + c))

    def cb_spec(k):
        return pl.BlockSpec((1, cb), lambda c, b: (0, k * nb + c))

    filt_spec = pl.BlockSpec((HY_ORDER, seq_len, cb), lambda c, b: (0, 0, c))
    mat_spec = pl.BlockSpec((seq_len, seq_len), lambda c, b: (0, 0))
    conv_b2 = conv_b.reshape(1, 3 * D_B)
    return pl.pallas_call(
        functools.partial(_hyena_conv_kernel, seq_len=seq_len, data_passes=data_passes),
        grid=(nb, batch),
        in_specs=[col_spec(0), col_spec(1), col_spec(2), cw_spec(0), cw_spec(1), cw_spec(2),
                  cb_spec(0), cb_spec(1), cb_spec(2), filt_spec, filt_spec,
                  pl.BlockSpec((HY_ORDER, cb), lambda c, b: (0, c))] + [mat_spec] * 6,
        out_specs=pl.BlockSpec((seq_len, cb), lambda c, b: (b, c)),
        out_shape=jax.ShapeDtypeStruct((batch * seq_len, D_B), BF16),
        scratch_shapes=[pltpu.VMEM((3 * HY_ORDER, seq_len, cb), F32)],
        compiler_params=_params(2),
        name="hyena_conv",
    )(proj, proj, proj, conv_w, conv_w, conv_w, conv_b2, conv_b2, conv_b2, a_filt, bm_filt, bias, *mats)


def _merge_kernel(oa_ref, ob_ref, ga_ref, gb_ref, wa_ref, wb_ref, o_ref, wabf, wbbf):
    @pl.when(pl.program_id(1) == 0)
    def _():
        wabf[...] = wa_ref[...].astype(BF16)
        wbbf[...] = wb_ref[...].astype(BF16)

    ya = jnp.dot(oa_ref[...], wabf[...], preferred_element_type=F32)
    yb = jnp.dot(ob_ref[...], wbbf[...], preferred_element_type=F32)
    o_ref[...] = (jax.nn.sigmoid(ga_ref[...]) * ya + jax.nn.sigmoid(gb_ref[...]) * yb).astype(BF16)


def _merge(oa, ob, proj, w_a, w_b, layer):
    tm, tn = 1024, 512
    nbn = D_MODEL // tn
    return pl.pallas_call(
        _merge_kernel,
        grid=(nbn, N_TOK // tm),
        in_specs=[pl.BlockSpec((tm, D_A), lambda j, i: (i, 0)),
                  pl.BlockSpec((tm, D_B), lambda j, i: (i, 0)),
                  pl.BlockSpec((tm, tn), lambda j, i: (i, COL_GATE_A // tn + j)),
                  pl.BlockSpec((tm, tn), lambda j, i: (i, COL_GATE_B // tn + j)),
                  pl.BlockSpec((None, D_A, tn), lambda j, i: (layer, 0, j)),
                  pl.BlockSpec((None, D_B, tn), lambda j, i: (layer, 0, j))],
        out_specs=pl.BlockSpec((tm, tn), lambda j, i: (i, j)),
        out_shape=jax.ShapeDtypeStruct((N_TOK, D_MODEL), BF16),
        scratch_shapes=[pltpu.VMEM((D_A, tn), BF16), pltpu.VMEM((D_B, tn), BF16)],
        compiler_params=_params(2),
        name="merge",
    )(oa, ob, proj, proj, w_a, w_b)


def _resid_kernel(*refs, has_next):
    if has_next:
        (a_ref, w_ref, x_ref, gpost_ref, gate_ref, gnext_ref, sh_ref, sc_ref,
         xo_ref, h_ref) = refs
    else:
        a_ref, w_ref, x_ref, gpost_ref, gate_ref, xo_ref = refs
    k = pl.program_id(1)
    y = jnp.dot(a_ref[...], w_ref[...].astype(BF16), preferred_element_type=F32)

    @pl.when(k == 0)
    def _():
        xo_ref[...] = y

    @pl.when(k > 0)
    def _():
        xo_ref[...] += y

    @pl.when(k == pl.num_programs(1) - 1)
    def _():
        def rows_step(r, carry):
            rows = pl.ds(pl.multiple_of(r * EPILOGUE_ROWS, EPILOGUE_ROWS), EPILOGUE_ROWS)
            xn = x_ref[rows, :] + gate_ref[...] * (_rms(xo_ref[rows, :]) * gpost_ref[...])
            xo_ref[rows, :] = xn
            if has_next:
                h = (_rms(xn) * gnext_ref[...]) * (1.0 + sc_ref[...]) + sh_ref[...]
                h_ref[rows, :] = h.astype(BF16)
            return carry

        lax.fori_loop(0, xo_ref.shape[0] // EPILOGUE_ROWS, rows_step, 0)


def _proj_residual(a, w, x, g_post, mod4, layer, gate_chunk, nxt):
    tm, tk = 1024, 512
    kdim = a.shape[1]
    has_next = nxt is not None
    row_spec = pl.BlockSpec((tm, D_MODEL), lambda i, k: (i, 0))
    x_spec = pl.BlockSpec((tm, D_MODEL), lambda i, k: (i, 0), pipeline_mode=pl.Buffered(1))
    in_specs = [pl.BlockSpec((tm, tk), lambda i, k: (i, k)),
                pl.BlockSpec((None, tk, D_MODEL), lambda i, k: (layer, k, 0)),
                x_spec,
                _gain_spec(layer), _mod_spec(layer, gate_chunk, tm)]
    args = [a, w, x, g_post, mod4]
    out_specs = [row_spec]
    out_shape = [jax.ShapeDtypeStruct((N_TOK, D_MODEL), F32)]
    if has_next:
        g_next, nl, sh_chunk, sc_chunk = nxt
        in_specs += [_gain_spec(nl), _mod_spec(nl, sh_chunk, tm), _mod_spec(nl, sc_chunk, tm)]
        args += [g_next, mod4, mod4]
        out_specs.append(row_spec)
        out_shape.append(jax.ShapeDtypeStruct((N_TOK, D_MODEL), BF16))
    res = pl.pallas_call(
        functools.partial(_resid_kernel, has_next=has_next),
        grid=(N_TOK // tm, kdim // tk),
        in_specs=in_specs,
        out_specs=out_specs,
        out_shape=out_shape,
        compiler_params=_params(2),
        name="proj_residual",
    )(*args)
    return res if has_next else (res[0], None)


def _ffn_up_kernel(h_ref, wa_ref, wv_ref, cwa_ref, cwv_ref, cba_ref, cbv_ref, o_ref, wabf, wvbf):
    i = pl.program_id(1)

    @pl.when(i == 0)
    def _():
        wabf[...] = wa_ref[...].astype(BF16)
        wvbf[...] = wv_ref[...].astype(BF16)

    h = h_ref[...]
    tm, tn = o_ref.shape
    grp = FFN_COL_GROUP
    t = lax.broadcasted_iota(jnp.int32, (tm, LANES), 0)
    span = jnp.where(i < N_CTX_BLOCKS, SEQ - 1, GRID_W - 1)
    pos = t & span
    keep_prev = jnp.concatenate([(pos != 0).astype(F32)] * (grp // LANES), axis=1)
    keep_next = jnp.concatenate([(pos != span).astype(F32)] * (grp // LANES), axis=1)
    zrows = jnp.zeros((GRID_W, grp), F32)

    def conv(u, cw_ref, cb_ref, cols):
        um = pltpu.roll(u, 1, axis=0) * keep_prev
        up = pltpu.roll(u, tm - 1, axis=0) * keep_next
        acc = cb_ref[:, cols]
        for dr in (-1, 0, 1):
            r = 3 * (dr + 1)
            s = (um * cw_ref[r:r + 1, cols] + u * cw_ref[r + 1:r + 2, cols]
                 + up * cw_ref[r + 2:r + 3, cols])
            if dr == -1:
                s = jnp.concatenate([zrows, s[:tm - GRID_W]], axis=0)
            elif dr == 1:
                s = jnp.concatenate([s[GRID_W:], zrows], axis=0)
            acc = acc + s
        return acc

    for c in range(tn // grp):
        cols = slice(c * grp, (c + 1) * grp)
        a = conv(jnp.dot(h, wabf[:, cols], preferred_element_type=F32), cwa_ref, cba_ref, cols)
        vv = conv(jnp.dot(h, wvbf[:, cols], preferred_element_type=F32), cwv_ref, cbv_ref, cols)
        o_ref[:, cols] = (_silu(a) * vv).astype(BF16)


def _ffn_up(h, w_up, conv_w, conv_b, layer):
    tm, tn = TOK_BLOCK, 512
    nbn = D_FF // tn
    conv_w9 = conv_w.reshape(DEPTH, 9, 2 * D_FF)
    mid = (jnp.arange(9) // 3 == 1).astype(F32)[None, :, None]
    conv_w9 = jnp.stack([conv_w9 * mid, conv_w9], axis=1)
    conv_b2 = conv_b.reshape(DEPTH, 1, 2 * D_FF)
    blk_type = lambda i: i // N_CTX_BLOCKS
    return pl.pallas_call(
        _ffn_up_kernel,
        grid=(nbn, N_TOK // tm),
        in_specs=[pl.BlockSpec((tm, D_MODEL), lambda j, i: (i, 0)),
                  pl.BlockSpec((None, D_MODEL, tn), lambda j, i: (layer, 0, j)),
                  pl.BlockSpec((None, D_MODEL, tn), lambda j, i: (layer, 0, nbn + j)),
                  pl.BlockSpec((None, None, 9, tn), lambda j, i: (layer, blk_type(i), 0, j)),
                  pl.BlockSpec((None, None, 9, tn), lambda j, i: (layer, blk_type(i), 0, nbn + j)),
                  pl.BlockSpec((None, 1, tn), lambda j, i: (layer, 0, j)),
                  pl.BlockSpec((None, 1, tn), lambda j, i: (layer, 0, nbn + j))],
        out_specs=pl.BlockSpec((tm, tn), lambda j, i: (i, j)),
        out_shape=jax.ShapeDtypeStruct((N_TOK, D_FF), BF16),
        scratch_shapes=[pltpu.VMEM((D_MODEL, tn), BF16), pltpu.VMEM((D_MODEL, tn), BF16)],
        compiler_params=_params(2),
        name="ffn_up",
    )(h, w_up, w_up, conv_w9, conv_w9, conv_b2, conv_b2)


def kernel(x_prompt, x_sample, state_hgrn, c, c_ctx, w_mod, b_mod, g_pre_mix, g_post_mix, g_pre_ffn,
           g_post_ffn, w_in, hgrn_lower_bounds, hgrn_norm, hy_conv_w, hy_conv_b, hy_w1, hy_b1, hy_w2,
           hy_b2, hy_w3, hy_decay, hy_bias, w_branch_a, w_branch_b, w_out, ffn_w_up, ffn_conv_w,
           ffn_conv_b, ffn_w_down):
    p_lb = jax.nn.softmax(hgrn_lower_bounds.astype(F32), axis=0)
    cs = jnp.cumsum(p_lb, axis=0)
    lbs = cs - cs[:1]

    x = jnp.concatenate([x_prompt.reshape(N_CTX, D_MODEL), x_sample.reshape(N_LAT, D_MODEL)], axis=0)
    cond = jnp.concatenate([jnp.broadcast_to(c_ctx[None, :], (N_CTX_BLOCKS, D_MODEL)), c], axis=0)
    mod4 = _modulation(cond, w_mod, b_mod).reshape(DEPTH, N_TOK_BLOCKS, 1, N_MOD)

    g_pre_mix, g_post_mix, g_pre_ffn, g_post_ffn = (
        g.reshape(DEPTH, 1, D_MODEL) for g in (g_pre_mix, g_post_mix, g_pre_ffn, g_post_ffn))

    h = _prenorm(x, g_pre_mix, mod4, 0)
    new_states = []
    for l in range(DEPTH):
        proj = _in_proj(h, w_in, l)
        oa_ctx, s_ctx = _hgrn_mixer(proj, lbs[l], hgrn_norm[l], None, l,
                                    row_block0=0, batch=BATCH, seq_len=SEQ)
        oa_lat, _ = _hgrn_mixer(proj, lbs[l], hgrn_norm[l], state_hgrn, l,
                                row_block0=N_CTX // DEC_SEQ, batch=DEC_BATCH, seq_len=DEC_SEQ)
        new_states.append(s_ctx)
        ob = []
        for row_block0, batch, seq_len in ((0, BATCH, SEQ), (N_CTX // DEC_SEQ, DEC_BATCH, DEC_SEQ)):
            a_filt, bm_filt = _hyena_filters(seq_len, hy_w1[l], hy_b1[l], hy_w2[l], hy_b2[l], hy_w3[l],
                                             hy_decay[l])
            ob.append(_hyena_mixer(proj, hy_conv_w[l], hy_conv_b[l], a_filt, bm_filt, hy_bias[l],
                                   row_block0=row_block0, batch=batch, seq_len=seq_len, data_passes=1))
        oa = jnp.concatenate([oa_ctx, oa_lat], axis=0)
        ob = jnp.concatenate(ob, axis=0)
        merged = _merge(oa, ob, proj, w_branch_a, w_branch_b, l)
        x, h2 = _proj_residual(merged, w_out, x, g_post_mix, mod4, l, 2, (g_pre_ffn, l, 3, 4))
        g = _ffn_up(h2, ffn_w_up, ffn_conv_w, ffn_conv_b, l)
        nxt = (g_pre_mix, l + 1, 0, 1) if l + 1 < DEPTH else None
        x, h = _proj_residual(g, ffn_w_down, x, g_post_ffn, mod4, l, 5, nxt)

    y_prompt = x[:N_CTX].reshape(BATCH, SEQ, D_MODEL)
    y_sample = x[N_CTX:].reshape(DEC_BATCH, DEC_SEQ, D_MODEL)
    new_state = jnp.stack(new_states, axis=1).astype(x_prompt.dtype)
    return (y_prompt, y_sample, new_state)
```

```python
import functools
import math

import numpy as np
import jax
import jax.numpy as jnp
from jax import lax
from jax.experimental import pallas as pl
from jax.experimental.pallas import tpu as pltpu

F32 = jnp.float32
BF16 = jnp.bfloat16

D_MODEL = 2048
BATCH = 16
SEQ = 256
DEPTH = 2
DEC_BATCH = 4
DEC_SEQ = 1024
GRID_W = 64
D_A = D_MODEL // 2
D_B = D_MODEL // 2
HGRN_EXPAND = 128
H_A = D_A // HGRN_EXPAND
DK_A = HGRN_EXPAND
DV_A = D_A // H_A
HY_ORDER = 2
HY_EMB = 33
HY_BANDS = (HY_EMB - 1) // 2
HY_HID = 64
HY_SIN_W = 1.0
D_FF = 5632
N_MOD = 6 * D_MODEL
N_PROJ = 5 * D_A + 3 * D_B + 2 * D_MODEL
EPS = 1e-6

N_CTX = BATCH * SEQ
N_LAT = DEC_BATCH * DEC_SEQ
N_TOK = N_CTX + N_LAT
TOK_BLOCK = 1024
N_TOK_BLOCKS = N_TOK // TOK_BLOCK
N_CTX_BLOCKS = N_CTX // TOK_BLOCK

COL_Q, COL_FA, COL_FB, COL_I, COL_G = 0, D_A, 2 * D_A, 3 * D_A, 4 * D_A
COL_HY = 5 * D_A
COL_GATE_A = 5 * D_A + 3 * D_B
COL_GATE_B = COL_GATE_A + D_MODEL

LANES = 128
SCAN_CHUNK = 32
SCAN_UNROLL = 8
HY_CB = 256
HY_ROWS = 2048
FFN_COL_GROUP = 256
EPILOGUE_ROWS = 128
VMEM_LIMIT = 56 * 1024 * 1024


def _params(n_axes):
    return pltpu.CompilerParams(dimension_semantics=("arbitrary",) * n_axes,
                                vmem_limit_bytes=VMEM_LIMIT)


def _silu(x):
    return x * jax.nn.sigmoid(x)


def _rms(x):
    return x * lax.rsqrt(jnp.mean(x * x, axis=-1, keepdims=True) + EPS)


def _mod_spec(layer, chunk, tm):
    return pl.BlockSpec((None, None, 1, D_MODEL),
                        lambda i, *_: (layer, (i * tm) // TOK_BLOCK, 0, chunk))


def _gain_spec(layer):
    return pl.BlockSpec((None, 1, D_MODEL), lambda *_: (layer, 0, 0))


def _mod_kernel(cond_ref, w_ref, b_ref, o_ref):
    s = _silu(cond_ref[...]).astype(BF16)
    o_ref[...] = jnp.dot(s, w_ref[...].astype(BF16), preferred_element_type=F32) + b_ref[...]


def _modulation(cond, w_mod, b_mod):
    tn = 1024
    return pl.pallas_call(
        _mod_kernel,
        grid=(DEPTH, N_MOD // tn),
        in_specs=[pl.BlockSpec((N_TOK_BLOCKS, D_MODEL), lambda l, j: (0, 0)),
                  pl.BlockSpec((None, D_MODEL, tn), lambda l, j: (l, 0, j)),
                  pl.BlockSpec((None, 1, tn), lambda l, j: (l, 0, j))],
        out_specs=pl.BlockSpec((None, N_TOK_BLOCKS, tn), lambda l, j: (l, 0, j)),
        out_shape=jax.ShapeDtypeStruct((DEPTH, N_TOK_BLOCKS, N_MOD), F32),
        compiler_params=_params(2),
        name="modulation",
    )(cond, w_mod, b_mod.reshape(DEPTH, 1, N_MOD))


def _prenorm_kernel(x_ref, g_ref, sh_ref, sc_ref, h_ref):
    h = (_rms(x_ref[...]) * g_ref[...]) * (1.0 + sc_ref[...]) + sh_ref[...]
    h_ref[...] = h.astype(BF16)


def _prenorm(x, g, mod4, layer):
    tm = 512
    return pl.pallas_call(
        _prenorm_kernel,
        grid=(N_TOK // tm,),
        in_specs=[pl.BlockSpec((tm, D_MODEL), lambda i: (i, 0)),
                  _gain_spec(layer), _mod_spec(layer, 0, tm), _mod_spec(layer, 1, tm)],
        out_specs=pl.BlockSpec((tm, D_MODEL), lambda i: (i, 0)),
        out_shape=jax.ShapeDtypeStruct((N_TOK, D_MODEL), BF16),
        compiler_params=_params(1),
        name="prenorm",
    )(x, g, mod4, mod4)


def _inproj_kernel(h_ref, w_ref, o_ref, wbf_ref):
    @pl.when(pl.program_id(1) == 0)
    def _():
        wbf_ref[...] = w_ref[...].astype(BF16)

    o_ref[...] = jnp.dot(h_ref[...], wbf_ref[...], preferred_element_type=F32)


def _in_proj(h, w_in, layer):
    tm, tn = 1024, 1024
    return pl.pallas_call(
        _inproj_kernel,
        grid=(N_PROJ // tn, N_TOK // tm),
        in_specs=[pl.BlockSpec((tm, D_MODEL), lambda j, i: (i, 0)),
                  pl.BlockSpec((None, D_MODEL, tn), lambda j, i: (layer, 0, j))],
        out_specs=pl.BlockSpec((tm, tn), lambda j, i: (i, j)),
        out_shape=jax.ShapeDtypeStruct((N_TOK, N_PROJ), F32),
        scratch_shapes=[pltpu.VMEM((D_MODEL, tn), BF16)],
        compiler_params=_params(2),
        name="in_proj",
    )(h, w_in)


def _scan_levels(chunk):
    m, levels = chunk // 2, []
    while m >= 1:
        levels.append(m)
        m //= 2
    return tuple(levels)


@functools.lru_cache(maxsize=None)
def _scan_tables(chunk):
    levels = _scan_levels(chunk)
    amats, rowsels, pairs = [], [], []
    for rev in (False, True):
        blocks, sels = [], []
        for m in levels:
            a = np.zeros((chunk, chunk), np.float32)
            sel = np.zeros((chunk,), np.float32)
            for t in range(chunk):
                mid = 2 * m * (t // (2 * m)) + m - 1
                upper = t > mid
                if not rev:
                    if upper:
                        a[t, mid + 1:t + 1] = 1.0
                    else:
                        a[t, t + 1:mid + 1] = 1.0
                    sel[t] = 1.0 if upper else 0.0
                else:
                    if upper:
                        a[t, mid + 1:t] = 1.0
                    else:
                        a[t, t:mid + 1] = 1.0
                    sel[t] = 0.0 if upper else 1.0
            blocks.append(a)
            sels.append(np.repeat(sel[:, None], LANES, axis=1))
        idx = np.arange(chunk)
        if not rev:
            inter = (idx[None, :] <= idx[:, None]).astype(np.float32)
            carry = (idx[None, :] > idx[:, None]).astype(np.float32)
        else:
            inter = (idx[None, :] >= idx[:, None]).astype(np.float32)
            carry = (idx[None, :] < idx[:, None]).astype(np.float32)
        blocks += [inter, carry, np.ones((16, chunk), np.float32)]
        amats.append(np.concatenate(blocks, axis=0))
        rowsels.append(np.stack(sels))
        masks = []
        for m, sel in zip(levels, sels):
            same = idx[:, None] // (2 * m) == idx[None, :] // (2 * m)
            masks.append((same & (sel[:, :1] > 0.5) & (sel[:, :1].T < 0.5)).astype(np.float32))
        masks.append(np.eye(chunk, dtype=np.float32))
        pairs.append(np.stack(masks))
    return np.stack(amats), np.stack(rowsels), np.stack(pairs)


def _dot_exact_rhs(a_bf16, x):
    x1 = x.astype(BF16)
    x2 = (x - x1.astype(F32)).astype(BF16)
    n = x.shape[1]
    y = jnp.dot(a_bf16, jnp.concatenate([x1, x2], axis=1), preferred_element_type=F32)
    return y[:, :n] + y[:, n:]


_NT = (((1,), (1,)), ((), ()))
_TN = (((0,), (0,)), ((), ()))


def _hgrn_kernel(*refs, seq_len, chunk, has_s0):
    if has_s0:
        (q_ref, fa_ref, fb_ref, i_ref, g_ref, lb_ref, nw_ref, amat_ref, rowsel_ref, pair_ref, s0_ref,
         o_ref, sfin_ref, of_scr, ob_scr, st_scr) = refs
    else:
        (q_ref, fa_ref, fb_ref, i_ref, g_ref, lb_ref, nw_ref, amat_ref, rowsel_ref, pair_ref,
         o_ref, sfin_ref, of_scr, ob_scr, st_scr) = refs
    n_chunks = seq_len // chunk
    n_lev = len(_scan_levels(chunk))

    for d in range(2):
        if has_s0:
            st_scr[d] = s0_ref[d].T
        else:
            st_scr[d] = jnp.zeros((DV_A, DK_A), F32)

    f_refs = (fa_ref, fb_ref)
    o_scrs = (of_scr, ob_scr)

    def gates(d, rows):
        z = f_refs[d][rows, :]
        q = _silu(q_ref[rows, :])
        lbd = lb_ref[d:d + 1, :]
        e = jnp.exp(-jnp.abs(z))
        den = 1.0 + e
        a1 = jnp.log(lbd)
        a2 = jnp.log1p(-lbd) + (jnp.minimum(z, 0.0) - jnp.log(den))
        logf = jnp.maximum(a1, a2) + jnp.log(1.0 + jnp.exp(-jnp.abs(a1 - a2)))
        k = (1.0 - lbd) * (jnp.where(z >= 0.0, e, 1.0) / den)
        return q, k, logf

    def body(jj, carry):
        chains = []
        for d in range(2):
            for u in range(SCAN_UNROLL):
                j = jj * SCAN_UNROLL + u
                if d == 1:
                    j = n_chunks - 1 - j
                chains.append((d, pl.ds(pl.multiple_of(j * chunk, chunk), chunk)))
        qkf = [gates(d, rows) for d, rows in chains]
        ws = [jnp.exp(_dot_exact_rhs(amat_ref[d], logf))
              for (d, _), (_, _, logf) in zip(chains, qkf)]
        vbs = [i_ref[rows, :].astype(BF16) for _, rows in chains]
        atts, dsts = [], []
        for (d, _), (q, k, _), w, vb in zip(chains, qkf, ws, vbs):
            att = lax.dot_general(q.astype(BF16), k.astype(BF16), _NT,
                                  preferred_element_type=F32) * pair_ref[d, n_lev]
            for lv in range(n_lev):
                x = (jnp.where(rowsel_ref[d, lv] > 0.5, q, k)
                     * w[lv * chunk:(lv + 1) * chunk]).astype(BF16)
                att = att + lax.dot_general(x, x, _NT, preferred_element_type=F32) * pair_ref[d, lv]
            atts.append(att.astype(BF16))
            wk = w[(n_lev + 1) * chunk:(n_lev + 2) * chunk]
            dsts.append(lax.dot_general(vb, (k * wk).astype(BF16), _TN, preferred_element_type=F32))
        for d in range(2):
            st = st_scr[d]
            for u in range(SCAN_UNROLL):
                c = d * SCAN_UNROLL + u
                q, w = qkf[c][0], ws[c]
                wq = w[n_lev * chunk:(n_lev + 1) * chunk]
                wall = w[(n_lev + 2) * chunk:(n_lev + 2) * chunk + 1]
                o = lax.dot_general((q * wq).astype(BF16), st.astype(BF16), _NT,
                                    preferred_element_type=F32)
                o = o + jnp.dot(atts[c], vbs[c], preferred_element_type=F32)
                o_scrs[d][chains[c][1], :] = o
                st = st * wall + dsts[c]
            st_scr[d] = st
        return carry

    lax.fori_loop(0, n_chunks // SCAN_UNROLL, body, 0)

    o = of_scr[...] + ob_scr[...]
    y = _rms(o) * nw_ref[...]
    o_ref[...] = (y * _silu(g_ref[...])).astype(BF16)
    for d in range(2):
        sfin_ref[d] = st_scr[d].T


def _hgrn_mixer(proj, lb, norm_w, s0, layer, *, row_block0, batch, seq_len):
    chunk = SCAN_CHUNK
    amat, rowsel, pair = _scan_tables(chunk)
    amat = jnp.asarray(amat).astype(BF16)
    rowsel = jnp.asarray(rowsel)
    pair = jnp.asarray(pair)
    has_s0 = s0 is not None

    def col_spec(col):
        return pl.BlockSpec((seq_len, LANES), lambda b, h: (row_block0 + b, col // LANES + h))

    def const_spec(a):
        nd = a.ndim
        return pl.BlockSpec(a.shape, lambda b, h: (0,) * nd)

    in_specs = [col_spec(COL_Q), col_spec(COL_FA), col_spec(COL_FB), col_spec(COL_I), col_spec(COL_G),
                pl.BlockSpec((2, LANES), lambda b, h: (0, h)),
                pl.BlockSpec((1, LANES), lambda b, h: (0, 0)),
                const_spec(amat), const_spec(rowsel), const_spec(pair)]
    args = [proj, proj, proj, proj, proj, lb, norm_w.reshape(1, DV_A), amat, rowsel, pair]
    if has_s0:
        in_specs.append(pl.BlockSpec((None, None, 2, None, DK_A, DV_A),
                                     lambda b, h: (b, layer, 0, h, 0, 0)))
        args.append(s0)
    kern = functools.partial(_hgrn_kernel, seq_len=seq_len, chunk=chunk, has_s0=has_s0)
    return pl.pallas_call(
        kern,
        grid=(batch, H_A),
        in_specs=in_specs,
        out_specs=[pl.BlockSpec((seq_len, LANES), lambda b, h: (b, h)),
                   pl.BlockSpec((None, 2, None, DK_A, DV_A), lambda b, h: (b, 0, h, 0, 0))],
        out_shape=[jax.ShapeDtypeStruct((batch * seq_len, D_A), BF16),
                   jax.ShapeDtypeStruct((batch, 2, H_A, DK_A, DV_A), F32)],
        scratch_shapes=[pltpu.VMEM((seq_len, DV_A), F32), pltpu.VMEM((seq_len, DV_A), F32),
                        pltpu.VMEM((2, DV_A, DK_A), F32)],
        compiler_params=_params(2),
        name="hgrn_mixer",
    )(*args)


@functools.lru_cache(maxsize=None)
def _hyena_tables(seq_len):
    n = seq_len
    t = np.arange(n, dtype=np.float64)
    t01 = t / max(n - 1, 1)
    bands = np.linspace(1e-4, HY_BANDS - 1, HY_BANDS)
    ang = (2.0 * math.pi / n) * t[:, None] * bands[None, :]
    feats = np.concatenate([t01[:, None], np.cos(ang), -np.sin(ang)], axis=-1)
    feats = np.pad(feats, ((0, 0), (0, LANES - HY_EMB))).astype(np.float32)
    ft = np.outer(np.arange(n), np.arange(n)) % (2 * n)
    fc = np.cos(math.pi * ft / n)
    fs = -np.sin(math.pi * ft / n)
    fs[0, :] = 1.0 - 2.0 * (np.arange(n) % 2)
    return feats, fc.astype(np.float32), fs.astype(np.float32), fs.T.astype(np.float32).copy()


def _dft3(f_hi_ref, f_lo_ref, x):
    xh = x.astype(BF16)
    xl = (x - xh.astype(F32)).astype(BF16)
    return (jnp.dot(f_hi_ref[...], xh, preferred_element_type=F32)
            + jnp.dot(f_hi_ref[...], xl, preferred_element_type=F32)
            + jnp.dot(f_lo_ref[...], xh, preferred_element_type=F32))


def _hyena_filter_kernel(feats_ref, w1_ref, b1_ref, w2_ref, b2_ref, w3_ref0, w3_ref1, w3_ref2, w3_ref3,
                         decay_ref, fc_hi, fc_lo, fs_hi, fs_lo, p_ref, *, seq_len):
    n = seq_len
    hp = lax.Precision.HIGHEST
    h = jnp.sin(HY_SIN_W * (jnp.dot(feats_ref[...], w1_ref[...], precision=hp,
                                    preferred_element_type=F32) + b1_ref[...]))
    h = jnp.sin(HY_SIN_W * (jnp.dot(h, w2_ref[...], precision=hp,
                                    preferred_element_type=F32) + b2_ref[...]))
    w3 = ((w3_ref0, w3_ref1), (w3_ref2, w3_ref3))
    cb = p_ref.shape[-1]
    row = lax.broadcasted_iota(jnp.int32, (n, cb), 0)
    t01 = row.astype(F32) / float(max(n - 1, 1))
    scale = 1.0 / (2.0 * n)
    wts = jnp.where(row == 0, scale, 2.0 * scale)
    sgn = (1 - 2 * (row & 1)).astype(F32)
    for o in range(HY_ORDER):
        win = jnp.exp(-t01 * jnp.abs(decay_ref[o:o + 1, :]))
        hf = jnp.dot(h, w3[o][0][...], precision=hp, preferred_element_type=F32) * win
        hb = jnp.dot(h, w3[o][1][...], precision=hp, preferred_element_type=F32) * win
        mag = jnp.where(row == 0, jnp.abs(hf + hb), jnp.abs(hf) + jnp.abs(hb))
        inv = 1.0 / (jnp.sum(mag, axis=0, keepdims=True) + EPS)
        even = (hf + hb) * inv
        odd = (hf - hb) * inv
        kre = _dft3(fc_hi, fc_lo, even)
        kim = _dft3(fs_hi, fs_lo, odd)
        knyq = jnp.sum(even * sgn, axis=0, keepdims=True)
        p_ref[3 * o + 0] = kre * wts
        p_ref[3 * o + 1] = jnp.where(row == 0, 0.0, kim * wts)
        p_ref[3 * o + 2] = jnp.where(row == 0, knyq * scale, kre * wts)


def _split_bf16(m):
    hi = m.astype(BF16)
    return hi, (m - hi.astype(F32)).astype(BF16)


def _hyena_spectrum(seq_len, w1, b1, w2, b2, w3, decay):
    feats, fc, fs, _ = _hyena_tables(seq_len)
    feats = jnp.asarray(feats)
    mats = _split_bf16(jnp.asarray(fc)) + _split_bf16(jnp.asarray(fs))
    w1p = jnp.pad(w1, ((0, LANES - HY_EMB), (0, 0)))
    cb = HY_CB
    nb = D_B // cb

    def w3_spec(k):
        return pl.BlockSpec((HY_HID, cb), lambda c: (0, k * nb + c))

    full2 = lambda a: pl.BlockSpec(a.shape, lambda c: (0, 0))
    b1r, b2r = b1.reshape(1, HY_HID), b2.reshape(1, HY_HID)
    return pl.pallas_call(
        functools.partial(_hyena_filter_kernel, seq_len=seq_len),
        grid=(nb,),
        in_specs=[full2(feats), full2(w1p), full2(b1r), full2(w2), full2(b2r),
                  w3_spec(0), w3_spec(1), w3_spec(2), w3_spec(3),
                  pl.BlockSpec((HY_ORDER, cb), lambda c: (0, c))] + [full2(m) for m in mats],
        out_specs=pl.BlockSpec((3 * HY_ORDER, seq_len, cb), lambda c: (0, 0, c)),
        out_shape=jax.ShapeDtypeStruct((3 * HY_ORDER, seq_len, D_B), F32),
        compiler_params=_params(1),
        name="hyena_filter",
    )(feats, w1p, b1r, w2, b2r, w3, w3, w3, w3, decay, *mats)


def _hyena_conv_kernel(v_ref, x1_ref, x2_ref, cwv_ref, cw1_ref, cw2_ref, cbv_ref, cb1_ref, cb2_ref,
                       p_ref, bias_ref, fc_ref, fs_ref, fst_ref, o_ref, *, seq_len):
    n = seq_len
    rows, cb = o_ref.shape
    nseq = rows // n
    pos = lax.broadcasted_iota(jnp.int32, (rows, cb), 0) & (n - 1)

    def dwconv(x_ref, w_ref, b_ref):
        x = x_ref[...]
        xm = jnp.where(pos == 0, 0.0, pltpu.roll(x, 1, axis=0))
        xp = jnp.where(pos == n - 1, 0.0, pltpu.roll(x, rows - 1, axis=0))
        y = xm * w_ref[0:1, :] + x * w_ref[1:2, :] + xp * w_ref[2:3, :] + b_ref[...]
        return jnp.concatenate([y[s * n:(s + 1) * n] for s in range(nseq)], axis=1)

    def tiled(a):
        return jnp.concatenate([a] * nseq, axis=1)

    def fftconv(z, o):
        zb16 = z.astype(BF16)
        zt = jnp.dot(fc_ref[...], zb16, preferred_element_type=F32)
        zb = jnp.dot(fs_ref[...], zb16, preferred_element_type=F32)
        p1, p3, p4 = tiled(p_ref[3 * o]), tiled(p_ref[3 * o + 1]), tiled(p_ref[3 * o + 2])
        yt = (zt * p1 - zb * p3).astype(BF16)
        yb = (zt * p3 + zb * p4).astype(BF16)
        y = (jnp.dot(fc_ref[...], yt, preferred_element_type=F32)
             + jnp.dot(fst_ref[...], yb, preferred_element_type=F32))
        return y + z * tiled(bias_ref[o:o + 1, :])

    v = dwconv(v_ref, cwv_ref, cbv_ref)
    x1 = dwconv(x1_ref, cw1_ref, cb1_ref)
    z2 = x1 * fftconv(v, 0)
    x2 = dwconv(x2_ref, cw2_ref, cb2_ref)
    out = x2 * fftconv(z2, 1)
    for s in range(nseq):
        o_ref[s * n:(s + 1) * n, :] = out[:, s * cb:(s + 1) * cb].astype(BF16)


def _hyena_mixer(proj, conv_w, conv_b, spectrum, bias, *, row0, n_rows, seq_len):
    assert seq_len & (seq_len - 1) == 0 and HY_ROWS % seq_len == 0 and row0 % HY_ROWS == 0
    _, fc, fs, fst = _hyena_tables(seq_len)
    mats = [jnp.asarray(m).astype(BF16) for m in (fc, fs, fst)]
    cb = HY_CB
    nb = D_B // cb
    rb0 = row0 // HY_ROWS

    def col_spec(k):
        return pl.BlockSpec((HY_ROWS, cb), lambda c, g: (rb0 + g, (COL_HY + k * D_B) // cb + c))

    def cw_spec(k):
        return pl.BlockSpec((3, cb), lambda c, g: (0, k * nb + c))

    def cb_spec(k):
        return pl.BlockSpec((1, cb), lambda c, g: (0, k * nb + c))

    mat_spec = pl.BlockSpec((seq_len, seq_len), lambda c, g: (0, 0))
    conv_b2 = conv_b.reshape(1, 3 * D_B)
    return pl.pallas_call(
        functools.partial(_hyena_conv_kernel, seq_len=seq_len),
        grid=(nb, n_rows // HY_ROWS),
        in_specs=[col_spec(0), col_spec(1), col_spec(2), cw_spec(0), cw_spec(1), cw_spec(2),
                  cb_spec(0), cb_spec(1), cb_spec(2),
                  pl.BlockSpec((3 * HY_ORDER, seq_len, cb), lambda c, g: (0, 0, c)),
                  pl.BlockSpec((HY_ORDER, cb), lambda c, g: (0, c))] + [mat_spec] * 3,
        out_specs=pl.BlockSpec((HY_ROWS, cb), lambda c, g: (g, c)),
        out_shape=jax.ShapeDtypeStruct((n_rows, D_B), BF16),
        compiler_params=_params(2),
        name="hyena_conv",
    )(proj, proj, proj, conv_w, conv_w, conv_w, conv_b2, conv_b2, conv_b2, spectrum, bias, *mats)


def _merge_kernel(oac_ref, oal_ref, obc_ref, obl_ref, ga_ref, gb_ref, wa_ref, wb_ref, o_ref,
                  wabf, wbbf):
    i = pl.program_id(1)

    @pl.when(i == 0)
    def _():
        wabf[...] = wa_ref[...].astype(BF16)
        wbbf[...] = wb_ref[...].astype(BF16)

    is_ctx = i < N_CTX_BLOCKS
    oa = jnp.where(is_ctx, oac_ref[...], oal_ref[...])
    ob = jnp.where(is_ctx, obc_ref[...], obl_ref[...])
    ya = jnp.dot(oa, wabf[...], preferred_element_type=F32)
    yb = jnp.dot(ob, wbbf[...], preferred_element_type=F32)
    o_ref[...] = (jax.nn.sigmoid(ga_ref[...]) * ya + jax.nn.sigmoid(gb_ref[...]) * yb).astype(BF16)


def _merge(oa_ctx, oa_lat, ob_ctx, ob_lat, proj, w_a, w_b, layer):
    tm, tn = TOK_BLOCK, 512
    nbn = D_MODEL // tn
    ctx_blk = lambda j, i: (jnp.minimum(i, N_CTX_BLOCKS - 1), 0)
    lat_blk = lambda j, i: (jnp.maximum(i - N_CTX_BLOCKS, 0), 0)
    return pl.pallas_call(
        _merge_kernel,
        grid=(nbn, N_TOK // tm),
        in_specs=[pl.BlockSpec((tm, D_A), ctx_blk), pl.BlockSpec((tm, D_A), lat_blk),
                  pl.BlockSpec((tm, D_B), ctx_blk), pl.BlockSpec((tm, D_B), lat_blk),
                  pl.BlockSpec((tm, tn), lambda j, i: (i, COL_GATE_A // tn + j)),
                  pl.BlockSpec((tm, tn), lambda j, i: (i, COL_GATE_B // tn + j)),
                  pl.BlockSpec((None, D_A, tn), lambda j, i: (layer, 0, j)),
                  pl.BlockSpec((None, D_B, tn), lambda j, i: (layer, 0, j))],
        out_specs=pl.BlockSpec((tm, tn), lambda j, i: (i, j)),
        out_shape=jax.ShapeDtypeStruct((N_TOK, D_MODEL), BF16),
        scratch_shapes=[pltpu.VMEM((D_A, tn), BF16), pltpu.VMEM((D_B, tn), BF16)],
        compiler_params=_params(2),
        name="merge",
    )(oa_ctx, oa_lat, ob_ctx, ob_lat, proj, proj, w_a, w_b)


def _resid_kernel(*refs, has_next):
    if has_next:
        (a_ref, w_ref, x_ref, gpost_ref, gate_ref, gnext_ref, sh_ref, sc_ref,
         xo_ref, h_ref) = refs
    else:
        a_ref, w_ref, x_ref, gpost_ref, gate_ref, xo_ref = refs
    k = pl.program_id(1)

    @pl.when(k == 0)
    def _():
        xo_ref[...] = jnp.zeros_like(xo_ref)

    xo_ref[...] += jnp.dot(a_ref[...], w_ref[...].astype(BF16), preferred_element_type=F32)

    @pl.when(k == pl.num_programs(1) - 1)
    def _():
        def rows_step(r, carry):
            rows = pl.ds(pl.multiple_of(r * EPILOGUE_ROWS, EPILOGUE_ROWS), EPILOGUE_ROWS)
            xn = x_ref[rows, :] + gate_ref[...] * (_rms(xo_ref[rows, :]) * gpost_ref[...])
            xo_ref[rows, :] = xn
            if has_next:
                h = (_rms(xn) * gnext_ref[...]) * (1.0 + sc_ref[...]) + sh_ref[...]
                h_ref[rows, :] = h.astype(BF16)
            return carry

        lax.fori_loop(0, xo_ref.shape[0] // EPILOGUE_ROWS, rows_step, 0)


def _proj_residual(a, w, x, g_post, mod4, layer, gate_chunk, nxt):
    tm, tk = 1024, 512
    kdim = a.shape[1]
    has_next = nxt is not None
    row_spec = pl.BlockSpec((tm, D_MODEL), lambda i, k: (i, 0))
    x_spec = pl.BlockSpec((tm, D_MODEL), lambda i, k: (i, 0), pipeline_mode=pl.Buffered(1))
    in_specs = [pl.BlockSpec((tm, tk), lambda i, k: (i, k)),
                pl.BlockSpec((None, tk, D_MODEL), lambda i, k: (layer, k, 0)),
                x_spec,
                _gain_spec(layer), _mod_spec(layer, gate_chunk, tm)]
    args = [a, w, x, g_post, mod4]
    out_specs = [row_spec]
    out_shape = [jax.ShapeDtypeStruct((N_TOK, D_MODEL), F32)]
    if has_next:
        g_next, nl, sh_chunk, sc_chunk = nxt
        in_specs += [_gain_spec(nl), _mod_spec(nl, sh_chunk, tm), _mod_spec(nl, sc_chunk, tm)]
        args += [g_next, mod4, mod4]
        out_specs.append(row_spec)
        out_shape.append(jax.ShapeDtypeStruct((N_TOK, D_MODEL), BF16))
    res = pl.pallas_call(
        functools.partial(_resid_kernel, has_next=has_next),
        grid=(N_TOK // tm, kdim // tk),
        in_specs=in_specs,
        out_specs=out_specs,
        out_shape=out_shape,
        compiler_params=_params(2),
        name="proj_residual",
    )(*args)
    return res if has_next else (res[0], None)


def _ffn_up_kernel(h_ref, wa_ref, wv_ref, cwa_ref, cwv_ref, cba_ref, cbv_ref, o_ref, wabf, wvbf):
    i = pl.program_id(1)

    @pl.when(i == 0)
    def _():
        wabf[...] = wa_ref[...].astype(BF16)
        wvbf[...] = wv_ref[...].astype(BF16)

    h = h_ref[...]
    tm, tn = o_ref.shape
    grp = FFN_COL_GROUP
    t = lax.broadcasted_iota(jnp.int32, (tm, LANES), 0)
    span = jnp.where(i < N_CTX_BLOCKS, SEQ - 1, GRID_W - 1)
    pos = t & span
    keep_prev = jnp.concatenate([(pos != 0).astype(F32)] * (grp // LANES), axis=1)
    keep_next = jnp.concatenate([(pos != span).astype(F32)] * (grp // LANES), axis=1)

    for c in range(tn // grp):
        cols = slice(c * grp, (c + 1) * grp)
        ua = jnp.dot(h, wabf[:, cols], preferred_element_type=F32)
        a = _ffn_conv(ua, cwa_ref, cba_ref, cols, keep_prev, keep_next)
        uv = jnp.dot(h, wvbf[:, cols], preferred_element_type=F32)
        vv = _ffn_conv(uv, cwv_ref, cbv_ref, cols, keep_prev, keep_next)
        o_ref[:, cols] = (_silu(a) * vv).astype(BF16)


def _ffn_conv(u, cw_ref, cb_ref, cols, keep_prev, keep_next):
    tm, grp = u.shape
    zrows = jnp.zeros((GRID_W, grp), F32)
    um = pltpu.roll(u, 1, axis=0) * keep_prev
    up = pltpu.roll(u, tm - 1, axis=0) * keep_next
    acc = cb_ref[:, cols]
    for dr in (-1, 0, 1):
        r = 3 * (dr + 1)
        s = (um * cw_ref[r:r + 1, cols] + u * cw_ref[r + 1:r + 2, cols]
             + up * cw_ref[r + 2:r + 3, cols])
        if dr == -1:
            s = jnp.concatenate([zrows, s[:tm - GRID_W]], axis=0)
        elif dr == 1:
            s = jnp.concatenate([s[GRID_W:], zrows], axis=0)
        acc = acc + s
    return acc


def _ffn_up(h, w_up, conv_w, conv_b, layer):
    tm, tn = TOK_BLOCK, 512
    nbn = D_FF // tn
    conv_w9 = conv_w.reshape(DEPTH, 9, 2 * D_FF)
    mid = (jnp.arange(9) // 3 == 1).astype(F32)[None, :, None]
    conv_w9 = jnp.stack([conv_w9 * mid, conv_w9], axis=1)
    conv_b2 = conv_b.reshape(DEPTH, 1, 2 * D_FF)
    blk_type = lambda i: i // N_CTX_BLOCKS
    return pl.pallas_call(
        _ffn_up_kernel,
        grid=(nbn, N_TOK // tm),
        in_specs=[pl.BlockSpec((tm, D_MODEL), lambda j, i: (i, 0)),
                  pl.BlockSpec((None, D_MODEL, tn), lambda j, i: (layer, 0, j)),
                  pl.BlockSpec((None, D_MODEL, tn), lambda j, i: (layer, 0, nbn + j)),
                  pl.BlockSpec((None, None, 9, tn), lambda j, i: (layer, blk_type(i), 0, j)),
                  pl.BlockSpec((None, None, 9, tn), lambda j, i: (layer, blk_type(i), 0, nbn + j)),
                  pl.BlockSpec((None, 1, tn), lambda j, i: (layer, 0, j)),
                  pl.BlockSpec((None, 1, tn), lambda j, i: (layer, 0, nbn + j))],
        out_specs=pl.BlockSpec((tm, tn), lambda j, i: (i, j)),
        out_shape=jax.ShapeDtypeStruct((N_TOK, D_FF), BF16),
        scratch_shapes=[pltpu.VMEM((D_MODEL, tn), BF16), pltpu.VMEM((D_MODEL, tn), BF16)],
        compiler_params=_params(2),
        name="ffn_up",
    )(h, w_up, w_up, conv_w9, conv_w9, conv_b2, conv_b2)


def kernel(x_prompt, x_sample, state_hgrn, c, c_ctx, w_mod, b_mod, g_pre_mix, g_post_mix, g_pre_ffn,
           g_post_ffn, w_in, hgrn_lower_bounds, hgrn_norm, hy_conv_w, hy_conv_b, hy_w1, hy_b1, hy_w2,
           hy_b2, hy_w3, hy_decay, hy_bias, w_branch_a, w_branch_b, w_out, ffn_w_up, ffn_conv_w,
           ffn_conv_b, ffn_w_down):
    p_lb = jax.nn.softmax(hgrn_lower_bounds.astype(F32), axis=0)
    cs = jnp.cumsum(p_lb, axis=0)
    lbs = cs - cs[:1]

    x = jnp.concatenate([x_prompt.reshape(N_CTX, D_MODEL), x_sample.reshape(N_LAT, D_MODEL)], axis=0)
    cond = jnp.concatenate([jnp.broadcast_to(c_ctx[None, :], (N_CTX_BLOCKS, D_MODEL)), c], axis=0)
    mod4 = _modulation(cond, w_mod, b_mod).reshape(DEPTH, N_TOK_BLOCKS, 1, N_MOD)

    g_pre_mix, g_post_mix, g_pre_ffn, g_post_ffn = (
        g.reshape(DEPTH, 1, D_MODEL) for g in (g_pre_mix, g_post_mix, g_pre_ffn, g_post_ffn))

    h = _prenorm(x, g_pre_mix, mod4, 0)
    new_states = []
    for l in range(DEPTH):
        proj = _in_proj(h, w_in, l)
        oa_ctx, s_ctx = _hgrn_mixer(proj, lbs[l], hgrn_norm[l], None, l,
                                    row_block0=0, batch=BATCH, seq_len=SEQ)
        oa_lat, _ = _hgrn_mixer(proj, lbs[l], hgrn_norm[l], state_hgrn, l,
                                row_block0=N_CTX // DEC_SEQ, batch=DEC_BATCH, seq_len=DEC_SEQ)
        new_states.append(s_ctx)
        ob = []
        for row0, n_rows, seq_len in ((0, N_CTX, SEQ), (N_CTX, N_LAT, DEC_SEQ)):
            spectrum = _hyena_spectrum(seq_len, hy_w1[l], hy_b1[l], hy_w2[l], hy_b2[l], hy_w3[l],
                                       hy_decay[l])
            ob.append(_hyena_mixer(proj, hy_conv_w[l], hy_conv_b[l], spectrum, hy_bias[l],
                                   row0=row0, n_rows=n_rows, seq_len=seq_len))
        merged = _merge(oa_ctx, oa_lat, ob[0], ob[1], proj, w_branch_a, w_branch_b, l)
        x, h2 = _proj_residual(merged, w_out, x, g_post_mix, mod4, l, 2, (g_pre_ffn, l, 3, 4))
        g = _ffn_up(h2, ffn_w_up, ffn_conv_w, ffn_conv_b, l)
        nxt = (g_pre_mix, l + 1, 0, 1) if l + 1 < DEPTH else None
        x, h = _proj_residual(g, ffn_w_down, x, g_post_ffn, mod4, l, 5, nxt)

    y_prompt = x[:N_CTX].reshape(BATCH, SEQ, D_MODEL)
    y_sample = x[N_CTX:].reshape(DEC_BATCH, DEC_SEQ, D_MODEL)
    new_state = jnp.stack(new_states, axis=1).astype(x_prompt.dtype)
    return (y_prompt, y_sample, new_state)
```

```python
import functools
import math

import numpy as np
import jax
import jax.numpy as jnp
from jax import lax
from jax.experimental import pallas as pl
from jax.experimental.pallas import tpu as pltpu

F32 = jnp.float32
BF16 = jnp.bfloat16

D_MODEL = 2048
BATCH = 16
SEQ = 256
DEPTH = 2
DEC_BATCH = 4
DEC_SEQ = 1024
GRID_W = 64
D_A = D_MODEL // 2
D_B = D_MODEL // 2
HGRN_EXPAND = 128
H_A = D_A // HGRN_EXPAND
DK_A = HGRN_EXPAND
DV_A = D_A // H_A
HY_ORDER = 2
HY_EMB = 33
HY_BANDS = (HY_EMB - 1) // 2
HY_HID = 64
HY_SIN_W = 1.0
D_FF = 5632
N_MOD = 6 * D_MODEL
N_PROJ = 5 * D_A + 3 * D_B + 2 * D_MODEL
EPS = 1e-6

N_CTX = BATCH * SEQ
N_LAT = DEC_BATCH * DEC_SEQ
N_TOK = N_CTX + N_LAT
TOK_BLOCK = 1024
N_TOK_BLOCKS = N_TOK // TOK_BLOCK
N_CTX_BLOCKS = N_CTX // TOK_BLOCK

COL_Q, COL_FA, COL_FB, COL_I, COL_G = 0, D_A, 2 * D_A, 3 * D_A, 4 * D_A
COL_HY = 5 * D_A
COL_GATE_A = 5 * D_A + 3 * D_B
COL_GATE_B = COL_GATE_A + D_MODEL

LANES = 128
SCAN_CHUNK = 32
SCAN_UNROLL = 8
HY_CB = 256
HY_ROWS = 2048
FFN_COL_GROUP = 256
EPILOGUE_ROWS = 128
VMEM_LIMIT = 56 * 1024 * 1024


def _params(n_axes):
    return pltpu.CompilerParams(dimension_semantics=("arbitrary",) * n_axes,
                                vmem_limit_bytes=VMEM_LIMIT)


def _silu(x):
    return x * jax.nn.sigmoid(x)


def _rms(x):
    return x * lax.rsqrt(jnp.mean(x * x, axis=-1, keepdims=True) + EPS)


def _mod_spec(layer, chunk, tm, block0=0):
    return pl.BlockSpec((None, None, 1, D_MODEL),
                        lambda i, *_: (layer, ((block0 + i) * tm) // TOK_BLOCK, 0, chunk))


def _gain_spec(layer):
    return pl.BlockSpec((None, 1, D_MODEL), lambda *_: (layer, 0, 0))


def _mod_kernel(cond_ref, w_ref, b_ref, o_ref):
    s = _silu(cond_ref[...]).astype(BF16)
    o_ref[...] = jnp.dot(s, w_ref[...].astype(BF16), preferred_element_type=F32) + b_ref[...]


def _modulation(cond, w_mod, b_mod):
    tn = 1024
    return pl.pallas_call(
        _mod_kernel,
        grid=(DEPTH, N_MOD // tn),
        in_specs=[pl.BlockSpec((N_TOK_BLOCKS, D_MODEL), lambda l, j: (0, 0)),
                  pl.BlockSpec((None, D_MODEL, tn), lambda l, j: (l, 0, j)),
                  pl.BlockSpec((None, 1, tn), lambda l, j: (l, 0, j))],
        out_specs=pl.BlockSpec((None, N_TOK_BLOCKS, tn), lambda l, j: (l, 0, j)),
        out_shape=jax.ShapeDtypeStruct((DEPTH, N_TOK_BLOCKS, N_MOD), F32),
        compiler_params=_params(2),
        name="modulation",
    )(cond, w_mod, b_mod.reshape(DEPTH, 1, N_MOD))


def _prenorm_kernel(x_ref, g_ref, sh_ref, sc_ref, h_ref):
    h = (_rms(x_ref[...]) * g_ref[...]) * (1.0 + sc_ref[...]) + sh_ref[...]
    h_ref[...] = h.astype(BF16)


def _prenorm(x, g, mod4, layer):
    tm = 512
    return pl.pallas_call(
        _prenorm_kernel,
        grid=(N_TOK // tm,),
        in_specs=[pl.BlockSpec((tm, D_MODEL), lambda i: (i, 0)),
                  _gain_spec(layer), _mod_spec(layer, 0, tm), _mod_spec(layer, 1, tm)],
        out_specs=pl.BlockSpec((tm, D_MODEL), lambda i: (i, 0)),
        out_shape=jax.ShapeDtypeStruct((N_TOK, D_MODEL), BF16),
        compiler_params=_params(1),
        name="prenorm",
    )(x, g, mod4, mod4)


def _inproj_kernel(h_ref, w_ref, o_ref, wbf_ref):
    @pl.when(pl.program_id(1) == 0)
    def _():
        wbf_ref[...] = w_ref[...].astype(BF16)

    o_ref[...] = jnp.dot(h_ref[...], wbf_ref[...], preferred_element_type=F32)


def _in_proj(h, w_in, layer):
    tm, tn = 1024, 1024
    return pl.pallas_call(
        _inproj_kernel,
        grid=(N_PROJ // tn, N_TOK // tm),
        in_specs=[pl.BlockSpec((tm, D_MODEL), lambda j, i: (i, 0)),
                  pl.BlockSpec((None, D_MODEL, tn), lambda j, i: (layer, 0, j))],
        out_specs=pl.BlockSpec((tm, tn), lambda j, i: (i, j)),
        out_shape=jax.ShapeDtypeStruct((N_TOK, N_PROJ), F32),
        scratch_shapes=[pltpu.VMEM((D_MODEL, tn), BF16)],
        compiler_params=_params(2),
        name="in_proj",
    )(h, w_in)


def _scan_levels(chunk):
    m, levels = chunk // 2, []
    while m >= 1:
        levels.append(m)
        m //= 2
    return tuple(levels)


@functools.lru_cache(maxsize=None)
def _scan_tables(chunk):
    levels = _scan_levels(chunk)
    amats, rowsels, pairs = [], [], []
    for rev in (False, True):
        blocks, sels = [], []
        for m in levels:
            a = np.zeros((chunk, chunk), np.float32)
            sel = np.zeros((chunk,), np.float32)
            for t in range(chunk):
                mid = 2 * m * (t // (2 * m)) + m - 1
                upper = t > mid
                if not rev:
                    if upper:
                        a[t, mid + 1:t + 1] = 1.0
                    else:
                        a[t, t + 1:mid + 1] = 1.0
                    sel[t] = 1.0 if upper else 0.0
                else:
                    if upper:
                        a[t, mid + 1:t] = 1.0
                    else:
                        a[t, t:mid + 1] = 1.0
                    sel[t] = 0.0 if upper else 1.0
            blocks.append(a)
            sels.append(np.repeat(sel[:, None], LANES, axis=1))
        idx = np.arange(chunk)
        if not rev:
            inter = (idx[None, :] <= idx[:, None]).astype(np.float32)
            carry = (idx[None, :] > idx[:, None]).astype(np.float32)
        else:
            inter = (idx[None, :] >= idx[:, None]).astype(np.float32)
            carry = (idx[None, :] < idx[:, None]).astype(np.float32)
        blocks += [inter, carry, np.ones((16, chunk), np.float32)]
        amats.append(np.concatenate(blocks, axis=0))
        rowsels.append(np.stack(sels))
        masks = []
        for m, sel in zip(levels, sels):
            same = idx[:, None] // (2 * m) == idx[None, :] // (2 * m)
            masks.append((same & (sel[:, :1] > 0.5) & (sel[:, :1].T < 0.5)).astype(np.float32))
        masks.append(np.eye(chunk, dtype=np.float32))
        pairs.append(np.stack(masks))
    return np.stack(amats), np.stack(rowsels), np.stack(pairs)


def _dot_exact_rhs(a_bf16, x):
    x1 = x.astype(BF16)
    x2 = (x - x1.astype(F32)).astype(BF16)
    n = x.shape[1]
    y = jnp.dot(a_bf16, jnp.concatenate([x1, x2], axis=1), preferred_element_type=F32)
    return y[:, :n] + y[:, n:]


_NT = (((1,), (1,)), ((), ()))
_TN = (((0,), (0,)), ((), ()))


def _hgrn_kernel(*refs, seq_len, chunk, has_s0):
    if has_s0:
        (q_ref, fa_ref, fb_ref, i_ref, g_ref, lb_ref, nw_ref, amat_ref, rowsel_ref, pair_ref, s0_ref,
         o_ref, sfin_ref, of_scr, ob_scr, st_scr) = refs
    else:
        (q_ref, fa_ref, fb_ref, i_ref, g_ref, lb_ref, nw_ref, amat_ref, rowsel_ref, pair_ref,
         o_ref, sfin_ref, of_scr, ob_scr, st_scr) = refs
    n_chunks = seq_len // chunk
    n_lev = len(_scan_levels(chunk))

    for d in range(2):
        if has_s0:
            st_scr[d] = s0_ref[d].T
        else:
            st_scr[d] = jnp.zeros((DV_A, DK_A), F32)

    f_refs = (fa_ref, fb_ref)
    o_scrs = (of_scr, ob_scr)

    def gates(d, rows):
        z = f_refs[d][rows, :]
        q = _silu(q_ref[rows, :])
        lbd = lb_ref[d:d + 1, :]
        e = jnp.exp(-jnp.abs(z))
        den = 1.0 + e
        a1 = jnp.log(lbd)
        a2 = jnp.log1p(-lbd) + (jnp.minimum(z, 0.0) - jnp.log(den))
        logf = jnp.maximum(a1, a2) + jnp.log(1.0 + jnp.exp(-jnp.abs(a1 - a2)))
        k = (1.0 - lbd) * (jnp.where(z >= 0.0, e, 1.0) / den)
        return q, k, logf

    def body(jj, carry):
        chains = []
        for d in range(2):
            for u in range(SCAN_UNROLL):
                j = jj * SCAN_UNROLL + u
                if d == 1:
                    j = n_chunks - 1 - j
                chains.append((d, pl.ds(pl.multiple_of(j * chunk, chunk), chunk)))
        qkf = [gates(d, rows) for d, rows in chains]
        ws = [jnp.exp(_dot_exact_rhs(amat_ref[d], logf))
              for (d, _), (_, _, logf) in zip(chains, qkf)]
        vbs = [i_ref[rows, :].astype(BF16) for _, rows in chains]
        atts, dsts = [], []
        for (d, _), (q, k, _), w, vb in zip(chains, qkf, ws, vbs):
            att = lax.dot_general(q.astype(BF16), k.astype(BF16), _NT,
                                  preferred_element_type=F32) * pair_ref[d, n_lev]
            for lv in range(n_lev):
                x = (jnp.where(rowsel_ref[d, lv] > 0.5, q, k)
                     * w[lv * chunk:(lv + 1) * chunk]).astype(BF16)
                att = att + lax.dot_general(x, x, _NT, preferred_element_type=F32) * pair_ref[d, lv]
            atts.append(att.astype(BF16))
            wk = w[(n_lev + 1) * chunk:(n_lev + 2) * chunk]
            dsts.append(lax.dot_general(vb, (k * wk).astype(BF16), _TN, preferred_element_type=F32))
        for d in range(2):
            st = st_scr[d]
            for u in range(SCAN_UNROLL):
                c = d * SCAN_UNROLL + u
                q, w = qkf[c][0], ws[c]
                wq = w[n_lev * chunk:(n_lev + 1) * chunk]
                wall = w[(n_lev + 2) * chunk:(n_lev + 2) * chunk + 1]
                o = lax.dot_general((q * wq).astype(BF16), st.astype(BF16), _NT,
                                    preferred_element_type=F32)
                o = o + jnp.dot(atts[c], vbs[c], preferred_element_type=F32)
                o_scrs[d][chains[c][1], :] = o
                st = st * wall + dsts[c]
            st_scr[d] = st
        return carry

    lax.fori_loop(0, n_chunks // SCAN_UNROLL, body, 0)

    o = of_scr[...] + ob_scr[...]
    y = _rms(o) * nw_ref[...]
    o_ref[...] = (y * _silu(g_ref[...])).astype(BF16)
    for d in range(2):
        sfin_ref[d] = st_scr[d].T


def _hgrn_mixer(proj, lb, norm_w, s0, layer, *, row_block0, batch, seq_len):
    chunk = SCAN_CHUNK
    amat, rowsel, pair = _scan_tables(chunk)
    amat = jnp.asarray(amat).astype(BF16)
    rowsel = jnp.asarray(rowsel)
    pair = jnp.asarray(pair)
    has_s0 = s0 is not None

    def col_spec(col):
        return pl.BlockSpec((seq_len, LANES), lambda b, h: (row_block0 + b, col // LANES + h))

    def const_spec(a):
        nd = a.ndim
        return pl.BlockSpec(a.shape, lambda b, h: (0,) * nd)

    in_specs = [col_spec(COL_Q), col_spec(COL_FA), col_spec(COL_FB), col_spec(COL_I), col_spec(COL_G),
                pl.BlockSpec((2, LANES), lambda b, h: (0, h)),
                pl.BlockSpec((1, LANES), lambda b, h: (0, 0)),
                const_spec(amat), const_spec(rowsel), const_spec(pair)]
    args = [proj, proj, proj, proj, proj, lb, norm_w.reshape(1, DV_A), amat, rowsel, pair]
    if has_s0:
        in_specs.append(pl.BlockSpec((None, None, 2, None, DK_A, DV_A),
                                     lambda b, h: (b, layer, 0, h, 0, 0)))
        args.append(s0)
    kern = functools.partial(_hgrn_kernel, seq_len=seq_len, chunk=chunk, has_s0=has_s0)
    return pl.pallas_call(
        kern,
        grid=(batch, H_A),
        in_specs=in_specs,
        out_specs=[pl.BlockSpec((seq_len, LANES), lambda b, h: (b, h)),
                   pl.BlockSpec((None, 2, None, DK_A, DV_A), lambda b, h: (b, 0, h, 0, 0))],
        out_shape=[jax.ShapeDtypeStruct((batch * seq_len, D_A), BF16),
                   jax.ShapeDtypeStruct((batch, 2, H_A, DK_A, DV_A), F32)],
        scratch_shapes=[pltpu.VMEM((seq_len, DV_A), F32), pltpu.VMEM((seq_len, DV_A), F32),
                        pltpu.VMEM((2, DV_A, DK_A), F32)],
        compiler_params=_params(2),
        name="hgrn_mixer",
    )(*args)


@functools.lru_cache(maxsize=None)
def _hyena_tables(seq_len):
    n = seq_len
    t = np.arange(n, dtype=np.float64)
    t01 = t / max(n - 1, 1)
    bands = np.linspace(1e-4, HY_BANDS - 1, HY_BANDS)
    ang = (2.0 * math.pi / n) * t[:, None] * bands[None, :]
    feats = np.concatenate([t01[:, None], np.cos(ang), -np.sin(ang)], axis=-1)
    feats = np.pad(feats, ((0, 0), (0, LANES - HY_EMB))).astype(np.float32)
    ft = np.outer(np.arange(n), np.arange(n)) % (2 * n)
    fc = np.cos(math.pi * ft / n)
    fs = -np.sin(math.pi * ft / n)
    fs[0, :] = 1.0 - 2.0 * (np.arange(n) % 2)
    return feats, fc.astype(np.float32), fs.astype(np.float32), fs.T.astype(np.float32).copy()


def _dft3(f_hi_ref, f_lo_ref, x):
    xh = x.astype(BF16)
    xl = (x - xh.astype(F32)).astype(BF16)
    return (jnp.dot(f_hi_ref[...], xh, preferred_element_type=F32)
            + jnp.dot(f_hi_ref[...], xl, preferred_element_type=F32)
            + jnp.dot(f_lo_ref[...], xh, preferred_element_type=F32))


def _hyena_filter_kernel(feats_ref, w1_ref, b1_ref, w2_ref, b2_ref, w3_ref0, w3_ref1, w3_ref2, w3_ref3,
                         decay_ref, fc_hi, fc_lo, fs_hi, fs_lo, p_ref, h_scr, *, seq_len):
    n = seq_len
    hp = lax.Precision.HIGHEST

    @pl.when(pl.program_id(0) == 0)
    def _():
        h1 = jnp.sin(HY_SIN_W * (jnp.dot(feats_ref[...], w1_ref[...], precision=hp,
                                         preferred_element_type=F32) + b1_ref[...]))
        h_scr[...] = jnp.sin(HY_SIN_W * (jnp.dot(h1, w2_ref[...], precision=hp,
                                                 preferred_element_type=F32) + b2_ref[...]))

    h = h_scr[...]
    w3 = ((w3_ref0, w3_ref1), (w3_ref2, w3_ref3))
    cb = p_ref.shape[-1]
    row = lax.broadcasted_iota(jnp.int32, (n, cb), 0)
    t01 = row.astype(F32) / float(max(n - 1, 1))
    scale = 1.0 / (2.0 * n)
    wts = jnp.where(row == 0, scale, 2.0 * scale)
    sgn = (1 - 2 * (row & 1)).astype(F32)
    for o in range(HY_ORDER):
        win = jnp.exp(-t01 * jnp.abs(decay_ref[o:o + 1, :]))
        hf = jnp.dot(h, w3[o][0][...], precision=hp, preferred_element_type=F32) * win
        hb = jnp.dot(h, w3[o][1][...], precision=hp, preferred_element_type=F32) * win
        mag = jnp.where(row == 0, jnp.abs(hf + hb), jnp.abs(hf) + jnp.abs(hb))
        inv = 1.0 / (jnp.sum(mag, axis=0, keepdims=True) + EPS)
        even = (hf + hb) * inv
        odd = (hf - hb) * inv
        kre = _dft3(fc_hi, fc_lo, even)
        kim = _dft3(fs_hi, fs_lo, odd)
        knyq = jnp.sum(even * sgn, axis=0, keepdims=True)
        p_ref[3 * o + 0] = kre * wts
        p_ref[3 * o + 1] = jnp.where(row == 0, 0.0, kim * wts)
        p_ref[3 * o + 2] = jnp.where(row == 0, knyq * scale, kre * wts)


def _split_bf16(m):
    hi = m.astype(BF16)
    return hi, (m - hi.astype(F32)).astype(BF16)


def _hyena_spectrum(seq_len, w1, b1, w2, b2, w3, decay):
    feats, fc, fs, _ = _hyena_tables(seq_len)
    feats = jnp.asarray(feats)
    mats = _split_bf16(jnp.asarray(fc)) + _split_bf16(jnp.asarray(fs))
    w1p = jnp.pad(w1, ((0, LANES - HY_EMB), (0, 0)))
    cb = HY_CB
    nb = D_B // cb

    def w3_spec(k):
        return pl.BlockSpec((HY_HID, cb), lambda c: (0, k * nb + c))

    full2 = lambda a: pl.BlockSpec(a.shape, lambda c: (0, 0))
    b1r, b2r = b1.reshape(1, HY_HID), b2.reshape(1, HY_HID)
    return pl.pallas_call(
        functools.partial(_hyena_filter_kernel, seq_len=seq_len),
        grid=(nb,),
        in_specs=[full2(feats), full2(w1p), full2(b1r), full2(w2), full2(b2r),
                  w3_spec(0), w3_spec(1), w3_spec(2), w3_spec(3),
                  pl.BlockSpec((HY_ORDER, cb), lambda c: (0, c))] + [full2(m) for m in mats],
        out_specs=pl.BlockSpec((3 * HY_ORDER, seq_len, cb), lambda c: (0, 0, c)),
        out_shape=jax.ShapeDtypeStruct((3 * HY_ORDER, seq_len, D_B), F32),
        scratch_shapes=[pltpu.VMEM((seq_len, HY_HID), F32)],
        compiler_params=_params(1),
        name="hyena_filter",
    )(feats, w1p, b1r, w2, b2r, w3, w3, w3, w3, decay, *mats)


def _hyena_conv_kernel(v_ref, x1_ref, x2_ref, cwv_ref, cw1_ref, cw2_ref, cbv_ref, cb1_ref, cb2_ref,
                       p_ref, bias_ref, fc_ref, fs_ref, fst_ref, o_ref, *, seq_len):
    n = seq_len
    rows, cb = o_ref.shape
    nseq = rows // n
    pos = lax.broadcasted_iota(jnp.int32, (rows, cb), 0) & (n - 1)

    def dwconv(x_ref, w_ref, b_ref):
        x = x_ref[...]
        xm = jnp.where(pos == 0, 0.0, pltpu.roll(x, 1, axis=0))
        xp = jnp.where(pos == n - 1, 0.0, pltpu.roll(x, rows - 1, axis=0))
        y = xm * w_ref[0:1, :] + x * w_ref[1:2, :] + xp * w_ref[2:3, :] + b_ref[...]
        return jnp.concatenate([y[s * n:(s + 1) * n] for s in range(nseq)], axis=1)

    def tiled(a):
        return jnp.concatenate([a] * nseq, axis=1)

    def fftconv(z, o):
        zb16 = z.astype(BF16)
        zt = jnp.dot(fc_ref[...], zb16, preferred_element_type=F32)
        zb = jnp.dot(fs_ref[...], zb16, preferred_element_type=F32)
        p1, p3, p4 = tiled(p_ref[3 * o]), tiled(p_ref[3 * o + 1]), tiled(p_ref[3 * o + 2])
        yt = (zt * p1 - zb * p3).astype(BF16)
        yb = (zt * p3 + zb * p4).astype(BF16)
        y = (jnp.dot(fc_ref[...], yt, preferred_element_type=F32)
             + jnp.dot(fst_ref[...], yb, preferred_element_type=F32))
        return y + z * tiled(bias_ref[o:o + 1, :])

    v = dwconv(v_ref, cwv_ref, cbv_ref)
    x1 = dwconv(x1_ref, cw1_ref, cb1_ref)
    z2 = x1 * fftconv(v, 0)
    x2 = dwconv(x2_ref, cw2_ref, cb2_ref)
    out = x2 * fftconv(z2, 1)
    for s in range(nseq):
        o_ref[s * n:(s + 1) * n, :] = out[:, s * cb:(s + 1) * cb].astype(BF16)


def _hyena_mixer(proj, conv_w, conv_b, spectrum, bias, *, row0, n_rows, seq_len):
    assert seq_len & (seq_len - 1) == 0 and HY_ROWS % seq_len == 0 and row0 % HY_ROWS == 0
    _, fc, fs, fst = _hyena_tables(seq_len)
    mats = [jnp.asarray(m).astype(BF16) for m in (fc, fs, fst)]
    cb = HY_CB
    nb = D_B // cb
    rb0 = row0 // HY_ROWS

    def col_spec(k):
        return pl.BlockSpec((HY_ROWS, cb), lambda c, g: (rb0 + g, (COL_HY + k * D_B) // cb + c))

    def cw_spec(k):
        return pl.BlockSpec((3, cb), lambda c, g: (0, k * nb + c))

    def cb_spec(k):
        return pl.BlockSpec((1, cb), lambda c, g: (0, k * nb + c))

    mat_spec = pl.BlockSpec((seq_len, seq_len), lambda c, g: (0, 0))
    conv_b2 = conv_b.reshape(1, 3 * D_B)
    return pl.pallas_call(
        functools.partial(_hyena_conv_kernel, seq_len=seq_len),
        grid=(nb, n_rows // HY_ROWS),
        in_specs=[col_spec(0), col_spec(1), col_spec(2), cw_spec(0), cw_spec(1), cw_spec(2),
                  cb_spec(0), cb_spec(1), cb_spec(2),
                  pl.BlockSpec((3 * HY_ORDER, seq_len, cb), lambda c, g: (0, 0, c)),
                  pl.BlockSpec((HY_ORDER, cb), lambda c, g: (0, c))] + [mat_spec] * 3,
        out_specs=pl.BlockSpec((HY_ROWS, cb), lambda c, g: (g, c)),
        out_shape=jax.ShapeDtypeStruct((n_rows, D_B), BF16),
        compiler_params=_params(2),
        name="hyena_conv",
    )(proj, proj, proj, conv_w, conv_w, conv_w, conv_b2, conv_b2, conv_b2, spectrum, bias, *mats)


def _merge_kernel(oac_ref, oal_ref, obc_ref, obl_ref, ga_ref, gb_ref, wa_ref, wb_ref, o_ref,
                  wabf, wbbf):
    i = pl.program_id(1)

    @pl.when(i == 0)
    def _():
        wabf[...] = wa_ref[...].astype(BF16)
        wbbf[...] = wb_ref[...].astype(BF16)

    is_ctx = i < N_CTX_BLOCKS
    oa = jnp.where(is_ctx, oac_ref[...], oal_ref[...])
    ob = jnp.where(is_ctx, obc_ref[...], obl_ref[...])
    ya = jnp.dot(oa, wabf[...], preferred_element_type=F32)
    yb = jnp.dot(ob, wbbf[...], preferred_element_type=F32)
    o_ref[...] = (jax.nn.sigmoid(ga_ref[...]) * ya + jax.nn.sigmoid(gb_ref[...]) * yb).astype(BF16)


def _merge(oa_ctx, oa_lat, ob_ctx, ob_lat, proj, w_a, w_b, layer):
    tm, tn = TOK_BLOCK, 512
    nbn = D_MODEL // tn
    ctx_blk = lambda j, i: (jnp.minimum(i, N_CTX_BLOCKS - 1), 0)
    lat_blk = lambda j, i: (jnp.maximum(i - N_CTX_BLOCKS, 0), 0)
    return pl.pallas_call(
        _merge_kernel,
        grid=(nbn, N_TOK // tm),
        in_specs=[pl.BlockSpec((tm, D_A), ctx_blk), pl.BlockSpec((tm, D_A), lat_blk),
                  pl.BlockSpec((tm, D_B), ctx_blk), pl.BlockSpec((tm, D_B), lat_blk),
                  pl.BlockSpec((tm, tn), lambda j, i: (i, COL_GATE_A // tn + j)),
                  pl.BlockSpec((tm, tn), lambda j, i: (i, COL_GATE_B // tn + j)),
                  pl.BlockSpec((None, D_A, tn), lambda j, i: (layer, 0, j)),
                  pl.BlockSpec((None, D_B, tn), lambda j, i: (layer, 0, j))],
        out_specs=pl.BlockSpec((tm, tn), lambda j, i: (i, j)),
        out_shape=jax.ShapeDtypeStruct((N_TOK, D_MODEL), BF16),
        scratch_shapes=[pltpu.VMEM((D_A, tn), BF16), pltpu.VMEM((D_B, tn), BF16)],
        compiler_params=_params(2),
        name="merge",
    )(oa_ctx, oa_lat, ob_ctx, ob_lat, proj, proj, w_a, w_b)


def _resid_kernel(*refs, has_next):
    if has_next:
        (a_ref, w_ref, x_ref, gpost_ref, gate_ref, gnext_ref, sh_ref, sc_ref,
         xo_ref, h_ref) = refs
    else:
        a_ref, w_ref, x_ref, gpost_ref, gate_ref, xo_ref = refs
    k = pl.program_id(1)

    @pl.when(k == 0)
    def _():
        xo_ref[...] = jnp.zeros_like(xo_ref)

    xo_ref[...] += jnp.dot(a_ref[...], w_ref[...].astype(BF16), preferred_element_type=F32)

    @pl.when(k == pl.num_programs(1) - 1)
    def _():
        def rows_step(r, carry):
            rows = pl.ds(pl.multiple_of(r * EPILOGUE_ROWS, EPILOGUE_ROWS), EPILOGUE_ROWS)
            xn = x_ref[rows, :] + gate_ref[...] * (_rms(xo_ref[rows, :]) * gpost_ref[...])
            xo_ref[rows, :] = xn
            if has_next:
                h = (_rms(xn) * gnext_ref[...]) * (1.0 + sc_ref[...]) + sh_ref[...]
                h_ref[rows, :] = h.astype(BF16)
            return carry

        lax.fori_loop(0, xo_ref.shape[0] // EPILOGUE_ROWS, rows_step, 0)


def _proj_residual(a, w, x, g_post, mod4, layer, gate_chunk, nxt, blocks=(0, N_TOK_BLOCKS)):
    tm, tk = TOK_BLOCK, 512
    kdim = a.shape[1]
    has_next = nxt is not None
    b0, nblk = blocks
    in_row_spec = pl.BlockSpec((tm, D_MODEL), lambda i, k: (b0 + i, 0))
    row_spec = pl.BlockSpec((tm, D_MODEL), lambda i, k: (i, 0))
    in_specs = [pl.BlockSpec((tm, tk), lambda i, k: (b0 + i, k)),
                pl.BlockSpec((None, tk, D_MODEL), lambda i, k: (layer, k, 0)),
                in_row_spec,
                _gain_spec(layer), _mod_spec(layer, gate_chunk, tm, b0)]
    args = [a, w, x, g_post, mod4]
    out_specs = [row_spec]
    out_shape = [jax.ShapeDtypeStruct((nblk * tm, D_MODEL), F32)]
    if has_next:
        g_next, nl, sh_chunk, sc_chunk = nxt
        in_specs += [_gain_spec(nl), _mod_spec(nl, sh_chunk, tm, b0), _mod_spec(nl, sc_chunk, tm, b0)]
        args += [g_next, mod4, mod4]
        out_specs.append(row_spec)
        out_shape.append(jax.ShapeDtypeStruct((nblk * tm, D_MODEL), BF16))
    res = pl.pallas_call(
        functools.partial(_resid_kernel, has_next=has_next),
        grid=(nblk, kdim // tk),
        in_specs=in_specs,
        out_specs=out_specs,
        out_shape=out_shape,
        compiler_params=_params(2),
        name="proj_residual",
    )(*args)
    return res if has_next else (res[0], None)


def _ffn_up_kernel(h_ref, wa_ref, wv_ref, cwa_ref, cwv_ref, cba_ref, cbv_ref, o_ref, wabf, wvbf):
    i = pl.program_id(1)

    @pl.when(i == 0)
    def _():
        wabf[...] = wa_ref[...].astype(BF16)
        wvbf[...] = wv_ref[...].astype(BF16)

    tm, tn = o_ref.shape
    grp = FFN_COL_GROUP
    t = lax.broadcasted_iota(jnp.int32, (tm, LANES), 0)

    def block(row_len, taps_rows):
        pos = t & (row_len - 1)
        keep_prev = jnp.concatenate([(pos != 0).astype(F32)] * (grp // LANES), axis=1)
        keep_next = jnp.concatenate([(pos != row_len - 1).astype(F32)] * (grp // LANES), axis=1)
        h = h_ref[...]
        for c in range(tn // grp):
            cols = slice(c * grp, (c + 1) * grp)
            ua = jnp.dot(h, wabf[:, cols], preferred_element_type=F32)
            a = _ffn_conv(ua, cwa_ref, cba_ref, cols, keep_prev, keep_next, taps_rows)
            uv = jnp.dot(h, wvbf[:, cols], preferred_element_type=F32)
            vv = _ffn_conv(uv, cwv_ref, cbv_ref, cols, keep_prev, keep_next, taps_rows)
            o_ref[:, cols] = (_silu(a) * vv).astype(BF16)

    @pl.when(i < N_CTX_BLOCKS)
    def _():
        block(SEQ, (0,))

    @pl.when(i >= N_CTX_BLOCKS)
    def _():
        block(GRID_W, (-1, 0, 1))


def _ffn_conv(u, cw_ref, cb_ref, cols, keep_prev, keep_next, taps_rows):
    tm, grp = u.shape
    zrows = jnp.zeros((GRID_W, grp), F32)
    um = pltpu.roll(u, 1, axis=0) * keep_prev
    up = pltpu.roll(u, tm - 1, axis=0) * keep_next
    acc = cb_ref[:, cols]
    for dr in taps_rows:
        r = 3 * (dr + 1)
        s = (um * cw_ref[r:r + 1, cols] + u * cw_ref[r + 1:r + 2, cols]
             + up * cw_ref[r + 2:r + 3, cols])
        if dr == -1:
            s = jnp.concatenate([zrows, s[:tm - GRID_W]], axis=0)
        elif dr == 1:
            s = jnp.concatenate([s[GRID_W:], zrows], axis=0)
        acc = acc + s
    return acc


def _ffn_up(h, w_up, conv_w, conv_b, layer):
    tm, tn = TOK_BLOCK, 512
    nbn = D_FF // tn
    conv_w9 = conv_w.reshape(DEPTH, 9, 2 * D_FF)
    conv_b2 = conv_b.reshape(DEPTH, 1, 2 * D_FF)
    return pl.pallas_call(
        _ffn_up_kernel,
        grid=(nbn, N_TOK // tm),
        in_specs=[pl.BlockSpec((tm, D_MODEL), lambda j, i: (i, 0)),
                  pl.BlockSpec((None, D_MODEL, tn), lambda j, i: (layer, 0, j)),
                  pl.BlockSpec((None, D_MODEL, tn), lambda j, i: (layer, 0, nbn + j)),
                  pl.BlockSpec((None, 9, tn), lambda j, i: (layer, 0, j)),
                  pl.BlockSpec((None, 9, tn), lambda j, i: (layer, 0, nbn + j)),
                  pl.BlockSpec((None, 1, tn), lambda j, i: (layer, 0, j)),
                  pl.BlockSpec((None, 1, tn), lambda j, i: (layer, 0, nbn + j))],
        out_specs=pl.BlockSpec((tm, tn), lambda j, i: (i, j)),
        out_shape=jax.ShapeDtypeStruct((N_TOK, D_FF), BF16),
        scratch_shapes=[pltpu.VMEM((D_MODEL, tn), BF16), pltpu.VMEM((D_MODEL, tn), BF16)],
        compiler_params=_params(2),
        name="ffn_up",
    )(h, w_up, w_up, conv_w9, conv_w9, conv_b2, conv_b2)


def kernel(x_prompt, x_sample, state_hgrn, c, c_ctx, w_mod, b_mod, g_pre_mix, g_post_mix, g_pre_ffn,
           g_post_ffn, w_in, hgrn_lower_bounds, hgrn_norm, hy_conv_w, hy_conv_b, hy_w1, hy_b1, hy_w2,
           hy_b2, hy_w3, hy_decay, hy_bias, w_branch_a, w_branch_b, w_out, ffn_w_up, ffn_conv_w,
           ffn_conv_b, ffn_w_down):
    p_lb = jax.nn.softmax(hgrn_lower_bounds.astype(F32), axis=0)
    cs = jnp.cumsum(p_lb, axis=0)
    lbs = cs - cs[:1]

    x = jnp.concatenate([x_prompt.reshape(N_CTX, D_MODEL), x_sample.reshape(N_LAT, D_MODEL)], axis=0)
    cond = jnp.concatenate([jnp.broadcast_to(c_ctx[None, :], (N_CTX_BLOCKS, D_MODEL)), c], axis=0)
    mod4 = _modulation(cond, w_mod, b_mod).reshape(DEPTH, N_TOK_BLOCKS, 1, N_MOD)

    g_pre_mix, g_post_mix, g_pre_ffn, g_post_ffn = (
        g.reshape(DEPTH, 1, D_MODEL) for g in (g_pre_mix, g_post_mix, g_pre_ffn, g_post_ffn))

    h = _prenorm(x, g_pre_mix, mod4, 0)
    new_states = []
    for l in range(DEPTH):
        proj = _in_proj(h, w_in, l)
        oa_ctx, s_ctx = _hgrn_mixer(proj, lbs[l], hgrn_norm[l], None, l,
                                    row_block0=0, batch=BATCH, seq_len=SEQ)
        oa_lat, _ = _hgrn_mixer(proj, lbs[l], hgrn_norm[l], state_hgrn, l,
                                row_block0=N_CTX // DEC_SEQ, batch=DEC_BATCH, seq_len=DEC_SEQ)
        new_states.append(s_ctx)
        ob = []
        for row0, n_rows, seq_len in ((0, N_CTX, SEQ), (N_CTX, N_LAT, DEC_SEQ)):
            spectrum = _hyena_spectrum(seq_len, hy_w1[l], hy_b1[l], hy_w2[l], hy_b2[l], hy_w3[l],
                                       hy_decay[l])
            ob.append(_hyena_mixer(proj, hy_conv_w[l], hy_conv_b[l], spectrum, hy_bias[l],
                                   row0=row0, n_rows=n_rows, seq_len=seq_len))
        merged = _merge(oa_ctx, oa_lat, ob[0], ob[1], proj, w_branch_a, w_branch_b, l)
        x, h2 = _proj_residual(merged, w_out, x, g_post_mix, mod4, l, 2, (g_pre_ffn, l, 3, 4))
        g = _ffn_up(h2, ffn_w_up, ffn_conv_w, ffn_conv_b, l)
        if l + 1 < DEPTH:
            x, h = _proj_residual(g, ffn_w_down, x, g_post_ffn, mod4, l, 5, (g_pre_mix, l + 1, 0, 1))
        else:
            y_ctx, _ = _proj_residual(g, ffn_w_down, x, g_post_ffn, mod4, l, 5, None,
                                      blocks=(0, N_CTX_BLOCKS))
            y_lat, _ = _proj_residual(g, ffn_w_down, x, g_post_ffn, mod4, l, 5, None,
                                      blocks=(N_CTX_BLOCKS, N_TOK_BLOCKS - N_CTX_BLOCKS))

    y_prompt = y_ctx.reshape(BATCH, SEQ, D_MODEL)
    y_sample = y_lat.reshape(DEC_BATCH, DEC_SEQ, D_MODEL)
    new_state = jnp.stack(new_states, axis=1).astype(x_prompt.dtype)
    return (y_prompt, y_sample, new_state)
```

```python
import functools
import math

import numpy as np
import jax
import jax.numpy as jnp
from jax import lax
from jax.experimental import pallas as pl
from jax.experimental.pallas import tpu as pltpu

F32 = jnp.float32
BF16 = jnp.bfloat16

D_MODEL = 2048
BATCH = 16
SEQ = 256
DEPTH = 2
DEC_BATCH = 4
DEC_SEQ = 1024
GRID_W = 64
D_A = D_MODEL // 2
D_B = D_MODEL // 2
HGRN_EXPAND = 128
H_A = D_A // HGRN_EXPAND
DK_A = HGRN_EXPAND
DV_A = D_A // H_A
HY_ORDER = 2
HY_EMB = 33
HY_BANDS = (HY_EMB - 1) // 2
HY_HID = 64
HY_SIN_W = 1.0
D_FF = 5632
N_MOD = 6 * D_MODEL
N_PROJ = 5 * D_A + 3 * D_B + 2 * D_MODEL
EPS = 1e-6

N_CTX = BATCH * SEQ
N_LAT = DEC_BATCH * DEC_SEQ
N_TOK = N_CTX + N_LAT
TOK_BLOCK = 1024
N_TOK_BLOCKS = N_TOK // TOK_BLOCK
N_CTX_BLOCKS = N_CTX // TOK_BLOCK

COL_Q, COL_FA, COL_FB, COL_I, COL_G = 0, D_A, 2 * D_A, 3 * D_A, 4 * D_A
COL_HY = 5 * D_A
COL_GATE_A = 5 * D_A + 3 * D_B
COL_GATE_B = COL_GATE_A + D_MODEL

LANES = 128
SUBLANES = 8
FFN_ROW_TILE = 128
SCAN_CHUNK = 32
SCAN_UNROLL = 8
HY_CB = 256
HY_ROWS = 2048
FFN_COL_GROUP = 256
EPILOGUE_ROWS = 128
VMEM_LIMIT = 60 * 1024 * 1024


def _params(n_axes):
    return pltpu.CompilerParams(dimension_semantics=("arbitrary",) * n_axes,
                                vmem_limit_bytes=VMEM_LIMIT)


def _silu(x):
    return x * jax.nn.sigmoid(x)


def _rms(x):
    return x * lax.rsqrt(jnp.mean(x * x, axis=-1, keepdims=True) + EPS)


def _mod_spec(layer, chunk, tm, block0=0):
    return pl.BlockSpec((None, None, 1, D_MODEL),
                        lambda i, *_: (layer, ((block0 + i) * tm) // TOK_BLOCK, 0, chunk))


def _gain_spec(layer):
    return pl.BlockSpec((None, 1, D_MODEL), lambda *_: (layer, 0, 0))


def _mod_kernel(cond_ref, w_ref, b_ref, o_ref):
    s = _silu(cond_ref[...]).astype(BF16)
    o_ref[...] = jnp.dot(s, w_ref[...].astype(BF16), preferred_element_type=F32) + b_ref[...]


def _modulation(cond, w_mod, b_mod):
    tn = 1024
    return pl.pallas_call(
        _mod_kernel,
        grid=(DEPTH, N_MOD // tn),
        in_specs=[pl.BlockSpec((N_TOK_BLOCKS, D_MODEL), lambda l, j: (0, 0)),
                  pl.BlockSpec((None, D_MODEL, tn), lambda l, j: (l, 0, j)),
                  pl.BlockSpec((None, 1, tn), lambda l, j: (l, 0, j))],
        out_specs=pl.BlockSpec((None, N_TOK_BLOCKS, tn), lambda l, j: (l, 0, j)),
        out_shape=jax.ShapeDtypeStruct((DEPTH, N_TOK_BLOCKS, N_MOD), F32),
        compiler_params=_params(2),
        name="modulation",
    )(cond, w_mod, b_mod.reshape(DEPTH, 1, N_MOD))


def _prenorm_kernel(x_ref, g_ref, sh_ref, sc_ref, h_ref):
    h = (_rms(x_ref[...]) * g_ref[...]) * (1.0 + sc_ref[...]) + sh_ref[...]
    h_ref[...] = h.astype(BF16)


def _prenorm(x, g, mod4, layer):
    tm = 512
    return pl.pallas_call(
        _prenorm_kernel,
        grid=(N_TOK // tm,),
        in_specs=[pl.BlockSpec((tm, D_MODEL), lambda i: (i, 0)),
                  _gain_spec(layer), _mod_spec(layer, 0, tm), _mod_spec(layer, 1, tm)],
        out_specs=pl.BlockSpec((tm, D_MODEL), lambda i: (i, 0)),
        out_shape=jax.ShapeDtypeStruct((N_TOK, D_MODEL), BF16),
        compiler_params=_params(1),
        name="prenorm",
    )(x, g, mod4, mod4)


def _inproj_kernel(h_ref, w_ref, o_ref, wbf_ref):
    @pl.when(pl.program_id(1) == 0)
    def _():
        wbf_ref[...] = w_ref[...].astype(BF16)

    o_ref[...] = jnp.dot(h_ref[...], wbf_ref[...], preferred_element_type=F32)


def _in_proj(h, w_in, layer):
    tm, tn = 1024, 1024
    return pl.pallas_call(
        _inproj_kernel,
        grid=(N_PROJ // tn, N_TOK // tm),
        in_specs=[pl.BlockSpec((tm, D_MODEL), lambda j, i: (i, 0)),
                  pl.BlockSpec((None, D_MODEL, tn), lambda j, i: (layer, 0, j))],
        out_specs=pl.BlockSpec((tm, tn), lambda j, i: (i, j)),
        out_shape=jax.ShapeDtypeStruct((N_TOK, N_PROJ), F32),
        scratch_shapes=[pltpu.VMEM((D_MODEL, tn), BF16)],
        compiler_params=_params(2),
        name="in_proj",
    )(h, w_in)


def _scan_levels(chunk):
    m, levels = chunk // 2, []
    while m >= 1:
        levels.append(m)
        m //= 2
    return tuple(levels)


@functools.lru_cache(maxsize=None)
def _scan_tables(chunk):
    levels = _scan_levels(chunk)
    amats, rowsels, pairs = [], [], []
    for rev in (False, True):
        blocks, sels = [], []
        for m in levels:
            a = np.zeros((chunk, chunk), np.float32)
            sel = np.zeros((chunk,), np.float32)
            for t in range(chunk):
                mid = 2 * m * (t // (2 * m)) + m - 1
                upper = t > mid
                if not rev:
                    if upper:
                        a[t, mid + 1:t + 1] = 1.0
                    else:
                        a[t, t + 1:mid + 1] = 1.0
                    sel[t] = 1.0 if upper else 0.0
                else:
                    if upper:
                        a[t, mid + 1:t] = 1.0
                    else:
                        a[t, t:mid + 1] = 1.0
                    sel[t] = 0.0 if upper else 1.0
            blocks.append(a)
            sels.append(np.repeat(sel[:, None], LANES, axis=1))
        idx = np.arange(chunk)
        if not rev:
            inter = (idx[None, :] <= idx[:, None]).astype(np.float32)
            carry = (idx[None, :] > idx[:, None]).astype(np.float32)
        else:
            inter = (idx[None, :] >= idx[:, None]).astype(np.float32)
            carry = (idx[None, :] < idx[:, None]).astype(np.float32)
        blocks += [inter, carry, np.ones((16, chunk), np.float32)]
        amats.append(np.concatenate(blocks, axis=0))
        rowsels.append(np.stack(sels))
        masks = []
        for m, sel in zip(levels, sels):
            same = idx[:, None] // (2 * m) == idx[None, :] // (2 * m)
            masks.append((same & (sel[:, :1] > 0.5) & (sel[:, :1].T < 0.5)).astype(np.float32))
        masks.append(np.eye(chunk, dtype=np.float32))
        pairs.append(np.stack(masks))
    return np.stack(amats), np.stack(rowsels), np.stack(pairs)


def _dot_exact_rhs(a_bf16, x):
    x1 = x.astype(BF16)
    x2 = (x - x1.astype(F32)).astype(BF16)
    n = x.shape[1]
    y = jnp.dot(a_bf16, jnp.concatenate([x1, x2], axis=1), preferred_element_type=F32)
    return y[:, :n] + y[:, n:]


_NT = (((1,), (1,)), ((), ()))
_TN = (((0,), (0,)), ((), ()))


def _hgrn_kernel(*refs, seq_len, chunk, has_s0):
    if has_s0:
        (q_ref, fa_ref, fb_ref, i_ref, g_ref, lb_ref, nw_ref, amat_ref, rowsel_ref, pair_ref, s0_ref,
         o_ref, sfin_ref, of_scr, ob_scr, st_scr) = refs
    else:
        (q_ref, fa_ref, fb_ref, i_ref, g_ref, lb_ref, nw_ref, amat_ref, rowsel_ref, pair_ref,
         o_ref, sfin_ref, of_scr, ob_scr, st_scr) = refs
    n_chunks = seq_len // chunk
    n_lev = len(_scan_levels(chunk))

    for d in range(2):
        if has_s0:
            st_scr[d] = s0_ref[d].T
        else:
            st_scr[d] = jnp.zeros((DV_A, DK_A), F32)

    f_refs = (fa_ref, fb_ref)
    o_scrs = (of_scr, ob_scr)

    def gates(d, rows):
        z = f_refs[d][rows, :]
        q = _silu(q_ref[rows, :])
        lbd = lb_ref[d:d + 1, :]
        e = jnp.exp(-jnp.abs(z))
        den = 1.0 + e
        a1 = jnp.log(lbd)
        a2 = jnp.log1p(-lbd) + (jnp.minimum(z, 0.0) - jnp.log(den))
        logf = jnp.maximum(a1, a2) + jnp.log(1.0 + jnp.exp(-jnp.abs(a1 - a2)))
        k = (1.0 - lbd) * (jnp.where(z >= 0.0, e, 1.0) / den)
        return q, k, logf

    def body(jj, carry):
        chains = []
        for d in range(2):
            for u in range(SCAN_UNROLL):
                j = jj * SCAN_UNROLL + u
                if d == 1:
                    j = n_chunks - 1 - j
                chains.append((d, pl.ds(pl.multiple_of(j * chunk, chunk), chunk)))
        qkf = [gates(d, rows) for d, rows in chains]
        ws = [jnp.exp(_dot_exact_rhs(amat_ref[d], logf))
              for (d, _), (_, _, logf) in zip(chains, qkf)]
        vbs = [i_ref[rows, :].astype(BF16) for _, rows in chains]
        atts, dsts = [], []
        for (d, _), (q, k, _), w, vb in zip(chains, qkf, ws, vbs):
            att = lax.dot_general(q.astype(BF16), k.astype(BF16), _NT,
                                  preferred_element_type=F32) * pair_ref[d, n_lev]
            for lv in range(n_lev):
                x = (jnp.where(rowsel_ref[d, lv] > 0.5, q, k)
                     * w[lv * chunk:(lv + 1) * chunk]).astype(BF16)
                att = att + lax.dot_general(x, x, _NT, preferred_element_type=F32) * pair_ref[d, lv]
            atts.append(att.astype(BF16))
            wk = w[(n_lev + 1) * chunk:(n_lev + 2) * chunk]
            dsts.append(lax.dot_general(vb, (k * wk).astype(BF16), _TN, preferred_element_type=F32))
        for d in range(2):
            st = st_scr[d]
            for u in range(SCAN_UNROLL):
                c = d * SCAN_UNROLL + u
                q, w = qkf[c][0], ws[c]
                wq = w[n_lev * chunk:(n_lev + 1) * chunk]
                wall = w[(n_lev + 2) * chunk:(n_lev + 2) * chunk + 1]
                o = lax.dot_general((q * wq).astype(BF16), st.astype(BF16), _NT,
                                    preferred_element_type=F32)
                o = o + jnp.dot(atts[c], vbs[c], preferred_element_type=F32)
                o_scrs[d][chains[c][1], :] = o
                st = st * wall + dsts[c]
            st_scr[d] = st
        return carry

    lax.fori_loop(0, n_chunks // SCAN_UNROLL, body, 0)

    o = of_scr[...] + ob_scr[...]
    y = _rms(o) * nw_ref[...]
    o_ref[...] = (y * _silu(g_ref[...])).astype(BF16)
    for d in range(2):
        sfin_ref[d] = st_scr[d].T


def _hgrn_mixer(proj, lb, norm_w, s0, layer, *, row_block0, batch, seq_len):
    chunk = SCAN_CHUNK
    amat, rowsel, pair = _scan_tables(chunk)
    amat = jnp.asarray(amat).astype(BF16)
    rowsel = jnp.asarray(rowsel)
    pair = jnp.asarray(pair)
    has_s0 = s0 is not None

    def col_spec(col):
        return pl.BlockSpec((seq_len, LANES), lambda b, h: (row_block0 + b, col // LANES + h))

    def const_spec(a):
        nd = a.ndim
        return pl.BlockSpec(a.shape, lambda b, h: (0,) * nd)

    in_specs = [col_spec(COL_Q), col_spec(COL_FA), col_spec(COL_FB), col_spec(COL_I), col_spec(COL_G),
                pl.BlockSpec((2, LANES), lambda b, h: (0, h)),
                pl.BlockSpec((1, LANES), lambda b, h: (0, 0)),
                const_spec(amat), const_spec(rowsel), const_spec(pair)]
    args = [proj, proj, proj, proj, proj, lb, norm_w.reshape(1, DV_A), amat, rowsel, pair]
    if has_s0:
        in_specs.append(pl.BlockSpec((None, None, 2, None, DK_A, DV_A),
                                     lambda b, h: (b, layer, 0, h, 0, 0)))
        args.append(s0)
    kern = functools.partial(_hgrn_kernel, seq_len=seq_len, chunk=chunk, has_s0=has_s0)
    return pl.pallas_call(
        kern,
        grid=(batch, H_A),
        in_specs=in_specs,
        out_specs=[pl.BlockSpec((seq_len, LANES), lambda b, h: (b, h)),
                   pl.BlockSpec((None, 2, None, DK_A, DV_A), lambda b, h: (b, 0, h, 0, 0))],
        out_shape=[jax.ShapeDtypeStruct((batch * seq_len, D_A), BF16),
                   jax.ShapeDtypeStruct((batch, 2, H_A, DK_A, DV_A), F32)],
        scratch_shapes=[pltpu.VMEM((seq_len, DV_A), F32), pltpu.VMEM((seq_len, DV_A), F32),
                        pltpu.VMEM((2, DV_A, DK_A), F32)],
        compiler_params=_params(2),
        name="hgrn_mixer",
    )(*args)


@functools.lru_cache(maxsize=None)
def _hyena_tables(seq_len):
    n = seq_len
    t = np.arange(n, dtype=np.float64)
    t01 = t / max(n - 1, 1)
    bands = np.linspace(1e-4, HY_BANDS - 1, HY_BANDS)
    ang = (2.0 * math.pi / n) * t[:, None] * bands[None, :]
    feats = np.concatenate([t01[:, None], np.cos(ang), -np.sin(ang)], axis=-1)
    feats = np.pad(feats, ((0, 0), (0, LANES - HY_EMB))).astype(np.float32)
    ft = np.outer(np.arange(n), np.arange(n)) % (2 * n)
    fc = np.cos(math.pi * ft / n)
    fs = -np.sin(math.pi * ft / n)
    fs[0, :] = 1.0 - 2.0 * (np.arange(n) % 2)
    return feats, fc.astype(np.float32), fs.astype(np.float32), fs.T.astype(np.float32).copy()


def _dft3(f_hi_ref, f_lo_ref, x):
    xh = x.astype(BF16)
    xl = (x - xh.astype(F32)).astype(BF16)
    return (jnp.dot(f_hi_ref[...], xh, preferred_element_type=F32)
            + jnp.dot(f_hi_ref[...], xl, preferred_element_type=F32)
            + jnp.dot(f_lo_ref[...], xh, preferred_element_type=F32))


def _hyena_filter_kernel(feats_ref, w1_ref, b1_ref, w2_ref, b2_ref, w3_ref0, w3_ref1, w3_ref2, w3_ref3,
                         decay_ref, fc_hi, fc_lo, fs_hi, fs_lo, p_ref, h_scr, *, seq_len):
    n = seq_len
    hp = lax.Precision.HIGHEST

    @pl.when(pl.program_id(0) == 0)
    def _():
        h1 = jnp.sin(HY_SIN_W * (jnp.dot(feats_ref[...], w1_ref[...], precision=hp,
                                         preferred_element_type=F32) + b1_ref[...]))
        h_scr[...] = jnp.sin(HY_SIN_W * (jnp.dot(h1, w2_ref[...], precision=hp,
                                                 preferred_element_type=F32) + b2_ref[...]))

    h = h_scr[...]
    w3 = ((w3_ref0, w3_ref1), (w3_ref2, w3_ref3))
    cb = p_ref.shape[-1]
    row = lax.broadcasted_iota(jnp.int32, (n, cb), 0)
    t01 = row.astype(F32) / float(max(n - 1, 1))
    scale = 1.0 / (2.0 * n)
    wts = jnp.where(row == 0, scale, 2.0 * scale)
    sgn = (1 - 2 * (row & 1)).astype(F32)
    for o in range(HY_ORDER):
        win = jnp.exp(-t01 * jnp.abs(decay_ref[o:o + 1, :]))
        hf = jnp.dot(h, w3[o][0][...], precision=hp, preferred_element_type=F32) * win
        hb = jnp.dot(h, w3[o][1][...], precision=hp, preferred_element_type=F32) * win
        mag = jnp.where(row == 0, jnp.abs(hf + hb), jnp.abs(hf) + jnp.abs(hb))
        inv = 1.0 / (jnp.sum(mag, axis=0, keepdims=True) + EPS)
        even = (hf + hb) * inv
        odd = (hf - hb) * inv
        kre = _dft3(fc_hi, fc_lo, even)
        kim = _dft3(fs_hi, fs_lo, odd)
        knyq = jnp.sum(even * sgn, axis=0, keepdims=True)
        p_ref[3 * o + 0] = kre * wts
        p_ref[3 * o + 1] = jnp.where(row == 0, 0.0, kim * wts)
        p_ref[3 * o + 2] = jnp.where(row == 0, knyq * scale, kre * wts)


def _split_bf16(m):
    hi = m.astype(BF16)
    return hi, (m - hi.astype(F32)).astype(BF16)


def _hyena_spectrum(seq_len, w1, b1, w2, b2, w3, decay):
    feats, fc, fs, _ = _hyena_tables(seq_len)
    feats = jnp.asarray(feats)
    mats = _split_bf16(jnp.asarray(fc)) + _split_bf16(jnp.asarray(fs))
    w1p = jnp.pad(w1, ((0, LANES - HY_EMB), (0, 0)))
    cb = HY_CB
    nb = D_B // cb

    def w3_spec(k):
        return pl.BlockSpec((HY_HID, cb), lambda c: (0, k * nb + c))

    full2 = lambda a: pl.BlockSpec(a.shape, lambda c: (0, 0))
    b1r, b2r = b1.reshape(1, HY_HID), b2.reshape(1, HY_HID)
    return pl.pallas_call(
        functools.partial(_hyena_filter_kernel, seq_len=seq_len),
        grid=(nb,),
        in_specs=[full2(feats), full2(w1p), full2(b1r), full2(w2), full2(b2r),
                  w3_spec(0), w3_spec(1), w3_spec(2), w3_spec(3),
                  pl.BlockSpec((HY_ORDER, cb), lambda c: (0, c))] + [full2(m) for m in mats],
        out_specs=pl.BlockSpec((3 * HY_ORDER, seq_len, cb), lambda c: (0, 0, c)),
        out_shape=jax.ShapeDtypeStruct((3 * HY_ORDER, seq_len, D_B), F32),
        scratch_shapes=[pltpu.VMEM((seq_len, HY_HID), F32)],
        compiler_params=_params(1),
        name="hyena_filter",
    )(feats, w1p, b1r, w2, b2r, w3, w3, w3, w3, decay, *mats)


def _hyena_conv_kernel(v_ref, x1_ref, x2_ref, cwv_ref, cw1_ref, cw2_ref, cbv_ref, cb1_ref, cb2_ref,
                       p_ref, bias_ref, fc_ref, fs_ref, fst_ref, o_ref, *, seq_len):
    n = seq_len
    rows, cb = o_ref.shape
    nseq = rows // n
    pos = lax.broadcasted_iota(jnp.int32, (rows, cb), 0) & (n - 1)

    def dwconv(x_ref, w_ref, b_ref):
        x = x_ref[...]
        xm = jnp.where(pos == 0, 0.0, pltpu.roll(x, 1, axis=0))
        xp = jnp.where(pos == n - 1, 0.0, pltpu.roll(x, rows - 1, axis=0))
        y = xm * w_ref[0:1, :] + x * w_ref[1:2, :] + xp * w_ref[2:3, :] + b_ref[...]
        return jnp.concatenate([y[s * n:(s + 1) * n] for s in range(nseq)], axis=1)

    def tiled(a):
        return jnp.concatenate([a] * nseq, axis=1)

    def fftconv(z, o):
        zb16 = z.astype(BF16)
        zt = jnp.dot(fc_ref[...], zb16, preferred_element_type=F32)
        zb = jnp.dot(fs_ref[...], zb16, preferred_element_type=F32)
        p1, p3, p4 = tiled(p_ref[3 * o]), tiled(p_ref[3 * o + 1]), tiled(p_ref[3 * o + 2])
        yt = (zt * p1 - zb * p3).astype(BF16)
        yb = (zt * p3 + zb * p4).astype(BF16)
        y = (jnp.dot(fc_ref[...], yt, preferred_element_type=F32)
             + jnp.dot(fst_ref[...], yb, preferred_element_type=F32))
        return y + z * tiled(bias_ref[o:o + 1, :])

    v = dwconv(v_ref, cwv_ref, cbv_ref)
    x1 = dwconv(x1_ref, cw1_ref, cb1_ref)
    z2 = x1 * fftconv(v, 0)
    x2 = dwconv(x2_ref, cw2_ref, cb2_ref)
    out = x2 * fftconv(z2, 1)
    for s in range(nseq):
        o_ref[s * n:(s + 1) * n, :] = out[:, s * cb:(s + 1) * cb].astype(BF16)


def _hyena_mixer(proj, conv_w, conv_b, spectrum, bias, *, row0, n_rows, seq_len):
    assert seq_len & (seq_len - 1) == 0 and HY_ROWS % seq_len == 0 and row0 % HY_ROWS == 0
    _, fc, fs, fst = _hyena_tables(seq_len)
    mats = [jnp.asarray(m).astype(BF16) for m in (fc, fs, fst)]
    cb = HY_CB
    nb = D_B // cb
    rb0 = row0 // HY_ROWS

    def col_spec(k):
        return pl.BlockSpec((HY_ROWS, cb), lambda c, g: (rb0 + g, (COL_HY + k * D_B) // cb + c))

    def cw_spec(k):
        return pl.BlockSpec((3, cb), lambda c, g: (0, k * nb + c))

    def cb_spec(k):
        return pl.BlockSpec((1, cb), lambda c, g: (0, k * nb + c))

    mat_spec = pl.BlockSpec((seq_len, seq_len), lambda c, g: (0, 0))
    conv_b2 = conv_b.reshape(1, 3 * D_B)
    return pl.pallas_call(
        functools.partial(_hyena_conv_kernel, seq_len=seq_len),
        grid=(nb, n_rows // HY_ROWS),
        in_specs=[col_spec(0), col_spec(1), col_spec(2), cw_spec(0), cw_spec(1), cw_spec(2),
                  cb_spec(0), cb_spec(1), cb_spec(2),
                  pl.BlockSpec((3 * HY_ORDER, seq_len, cb), lambda c, g: (0, 0, c)),
                  pl.BlockSpec((HY_ORDER, cb), lambda c, g: (0, c))] + [mat_spec] * 3,
        out_specs=pl.BlockSpec((HY_ROWS, cb), lambda c, g: (g, c)),
        out_shape=jax.ShapeDtypeStruct((n_rows, D_B), BF16),
        compiler_params=_params(2),
        name="hyena_conv",
    )(proj, proj, proj, conv_w, conv_w, conv_w, conv_b2, conv_b2, conv_b2, spectrum, bias, *mats)


def _merge_kernel(oac_ref, oal_ref, obc_ref, obl_ref, ga_ref, gb_ref, wa_ref, wb_ref, o_ref,
                  wabf, wbbf):
    i = pl.program_id(1)

    @pl.when(i == 0)
    def _():
        wabf[...] = wa_ref[...].astype(BF16)
        wbbf[...] = wb_ref[...].astype(BF16)

    is_ctx = i < N_CTX_BLOCKS
    oa = jnp.where(is_ctx, oac_ref[...], oal_ref[...])
    ob = jnp.where(is_ctx, obc_ref[...], obl_ref[...])
    ya = jnp.dot(oa, wabf[...], preferred_element_type=F32)
    yb = jnp.dot(ob, wbbf[...], preferred_element_type=F32)
    o_ref[...] = (jax.nn.sigmoid(ga_ref[...]) * ya + jax.nn.sigmoid(gb_ref[...]) * yb).astype(BF16)


def _merge(oa_ctx, oa_lat, ob_ctx, ob_lat, proj, w_a, w_b, layer):
    tm, tn = TOK_BLOCK, 512
    nbn = D_MODEL // tn
    ctx_blk = lambda j, i: (jnp.minimum(i, N_CTX_BLOCKS - 1), 0)
    lat_blk = lambda j, i: (jnp.maximum(i - N_CTX_BLOCKS, 0), 0)
    return pl.pallas_call(
        _merge_kernel,
        grid=(nbn, N_TOK // tm),
        in_specs=[pl.BlockSpec((tm, D_A), ctx_blk), pl.BlockSpec((tm, D_A), lat_blk),
                  pl.BlockSpec((tm, D_B), ctx_blk), pl.BlockSpec((tm, D_B), lat_blk),
                  pl.BlockSpec((tm, tn), lambda j, i: (i, COL_GATE_A // tn + j)),
                  pl.BlockSpec((tm, tn), lambda j, i: (i, COL_GATE_B // tn + j)),
                  pl.BlockSpec((None, D_A, tn), lambda j, i: (layer, 0, j)),
                  pl.BlockSpec((None, D_B, tn), lambda j, i: (layer, 0, j))],
        out_specs=pl.BlockSpec((tm, tn), lambda j, i: (i, j)),
        out_shape=jax.ShapeDtypeStruct((N_TOK, D_MODEL), BF16),
        scratch_shapes=[pltpu.VMEM((D_A, tn), BF16), pltpu.VMEM((D_B, tn), BF16)],
        compiler_params=_params(2),
        name="merge",
    )(oa_ctx, oa_lat, ob_ctx, ob_lat, proj, proj, w_a, w_b)


def _resid_kernel(*refs, has_next, n_act):
    a_refs, refs = refs[:n_act], refs[n_act:]
    if has_next:
        (w_ref, x_ref, gpost_ref, gate_ref, gnext_ref, sh_ref, sc_ref, xo_ref, h_ref) = refs
    else:
        w_ref, x_ref, gpost_ref, gate_ref, xo_ref = refs
    k = pl.program_id(1)

    @pl.when(k == 0)
    def _():
        xo_ref[...] = jnp.zeros_like(xo_ref)

    part = w_ref.shape[0] // n_act
    y = None
    for p, a_ref in enumerate(a_refs):
        yp = jnp.dot(a_ref[...], w_ref[p * part:(p + 1) * part, :].astype(BF16),
                     preferred_element_type=F32)
        y = yp if y is None else y + yp
    xo_ref[...] += y

    @pl.when(k == pl.num_programs(1) - 1)
    def _():
        def rows_step(r, carry):
            rows = pl.ds(pl.multiple_of(r * EPILOGUE_ROWS, EPILOGUE_ROWS), EPILOGUE_ROWS)
            xn = x_ref[rows, :] + gate_ref[...] * (_rms(xo_ref[rows, :]) * gpost_ref[...])
            xo_ref[rows, :] = xn
            if has_next:
                h = (_rms(xn) * gnext_ref[...]) * (1.0 + sc_ref[...]) + sh_ref[...]
                h_ref[rows, :] = h.astype(BF16)
            return carry

        lax.fori_loop(0, xo_ref.shape[0] // EPILOGUE_ROWS, rows_step, 0)


def _proj_residual(a, w, x, g_post, mod4, layer, gate_chunk, nxt, blocks=(0, N_TOK_BLOCKS)):
    tm, tk = TOK_BLOCK, 512
    acts = a if isinstance(a, tuple) else (a,)
    n_act = len(acts)
    kdim = sum(p.shape[1] for p in acts)
    has_next = nxt is not None
    b0, nblk = blocks
    in_row_spec = pl.BlockSpec((tm, D_MODEL), lambda i, k: (b0 + i, 0))
    row_spec = pl.BlockSpec((tm, D_MODEL), lambda i, k: (i, 0))
    in_specs = [pl.BlockSpec((tm, tk // n_act), lambda i, k: (b0 + i, k)) for _ in acts]
    in_specs += [pl.BlockSpec((None, tk, D_MODEL), lambda i, k: (layer, k, 0)),
                 in_row_spec,
                 _gain_spec(layer), _mod_spec(layer, gate_chunk, tm, b0)]
    args = [*acts, w, x, g_post, mod4]
    out_specs = [row_spec]
    out_shape = [jax.ShapeDtypeStruct((nblk * tm, D_MODEL), F32)]
    if has_next:
        g_next, nl, sh_chunk, sc_chunk = nxt
        in_specs += [_gain_spec(nl), _mod_spec(nl, sh_chunk, tm, b0), _mod_spec(nl, sc_chunk, tm, b0)]
        args += [g_next, mod4, mod4]
        out_specs.append(row_spec)
        out_shape.append(jax.ShapeDtypeStruct((nblk * tm, D_MODEL), BF16))
    res = pl.pallas_call(
        functools.partial(_resid_kernel, has_next=has_next, n_act=n_act),
        grid=(nblk, kdim // tk),
        in_specs=in_specs,
        out_specs=out_specs,
        out_shape=out_shape,
        compiler_params=_params(2),
        name="proj_residual",
    )(*args)
    return res if has_next else (res[0], None)


def _ffn_up_kernel(h_ref, wa_ref, wv_ref, cwa_ref, cwv_ref, cba_ref, cbv_ref, o_ref,
                   wabf, wvbf, ua_scr, uv_scr, sa_scr, sv_scr):
    j, i = pl.program_id(0), pl.program_id(1)
    tm, tn = o_ref.shape
    grp, rt, halo = FFN_COL_GROUP, FFN_ROW_TILE, SUBLANES

    @pl.when(i == 0)
    def _():
        wabf[...] = wa_ref[...].astype(BF16)
        wvbf[...] = wv_ref[...].astype(BF16)

    @pl.when((i == 0) & (j == 0))
    def _():
        for u_scr in (ua_scr, uv_scr):
            u_scr[0:halo, :] = jnp.zeros((halo, grp), F32)
            u_scr[halo + tm:, :] = jnp.zeros((halo, grp), F32)
        for s_scr in (sa_scr, sv_scr):
            s_scr[:, 0:GRID_W, :] = jnp.zeros((3, GRID_W, grp), F32)
            s_scr[:, GRID_W + tm:, :] = jnp.zeros((3, GRID_W, grp), F32)

    def tile_taps(u_scr, r0, row_len):
        piece = u_scr[r0:r0 + rt + 2 * halo, :]
        t = r0 + lax.broadcasted_iota(jnp.int32, (rt, LANES), 0)
        pos = t & (row_len - 1)
        keep_prev = jnp.concatenate([(pos != 0).astype(F32)] * (grp // LANES), axis=1)
        keep_next = jnp.concatenate([(pos != row_len - 1).astype(F32)] * (grp // LANES), axis=1)
        n = rt + 2 * halo
        um = pltpu.roll(piece, 1, axis=0)[halo:halo + rt] * keep_prev
        up = pltpu.roll(piece, n - 1, axis=0)[halo:halo + rt] * keep_next
        return um, piece[halo:halo + rt], up

    def kernel_row(taps, cw_ref, dr, cols):
        um, u, up = taps
        r = 3 * (dr + 1)
        return um * cw_ref[r:r + 1, cols] + u * cw_ref[r + 1:r + 2, cols] + up * cw_ref[r + 2:r + 3, cols]

    def project(dst, cols, h):
        dst[0][halo:halo + tm, :] = jnp.dot(h, wabf[:, cols], preferred_element_type=F32)
        dst[1][halo:halo + tm, :] = jnp.dot(h, wvbf[:, cols], preferred_element_type=F32)

    def conv_glu(src, params, cols, latent):
        ua_scr, uv_scr = src
        wa, wv, ba, bv = params
        if not latent:
            for r0 in range(0, tm, rt):
                a = kernel_row(tile_taps(ua_scr, r0, SEQ), wa, 0, cols) + ba[:, cols]
                vv = kernel_row(tile_taps(uv_scr, r0, SEQ), wv, 0, cols) + bv[:, cols]
                o_ref[r0:r0 + rt, cols] = (_silu(a) * vv).astype(BF16)
        else:
            for r0 in range(0, tm, rt):
                ta, tv = tile_taps(ua_scr, r0, GRID_W), tile_taps(uv_scr, r0, GRID_W)
                for dr in (-1, 0, 1):
                    sa_scr[dr + 1, GRID_W + r0:GRID_W + r0 + rt, :] = kernel_row(ta, wa, dr, cols)
                    sv_scr[dr + 1, GRID_W + r0:GRID_W + r0 + rt, :] = kernel_row(tv, wv, dr, cols)
            for r0 in range(0, tm, rt):
                def gather(s_scr, b_ref):
                    return (b_ref[:, cols] + s_scr[0, r0:r0 + rt, :]
                            + s_scr[1, GRID_W + r0:GRID_W + r0 + rt, :]
                            + s_scr[2, 2 * GRID_W + r0:2 * GRID_W + r0 + rt, :])
                a, vv = gather(sa_scr, ba), gather(sv_scr, bv)
                o_ref[r0:r0 + rt, cols] = (_silu(a) * vv).astype(BF16)

    def block(latent):
        h = h_ref[...]
        for c in range(tn // grp):
            cols = slice(c * grp, (c + 1) * grp)
            project((ua_scr, uv_scr), cols, h)
            conv_glu((ua_scr, uv_scr), (cwa_ref, cwv_ref, cba_ref, cbv_ref), cols, latent)

    @pl.when(i < N_CTX_BLOCKS)
    def _():
        block(False)

    @pl.when(i >= N_CTX_BLOCKS)
    def _():
        block(True)


def _ffn_up(h, w_up, conv_w, conv_b, layer):
    tm, tn = TOK_BLOCK, 512
    nbn = D_FF // tn
    conv_w9 = conv_w.reshape(DEPTH, 9, 2 * D_FF)
    conv_b2 = conv_b.reshape(DEPTH, 1, 2 * D_FF)
    u_scratch = pltpu.VMEM((tm + 2 * SUBLANES, FFN_COL_GROUP), F32)
    s_scratch = pltpu.VMEM((3, tm + 2 * GRID_W, FFN_COL_GROUP), F32)
    return pl.pallas_call(
        _ffn_up_kernel,
        grid=(nbn, N_TOK // tm),
        in_specs=[pl.BlockSpec((tm, D_MODEL), lambda j, i: (i, 0)),
                  pl.BlockSpec((None, D_MODEL, tn), lambda j, i: (layer, 0, j)),
                  pl.BlockSpec((None, D_MODEL, tn), lambda j, i: (layer, 0, nbn + j)),
                  pl.BlockSpec((None, 9, tn), lambda j, i: (layer, 0, j)),
                  pl.BlockSpec((None, 9, tn), lambda j, i: (layer, 0, nbn + j)),
                  pl.BlockSpec((None, 1, tn), lambda j, i: (layer, 0, j)),
                  pl.BlockSpec((None, 1, tn), lambda j, i: (layer, 0, nbn + j))],
        out_specs=pl.BlockSpec((tm, tn), lambda j, i: (i, j)),
        out_shape=jax.ShapeDtypeStruct((N_TOK, D_FF), BF16),
        scratch_shapes=[pltpu.VMEM((D_MODEL, tn), BF16), pltpu.VMEM((D_MODEL, tn), BF16),
                        u_scratch, u_scratch, s_scratch, s_scratch],
        compiler_params=_params(2),
        name="ffn_up",
    )(h, w_up, w_up, conv_w9, conv_w9, conv_b2, conv_b2)


def kernel(x_prompt, x_sample, state_hgrn, c, c_ctx, w_mod, b_mod, g_pre_mix, g_post_mix, g_pre_ffn,
           g_post_ffn, w_in, hgrn_lower_bounds, hgrn_norm, hy_conv_w, hy_conv_b, hy_w1, hy_b1, hy_w2,
           hy_b2, hy_w3, hy_decay, hy_bias, w_branch_a, w_branch_b, w_out, ffn_w_up, ffn_conv_w,
           ffn_conv_b, ffn_w_down):
    p_lb = jax.nn.softmax(hgrn_lower_bounds.astype(F32), axis=0)
    cs = jnp.cumsum(p_lb, axis=0)
    lbs = cs - cs[:1]

    x = jnp.concatenate([x_prompt.reshape(N_CTX, D_MODEL), x_sample.reshape(N_LAT, D_MODEL)], axis=0)
    cond = jnp.concatenate([jnp.broadcast_to(c_ctx[None, :], (N_CTX_BLOCKS, D_MODEL)), c], axis=0)
    mod4 = _modulation(cond, w_mod, b_mod).reshape(DEPTH, N_TOK_BLOCKS, 1, N_MOD)

    g_pre_mix, g_post_mix, g_pre_ffn, g_post_ffn = (
        g.reshape(DEPTH, 1, D_MODEL) for g in (g_pre_mix, g_post_mix, g_pre_ffn, g_post_ffn))

    h = _prenorm(x, g_pre_mix, mod4, 0)
    new_states = []
    for l in range(DEPTH):
        proj = _in_proj(h, w_in, l)
        oa_ctx, s_ctx = _hgrn_mixer(proj, lbs[l], hgrn_norm[l], None, l,
                                    row_block0=0, batch=BATCH, seq_len=SEQ)
        oa_lat, _ = _hgrn_mixer(proj, lbs[l], hgrn_norm[l], state_hgrn, l,
                                row_block0=N_CTX // DEC_SEQ, batch=DEC_BATCH, seq_len=DEC_SEQ)
        new_states.append(s_ctx)
        ob = []
        for row0, n_rows, seq_len in ((0, N_CTX, SEQ), (N_CTX, N_LAT, DEC_SEQ)):
            spectrum = _hyena_spectrum(seq_len, hy_w1[l], hy_b1[l], hy_w2[l], hy_b2[l], hy_w3[l],
                                       hy_decay[l])
            ob.append(_hyena_mixer(proj, hy_conv_w[l], hy_conv_b[l], spectrum, hy_bias[l],
                                   row0=row0, n_rows=n_rows, seq_len=seq_len))
        merged = _merge(oa_ctx, oa_lat, ob[0], ob[1], proj, w_branch_a, w_branch_b, l)
        x, h2 = _proj_residual(merged, w_out, x, g_post_mix, mod4, l, 2, (g_pre_ffn, l, 3, 4))
        g = _ffn_up(h2, ffn_w_up, ffn_conv_w, ffn_conv_b, l)
        if l + 1 < DEPTH:
            x, h = _proj_residual(g, ffn_w_down, x, g_post_ffn, mod4, l, 5, (g_pre_mix, l + 1, 0, 1))
        else:
            y_ctx, _ = _proj_residual(g, ffn_w_down, x, g_post_ffn, mod4, l, 5, None,
                                      blocks=(0, N_CTX_BLOCKS))
            y_lat, _ = _proj_residual(g, ffn_w_down, x, g_post_ffn, mod4, l, 5, None,
                                      blocks=(N_CTX_BLOCKS, N_TOK_BLOCKS - N_CTX_BLOCKS))

    y_prompt = y_ctx.reshape(BATCH, SEQ, D_MODEL)
    y_sample = y_lat.reshape(DEC_BATCH, DEC_SEQ, D_MODEL)
    new_state = jnp.stack(new_states, axis=1).astype(x_prompt.dtype)
    return (y_prompt, y_sample, new_state)
```

```python
import functools
import math

import numpy as np
import jax
import jax.numpy as jnp
from jax import lax
from jax.experimental import pallas as pl
from jax.experimental.pallas import tpu as pltpu

F32 = jnp.float32
BF16 = jnp.bfloat16

D_MODEL = 2048
BATCH = 16
SEQ = 256
DEPTH = 2
DEC_BATCH = 4
DEC_SEQ = 1024
GRID_W = 64
D_A = D_MODEL // 2
D_B = D_MODEL // 2
HGRN_EXPAND = 128
H_A = D_A // HGRN_EXPAND
DK_A = HGRN_EXPAND
DV_A = D_A // H_A
HY_ORDER = 2
HY_EMB = 33
HY_BANDS = (HY_EMB - 1) // 2
HY_HID = 64
HY_SIN_W = 1.0
D_FF = 5632
N_MOD = 6 * D_MODEL
N_PROJ = 5 * D_A + 3 * D_B + 2 * D_MODEL
EPS = 1e-6

N_CTX = BATCH * SEQ
N_LAT = DEC_BATCH * DEC_SEQ
N_TOK = N_CTX + N_LAT
TOK_BLOCK = 1024
N_TOK_BLOCKS = N_TOK // TOK_BLOCK
N_CTX_BLOCKS = N_CTX // TOK_BLOCK

COL_Q, COL_FA, COL_FB, COL_I, COL_G = 0, D_A, 2 * D_A, 3 * D_A, 4 * D_A
COL_HY = 5 * D_A
COL_GATE_A = 5 * D_A + 3 * D_B
COL_GATE_B = COL_GATE_A + D_MODEL

LANES = 128
SUBLANES = 8
FFN_ROW_TILE = 128
SCAN_CHUNK = 32
SCAN_UNROLL = 8
HY_CB = 256
HY_ROWS = 2048
FFN_COL_GROUP = 256
EPILOGUE_ROWS = 128
VMEM_LIMIT = 60 * 1024 * 1024


def _params(n_axes):
    return pltpu.CompilerParams(dimension_semantics=("arbitrary",) * n_axes,
                                vmem_limit_bytes=VMEM_LIMIT)


def _silu(x):
    return x * jax.nn.sigmoid(x)


def _rms(x):
    return x * lax.rsqrt(jnp.mean(x * x, axis=-1, keepdims=True) + EPS)


def _mod_spec(layer, chunk, tm, block0=0):
    return pl.BlockSpec((None, None, 1, D_MODEL),
                        lambda i, *_: (layer, ((block0 + i) * tm) // TOK_BLOCK, 0, chunk))


def _gain_spec(layer):
    return pl.BlockSpec((None, 1, D_MODEL), lambda *_: (layer, 0, 0))


def _mod_kernel(cond_ref, w_ref, b_ref, o_ref):
    s = _silu(cond_ref[...]).astype(BF16)
    o_ref[...] = jnp.dot(s, w_ref[...].astype(BF16), preferred_element_type=F32) + b_ref[...]


def _modulation(cond, w_mod, b_mod):
    tn = 1024
    return pl.pallas_call(
        _mod_kernel,
        grid=(DEPTH, N_MOD // tn),
        in_specs=[pl.BlockSpec((N_TOK_BLOCKS, D_MODEL), lambda l, j: (0, 0)),
                  pl.BlockSpec((None, D_MODEL, tn), lambda l, j: (l, 0, j)),
                  pl.BlockSpec((None, 1, tn), lambda l, j: (l, 0, j))],
        out_specs=pl.BlockSpec((None, N_TOK_BLOCKS, tn), lambda l, j: (l, 0, j)),
        out_shape=jax.ShapeDtypeStruct((DEPTH, N_TOK_BLOCKS, N_MOD), F32),
        compiler_params=_params(2),
        name="modulation",
    )(cond, w_mod, b_mod.reshape(DEPTH, 1, N_MOD))


def _prenorm_kernel(xc_ref, xl_ref, g_ref, sh_ref, sc_ref, h_ref, x_ref, *, n_ctx_blocks):
    x = jnp.where(pl.program_id(0) < n_ctx_blocks, xc_ref[...], xl_ref[...])
    x_ref[...] = x
    h = (_rms(x) * g_ref[...]) * (1.0 + sc_ref[...]) + sh_ref[...]
    h_ref[...] = h.astype(BF16)


def _prenorm(x_ctx, x_lat, g, mod4, layer):
    tm = 512
    nc = N_CTX // tm
    row_spec = pl.BlockSpec((tm, D_MODEL), lambda i: (i, 0))
    return pl.pallas_call(
        functools.partial(_prenorm_kernel, n_ctx_blocks=nc),
        grid=(N_TOK // tm,),
        in_specs=[pl.BlockSpec((tm, D_MODEL), lambda i: (jnp.minimum(i, nc - 1), 0)),
                  pl.BlockSpec((tm, D_MODEL), lambda i: (jnp.maximum(i - nc, 0), 0)),
                  _gain_spec(layer), _mod_spec(layer, 0, tm), _mod_spec(layer, 1, tm)],
        out_specs=[row_spec, row_spec],
        out_shape=[jax.ShapeDtypeStruct((N_TOK, D_MODEL), BF16),
                   jax.ShapeDtypeStruct((N_TOK, D_MODEL), F32)],
        compiler_params=_params(1),
        name="prenorm",
    )(x_ctx, x_lat, g, mod4, mod4)


def _inproj_kernel(h_ref, w_ref, o_ref, wbf_ref):
    @pl.when(pl.program_id(1) == 0)
    def _():
        wbf_ref[...] = w_ref[...].astype(BF16)

    o_ref[...] = jnp.dot(h_ref[...], wbf_ref[...], preferred_element_type=F32)


def _in_proj(h, w_in, layer):
    tm, tn = 1024, 1024
    return pl.pallas_call(
        _inproj_kernel,
        grid=(N_PROJ // tn, N_TOK // tm),
        in_specs=[pl.BlockSpec((tm, D_MODEL), lambda j, i: (i, 0)),
                  pl.BlockSpec((None, D_MODEL, tn), lambda j, i: (layer, 0, j))],
        out_specs=pl.BlockSpec((tm, tn), lambda j, i: (i, j)),
        out_shape=jax.ShapeDtypeStruct((N_TOK, N_PROJ), F32),
        scratch_shapes=[pltpu.VMEM((D_MODEL, tn), BF16)],
        compiler_params=_params(2),
        name="in_proj",
    )(h, w_in)


def _scan_levels(chunk):
    m, levels = chunk // 2, []
    while m >= 1:
        levels.append(m)
        m //= 2
    return tuple(levels)


@functools.lru_cache(maxsize=None)
def _scan_tables(chunk):
    levels = _scan_levels(chunk)
    amats, rowsels, pairs = [], [], []
    for rev in (False, True):
        blocks, sels = [], []
        for m in levels:
            a = np.zeros((chunk, chunk), np.float32)
            sel = np.zeros((chunk,), np.float32)
            for t in range(chunk):
                mid = 2 * m * (t // (2 * m)) + m - 1
                upper = t > mid
                if not rev:
                    if upper:
                        a[t, mid + 1:t + 1] = 1.0
                    else:
                        a[t, t + 1:mid + 1] = 1.0
                    sel[t] = 1.0 if upper else 0.0
                else:
                    if upper:
                        a[t, mid + 1:t] = 1.0
                    else:
                        a[t, t:mid + 1] = 1.0
                    sel[t] = 0.0 if upper else 1.0
            blocks.append(a)
            sels.append(np.repeat(sel[:, None], LANES, axis=1))
        idx = np.arange(chunk)
        if not rev:
            inter = (idx[None, :] <= idx[:, None]).astype(np.float32)
            carry = (idx[None, :] > idx[:, None]).astype(np.float32)
        else:
            inter = (idx[None, :] >= idx[:, None]).astype(np.float32)
            carry = (idx[None, :] < idx[:, None]).astype(np.float32)
        blocks += [inter, carry, np.ones((16, chunk), np.float32)]
        amats.append(np.concatenate(blocks, axis=0))
        rowsels.append(np.stack(sels))
        masks = []
        for m, sel in zip(levels, sels):
            same = idx[:, None] // (2 * m) == idx[None, :] // (2 * m)
            masks.append((same & (sel[:, :1] > 0.5) & (sel[:, :1].T < 0.5)).astype(np.float32))
        masks.append(np.eye(chunk, dtype=np.float32))
        pairs.append(np.stack(masks))
    return np.stack(amats), np.stack(rowsels), np.stack(pairs)


def _dot_exact_rhs(a_bf16, x):
    x1 = x.astype(BF16)
    x2 = (x - x1.astype(F32)).astype(BF16)
    n = x.shape[1]
    y = jnp.dot(a_bf16, jnp.concatenate([x1, x2], axis=1), preferred_element_type=F32)
    return y[:, :n] + y[:, n:]


_NT = (((1,), (1,)), ((), ()))
_TN = (((0,), (0,)), ((), ()))


def _hgrn_kernel(*refs, seq_len, chunk, has_s0):
    if has_s0:
        (q_ref, fa_ref, fb_ref, i_ref, g_ref, lb_ref, nw_ref, amat_ref, rowsel_ref, pair_ref, s0_ref,
         o_ref, sfin_ref, of_scr, ob_scr, st_scr) = refs
    else:
        (q_ref, fa_ref, fb_ref, i_ref, g_ref, lb_ref, nw_ref, amat_ref, rowsel_ref, pair_ref,
         o_ref, sfin_ref, of_scr, ob_scr, st_scr) = refs
    n_chunks = seq_len // chunk
    n_lev = len(_scan_levels(chunk))

    for d in range(2):
        if has_s0:
            st_scr[d] = s0_ref[d].T
        else:
            st_scr[d] = jnp.zeros((DV_A, DK_A), F32)

    f_refs = (fa_ref, fb_ref)
    o_scrs = (of_scr, ob_scr)

    def gates(d, rows):
        z = f_refs[d][rows, :]
        q = _silu(q_ref[rows, :])
        lbd = lb_ref[d:d + 1, :]
        e = jnp.exp(-jnp.abs(z))
        den = 1.0 + e
        a1 = jnp.log(lbd)
        a2 = jnp.log1p(-lbd) + (jnp.minimum(z, 0.0) - jnp.log(den))
        logf = jnp.maximum(a1, a2) + jnp.log(1.0 + jnp.exp(-jnp.abs(a1 - a2)))
        k = (1.0 - lbd) * (jnp.where(z >= 0.0, e, 1.0) / den)
        return q, k, logf

    def body(jj, carry):
        chains = []
        for d in range(2):
            for u in range(SCAN_UNROLL):
                j = jj * SCAN_UNROLL + u
                if d == 1:
                    j = n_chunks - 1 - j
                chains.append((d, pl.ds(pl.multiple_of(j * chunk, chunk), chunk)))
        qkf = [gates(d, rows) for d, rows in chains]
        ws = [jnp.exp(_dot_exact_rhs(amat_ref[d], logf))
              for (d, _), (_, _, logf) in zip(chains, qkf)]
        vbs = [i_ref[rows, :].astype(BF16) for _, rows in chains]
        atts, dsts = [], []
        for (d, _), (q, k, _), w, vb in zip(chains, qkf, ws, vbs):
            att = lax.dot_general(q.astype(BF16), k.astype(BF16), _NT,
                                  preferred_element_type=F32) * pair_ref[d, n_lev]
            for lv in range(n_lev):
                x = (jnp.where(rowsel_ref[d, lv] > 0.5, q, k)
                     * w[lv * chunk:(lv + 1) * chunk]).astype(BF16)
                att = att + lax.dot_general(x, x, _NT, preferred_element_type=F32) * pair_ref[d, lv]
            atts.append(att.astype(BF16))
            wk = w[(n_lev + 1) * chunk:(n_lev + 2) * chunk]
            dsts.append(lax.dot_general(vb, (k * wk).astype(BF16), _TN, preferred_element_type=F32))
        for d in range(2):
            st = st_scr[d]
            for u in range(SCAN_UNROLL):
                c = d * SCAN_UNROLL + u
                q, w = qkf[c][0], ws[c]
                wq = w[n_lev * chunk:(n_lev + 1) * chunk]
                wall = w[(n_lev + 2) * chunk:(n_lev + 2) * chunk + 1]
                o = lax.dot_general((q * wq).astype(BF16), st.astype(BF16), _NT,
                                    preferred_element_type=F32)
                o = o + jnp.dot(atts[c], vbs[c], preferred_element_type=F32)
                o_scrs[d][chains[c][1], :] = o
                st = st * wall + dsts[c]
            st_scr[d] = st
        return carry

    lax.fori_loop(0, n_chunks // SCAN_UNROLL, body, 0)

    o = of_scr[...] + ob_scr[...]
    y = _rms(o) * nw_ref[...]
    o_ref[...] = (y * _silu(g_ref[...])).astype(BF16)
    for d in range(2):
        sfin_ref[d] = st_scr[d].T


def _hgrn_mixer(proj, lb, norm_w, s0, layer, *, row_block0, batch, seq_len):
    chunk = SCAN_CHUNK
    amat, rowsel, pair = _scan_tables(chunk)
    amat = jnp.asarray(amat).astype(BF16)
    rowsel = jnp.asarray(rowsel)
    pair = jnp.asarray(pair)
    has_s0 = s0 is not None

    def col_spec(col):
        return pl.BlockSpec((seq_len, LANES), lambda b, h: (row_block0 + b, col // LANES + h))

    def const_spec(a):
        nd = a.ndim
        return pl.BlockSpec(a.shape, lambda b, h: (0,) * nd)

    in_specs = [col_spec(COL_Q), col_spec(COL_FA), col_spec(COL_FB), col_spec(COL_I), col_spec(COL_G),
                pl.BlockSpec((2, LANES), lambda b, h: (0, h)),
                pl.BlockSpec((1, LANES), lambda b, h: (0, 0)),
                const_spec(amat), const_spec(rowsel), const_spec(pair)]
    args = [proj, proj, proj, proj, proj, lb, norm_w.reshape(1, DV_A), amat, rowsel, pair]
    if has_s0:
        in_specs.append(pl.BlockSpec((None, None, 2, None, DK_A, DV_A),
                                     lambda b, h: (b, layer, 0, h, 0, 0)))
        args.append(s0)
    kern = functools.partial(_hgrn_kernel, seq_len=seq_len, chunk=chunk, has_s0=has_s0)
    return pl.pallas_call(
        kern,
        grid=(batch, H_A),
        in_specs=in_specs,
        out_specs=[pl.BlockSpec((seq_len, LANES), lambda b, h: (b, h)),
                   pl.BlockSpec((None, 2, None, DK_A, DV_A), lambda b, h: (b, 0, h, 0, 0))],
        out_shape=[jax.ShapeDtypeStruct((batch * seq_len, D_A), BF16),
                   jax.ShapeDtypeStruct((batch, 2, H_A, DK_A, DV_A), F32)],
        scratch_shapes=[pltpu.VMEM((seq_len, DV_A), F32), pltpu.VMEM((seq_len, DV_A), F32),
                        pltpu.VMEM((2, DV_A, DK_A), F32)],
        compiler_params=_params(2),
        name="hgrn_mixer",
    )(*args)


@functools.lru_cache(maxsize=None)
def _hyena_tables(seq_len):
    n = seq_len
    t = np.arange(n, dtype=np.float64)
    t01 = t / max(n - 1, 1)
    bands = np.linspace(1e-4, HY_BANDS - 1, HY_BANDS)
    ang = (2.0 * math.pi / n) * t[:, None] * bands[None, :]
    feats = np.concatenate([t01[:, None], np.cos(ang), -np.sin(ang)], axis=-1)
    feats = np.pad(feats, ((0, 0), (0, LANES - HY_EMB))).astype(np.float32)
    ft = np.outer(np.arange(n), np.arange(n)) % (2 * n)
    fc = np.cos(math.pi * ft / n)
    fs = -np.sin(math.pi * ft / n)
    fs[0, :] = 1.0 - 2.0 * (np.arange(n) % 2)
    return feats, fc.astype(np.float32), fs.astype(np.float32), fs.T.astype(np.float32).copy()


def _hyena_filter_kernel(feats_ref, w1_ref, b1_ref, w2_ref, b2_ref, w3_ref0, w3_ref1, w3_ref2, w3_ref3,
                         decay_ref, fc_ref, fs_ref, p_ref, h_scr, *, seq_len):
    n = seq_len
    hp = lax.Precision.HIGHEST

    @pl.when(pl.program_id(0) == 0)
    def _():
        h1 = jnp.sin(HY_SIN_W * (jnp.dot(feats_ref[...], w1_ref[...], precision=hp,
                                         preferred_element_type=F32) + b1_ref[...]))
        h_scr[...] = jnp.sin(HY_SIN_W * (jnp.dot(h1, w2_ref[...], precision=hp,
                                                 preferred_element_type=F32) + b2_ref[...]))

    h = h_scr[...]
    w3 = ((w3_ref0, w3_ref1), (w3_ref2, w3_ref3))
    cb = p_ref.shape[-1]
    row = lax.broadcasted_iota(jnp.int32, (n, cb), 0)
    t01 = row.astype(F32) / float(max(n - 1, 1))
    scale = 1.0 / (2.0 * n)
    wts = jnp.where(row == 0, scale, 2.0 * scale)
    sgn = (1 - 2 * (row & 1)).astype(F32)
    for o in range(HY_ORDER):
        win = jnp.exp(-t01 * jnp.abs(decay_ref[o:o + 1, :]))
        hf = jnp.dot(h, w3[o][0][...], precision=hp, preferred_element_type=F32) * win
        hb = jnp.dot(h, w3[o][1][...], precision=hp, preferred_element_type=F32) * win
        mag = jnp.where(row == 0, jnp.abs(hf + hb), jnp.abs(hf) + jnp.abs(hb))
        inv = 1.0 / (jnp.sum(mag, axis=0, keepdims=True) + EPS)
        even = (hf + hb) * inv
        odd = (hf - hb) * inv
        kre = jnp.dot(fc_ref[...], even.astype(BF16), preferred_element_type=F32)
        kim = jnp.dot(fs_ref[...], odd.astype(BF16), preferred_element_type=F32)
        knyq = jnp.sum(even * sgn, axis=0, keepdims=True)
        p_ref[3 * o + 0] = kre * wts
        p_ref[3 * o + 1] = jnp.where(row == 0, 0.0, kim * wts)
        p_ref[3 * o + 2] = jnp.where(row == 0, knyq * scale, kre * wts)


def _hyena_spectrum(seq_len, w1, b1, w2, b2, w3, decay):
    feats, fc, fs, _ = _hyena_tables(seq_len)
    feats = jnp.asarray(feats)
    mats = [jnp.asarray(m).astype(BF16) for m in (fc, fs)]
    w1p = jnp.pad(w1, ((0, LANES - HY_EMB), (0, 0)))
    cb = HY_CB
    nb = D_B // cb

    def w3_spec(k):
        return pl.BlockSpec((HY_HID, cb), lambda c: (0, k * nb + c))

    full2 = lambda a: pl.BlockSpec(a.shape, lambda c: (0, 0))
    b1r, b2r = b1.reshape(1, HY_HID), b2.reshape(1, HY_HID)
    return pl.pallas_call(
        functools.partial(_hyena_filter_kernel, seq_len=seq_len),
        grid=(nb,),
        in_specs=[full2(feats), full2(w1p), full2(b1r), full2(w2), full2(b2r),
                  w3_spec(0), w3_spec(1), w3_spec(2), w3_spec(3),
                  pl.BlockSpec((HY_ORDER, cb), lambda c: (0, c))] + [full2(m) for m in mats],
        out_specs=pl.BlockSpec((3 * HY_ORDER, seq_len, cb), lambda c: (0, 0, c)),
        out_shape=jax.ShapeDtypeStruct((3 * HY_ORDER, seq_len, D_B), F32),
        scratch_shapes=[pltpu.VMEM((seq_len, HY_HID), F32)],
        compiler_params=_params(1),
        name="hyena_filter",
    )(feats, w1p, b1r, w2, b2r, w3, w3, w3, w3, decay, *mats)


def _hyena_conv_kernel(v_ref, x1_ref, x2_ref, cwv_ref, cw1_ref, cw2_ref, cbv_ref, cb1_ref, cb2_ref,
                       p_ref, bias_ref, fc_ref, fs_ref, fst_ref, o_ref, *, seq_len):
    n = seq_len
    rows, cb = o_ref.shape
    nseq = rows // n
    pos = lax.broadcasted_iota(jnp.int32, (rows, cb), 0) & (n - 1)

    def dwconv(x_ref, w_ref, b_ref):
        x = x_ref[...]
        xm = jnp.where(pos == 0, 0.0, pltpu.roll(x, 1, axis=0))
        xp = jnp.where(pos == n - 1, 0.0, pltpu.roll(x, rows - 1, axis=0))
        y = xm * w_ref[0:1, :] + x * w_ref[1:2, :] + xp * w_ref[2:3, :] + b_ref[...]
        return jnp.concatenate([y[s * n:(s + 1) * n] for s in range(nseq)], axis=1)

    def tiled(a):
        return jnp.concatenate([a] * nseq, axis=1)

    def fftconv(z, o):
        zb16 = z.astype(BF16)
        zt = jnp.dot(fc_ref[...], zb16, preferred_element_type=F32)
        zb = jnp.dot(fs_ref[...], zb16, preferred_element_type=F32)
        p1, p3, p4 = tiled(p_ref[3 * o]), tiled(p_ref[3 * o + 1]), tiled(p_ref[3 * o + 2])
        yt = (zt * p1 - zb * p3).astype(BF16)
        yb = (zt * p3 + zb * p4).astype(BF16)
        y = (jnp.dot(fc_ref[...], yt, preferred_element_type=F32)
             + jnp.dot(fst_ref[...], yb, preferred_element_type=F32))
        return y + z * tiled(bias_ref[o:o + 1, :])

    v = dwconv(v_ref, cwv_ref, cbv_ref)
    x1 = dwconv(x1_ref, cw1_ref, cb1_ref)
    z2 = x1 * fftconv(v, 0)
    x2 = dwconv(x2_ref, cw2_ref, cb2_ref)
    out = x2 * fftconv(z2, 1)
    for s in range(nseq):
        o_ref[s * n:(s + 1) * n, :] = out[:, s * cb:(s + 1) * cb].astype(BF16)


def _hyena_mixer(proj, conv_w, conv_b, spectrum, bias, *, row0, n_rows, seq_len):
    assert seq_len & (seq_len - 1) == 0 and HY_ROWS % seq_len == 0 and row0 % HY_ROWS == 0
    _, fc, fs, fst = _hyena_tables(seq_len)
    mats = [jnp.asarray(m).astype(BF16) for m in (fc, fs, fst)]
    cb = HY_CB
    nb = D_B // cb
    rb0 = row0 // HY_ROWS

    def col_spec(k):
        return pl.BlockSpec((HY_ROWS, cb), lambda c, g: (rb0 + g, (COL_HY + k * D_B) // cb + c))

    def cw_spec(k):
        return pl.BlockSpec((3, cb), lambda c, g: (0, k * nb + c))

    def cb_spec(k):
        return pl.BlockSpec((1, cb), lambda c, g: (0, k * nb + c))

    mat_spec = pl.BlockSpec((seq_len, seq_len), lambda c, g: (0, 0))
    conv_b2 = conv_b.reshape(1, 3 * D_B)
    return pl.pallas_call(
        functools.partial(_hyena_conv_kernel, seq_len=seq_len),
        grid=(nb, n_rows // HY_ROWS),
        in_specs=[col_spec(0), col_spec(1), col_spec(2), cw_spec(0), cw_spec(1), cw_spec(2),
                  cb_spec(0), cb_spec(1), cb_spec(2),
                  pl.BlockSpec((3 * HY_ORDER, seq_len, cb), lambda c, g: (0, 0, c)),
                  pl.BlockSpec((HY_ORDER, cb), lambda c, g: (0, c))] + [mat_spec] * 3,
        out_specs=pl.BlockSpec((HY_ROWS, cb), lambda c, g: (g, c)),
        out_shape=jax.ShapeDtypeStruct((n_rows, D_B), BF16),
        compiler_params=_params(2),
        name="hyena_conv",
    )(proj, proj, proj, conv_w, conv_w, conv_w, conv_b2, conv_b2, conv_b2, spectrum, bias, *mats)


def _merge_kernel(oac_ref, oal_ref, obc_ref, obl_ref, ga_ref, gb_ref, wa_ref, wb_ref, o_ref,
                  wabf, wbbf):
    i = pl.program_id(1)

    @pl.when(i == 0)
    def _():
        wabf[...] = wa_ref[...].astype(BF16)
        wbbf[...] = wb_ref[...].astype(BF16)

    is_ctx = i < N_CTX_BLOCKS
    oa = jnp.where(is_ctx, oac_ref[...], oal_ref[...])
    ob = jnp.where(is_ctx, obc_ref[...], obl_ref[...])
    ya = jnp.dot(oa, wabf[...], preferred_element_type=F32)
    yb = jnp.dot(ob, wbbf[...], preferred_element_type=F32)
    o_ref[...] = (jax.nn.sigmoid(ga_ref[...]) * ya + jax.nn.sigmoid(gb_ref[...]) * yb).astype(BF16)


def _merge(oa_ctx, oa_lat, ob_ctx, ob_lat, proj, w_a, w_b, layer):
    tm, tn = TOK_BLOCK, 512
    nbn = D_MODEL // tn
    ctx_blk = lambda j, i: (jnp.minimum(i, N_CTX_BLOCKS - 1), 0)
    lat_blk = lambda j, i: (jnp.maximum(i - N_CTX_BLOCKS, 0), 0)
    return pl.pallas_call(
        _merge_kernel,
        grid=(nbn, N_TOK // tm),
        in_specs=[pl.BlockSpec((tm, D_A), ctx_blk), pl.BlockSpec((tm, D_A), lat_blk),
                  pl.BlockSpec((tm, D_B), ctx_blk), pl.BlockSpec((tm, D_B), lat_blk),
                  pl.BlockSpec((tm, tn), lambda j, i: (i, COL_GATE_A // tn + j)),
                  pl.BlockSpec((tm, tn), lambda j, i: (i, COL_GATE_B // tn + j)),
                  pl.BlockSpec((None, D_A, tn), lambda j, i: (layer, 0, j)),
                  pl.BlockSpec((None, D_B, tn), lambda j, i: (layer, 0, j))],
        out_specs=pl.BlockSpec((tm, tn), lambda j, i: (i, j)),
        out_shape=jax.ShapeDtypeStruct((N_TOK, D_MODEL), BF16),
        scratch_shapes=[pltpu.VMEM((D_A, tn), BF16), pltpu.VMEM((D_B, tn), BF16)],
        compiler_params=_params(2),
        name="merge",
    )(oa_ctx, oa_lat, ob_ctx, ob_lat, proj, proj, w_a, w_b)


def _resid_kernel(*refs, has_next, n_act):
    a_refs, refs = refs[:n_act], refs[n_act:]
    if has_next:
        (w_ref, x_ref, gpost_ref, gate_ref, gnext_ref, sh_ref, sc_ref, xo_ref, h_ref) = refs
    else:
        w_ref, x_ref, gpost_ref, gate_ref, xo_ref = refs
    k = pl.program_id(1)

    @pl.when(k == 0)
    def _():
        xo_ref[...] = jnp.zeros_like(xo_ref)

    part = w_ref.shape[0] // n_act
    y = None
    for p, a_ref in enumerate(a_refs):
        yp = jnp.dot(a_ref[...], w_ref[p * part:(p + 1) * part, :].astype(BF16),
                     preferred_element_type=F32)
        y = yp if y is None else y + yp
    xo_ref[...] += y

    @pl.when(k == pl.num_programs(1) - 1)
    def _():
        def rows_step(r, carry):
            rows = pl.ds(pl.multiple_of(r * EPILOGUE_ROWS, EPILOGUE_ROWS), EPILOGUE_ROWS)
            xn = x_ref[rows, :] + gate_ref[...] * (_rms(xo_ref[rows, :]) * gpost_ref[...])
            xo_ref[rows, :] = xn
            if has_next:
                h = (_rms(xn) * gnext_ref[...]) * (1.0 + sc_ref[...]) + sh_ref[...]
                h_ref[rows, :] = h.astype(BF16)
            return carry

        lax.fori_loop(0, xo_ref.shape[0] // EPILOGUE_ROWS, rows_step, 0)


def _proj_residual(a, w, x, g_post, mod4, layer, gate_chunk, nxt, blocks=(0, N_TOK_BLOCKS)):
    tm, tk = TOK_BLOCK, 512
    acts = a if isinstance(a, tuple) else (a,)
    n_act = len(acts)
    kdim = sum(p.shape[1] for p in acts)
    has_next = nxt is not None
    b0, nblk = blocks
    in_row_spec = pl.BlockSpec((tm, D_MODEL), lambda i, k: (b0 + i, 0))
    row_spec = pl.BlockSpec((tm, D_MODEL), lambda i, k: (i, 0))
    in_specs = [pl.BlockSpec((tm, tk // n_act), lambda i, k: (b0 + i, k)) for _ in acts]
    in_specs += [pl.BlockSpec((None, tk, D_MODEL), lambda i, k: (layer, k, 0)),
                 in_row_spec,
                 _gain_spec(layer), _mod_spec(layer, gate_chunk, tm, b0)]
    args = [*acts, w, x, g_post, mod4]
    out_specs = [row_spec]
    out_shape = [jax.ShapeDtypeStruct((nblk * tm, D_MODEL), F32)]
    if has_next:
        g_next, nl, sh_chunk, sc_chunk = nxt
        in_specs += [_gain_spec(nl), _mod_spec(nl, sh_chunk, tm, b0), _mod_spec(nl, sc_chunk, tm, b0)]
        args += [g_next, mod4, mod4]
        out_specs.append(row_spec)
        out_shape.append(jax.ShapeDtypeStruct((nblk * tm, D_MODEL), BF16))
    res = pl.pallas_call(
        functools.partial(_resid_kernel, has_next=has_next, n_act=n_act),
        grid=(nblk, kdim // tk),
        in_specs=in_specs,
        out_specs=out_specs,
        out_shape=out_shape,
        compiler_params=_params(2),
        name="proj_residual",
    )(*args)
    return res if has_next else (res[0], None)


def _ffn_up_kernel(h_ref, wa_ref, wv_ref, cwa_ref, cwv_ref, cba_ref, cbv_ref, o_ref,
                   wabf, wvbf, ua_scr, uv_scr, sa_scr, sv_scr):
    j, i = pl.program_id(0), pl.program_id(1)
    tm, tn = o_ref.shape
    grp, rt, halo = FFN_COL_GROUP, FFN_ROW_TILE, SUBLANES

    @pl.when(i == 0)
    def _():
        wabf[...] = wa_ref[...].astype(BF16)
        wvbf[...] = wv_ref[...].astype(BF16)

    @pl.when((i == 0) & (j == 0))
    def _():
        for u_scr in (ua_scr, uv_scr):
            u_scr[0:halo, :] = jnp.zeros((halo, grp), F32)
            u_scr[halo + tm:, :] = jnp.zeros((halo, grp), F32)
        for s_scr in (sa_scr, sv_scr):
            s_scr[:, 0:GRID_W, :] = jnp.zeros((3, GRID_W, grp), F32)
            s_scr[:, GRID_W + tm:, :] = jnp.zeros((3, GRID_W, grp), F32)

    def tile_taps(u_scr, r0, row_len):
        piece = u_scr[r0:r0 + rt + 2 * halo, :]
        t = r0 + lax.broadcasted_iota(jnp.int32, (rt, LANES), 0)
        pos = t & (row_len - 1)
        keep_prev = jnp.concatenate([(pos != 0).astype(F32)] * (grp // LANES), axis=1)
        keep_next = jnp.concatenate([(pos != row_len - 1).astype(F32)] * (grp // LANES), axis=1)
        n = rt + 2 * halo
        um = pltpu.roll(piece, 1, axis=0)[halo:halo + rt] * keep_prev
        up = pltpu.roll(piece, n - 1, axis=0)[halo:halo + rt] * keep_next
        return um, piece[halo:halo + rt], up

    def kernel_row(taps, cw_ref, dr, cols):
        um, u, up = taps
        r = 3 * (dr + 1)
        return um * cw_ref[r:r + 1, cols] + u * cw_ref[r + 1:r + 2, cols] + up * cw_ref[r + 2:r + 3, cols]

    def project(dst, cols, h):
        dst[0][halo:halo + tm, :] = jnp.dot(h, wabf[:, cols], preferred_element_type=F32)
        dst[1][halo:halo + tm, :] = jnp.dot(h, wvbf[:, cols], preferred_element_type=F32)

    def conv_glu(src, params, cols, latent):
        ua_scr, uv_scr = src
        wa, wv, ba, bv = params
        if not latent:
            for r0 in range(0, tm, rt):
                a = kernel_row(tile_taps(ua_scr, r0, SEQ), wa, 0, cols) + ba[:, cols]
                vv = kernel_row(tile_taps(uv_scr, r0, SEQ), wv, 0, cols) + bv[:, cols]
                o_ref[r0:r0 + rt, cols] = (_silu(a) * vv).astype(BF16)
        else:
            for r0 in range(0, tm, rt):
                ta, tv = tile_taps(ua_scr, r0, GRID_W), tile_taps(uv_scr, r0, GRID_W)
                for dr in (-1, 0, 1):
                    sa_scr[dr + 1, GRID_W + r0:GRID_W + r0 + rt, :] = kernel_row(ta, wa, dr, cols)
                    sv_scr[dr + 1, GRID_W + r0:GRID_W + r0 + rt, :] = kernel_row(tv, wv, dr, cols)
            for r0 in range(0, tm, rt):
                def gather(s_scr, b_ref):
                    return (b_ref[:, cols] + s_scr[0, r0:r0 + rt, :]
                            + s_scr[1, GRID_W + r0:GRID_W + r0 + rt, :]
                            + s_scr[2, 2 * GRID_W + r0:2 * GRID_W + r0 + rt, :])
                a, vv = gather(sa_scr, ba), gather(sv_scr, bv)
                o_ref[r0:r0 + rt, cols] = (_silu(a) * vv).astype(BF16)

    def block(latent):
        h = h_ref[...]
        for c in range(tn // grp):
            cols = slice(c * grp, (c + 1) * grp)
            project((ua_scr, uv_scr), cols, h)
            conv_glu((ua_scr, uv_scr), (cwa_ref, cwv_ref, cba_ref, cbv_ref), cols, latent)

    @pl.when(i < N_CTX_BLOCKS)
    def _():
        block(False)

    @pl.when(i >= N_CTX_BLOCKS)
    def _():
        block(True)


def _ffn_up(h, w_up, conv_w, conv_b, layer):
    tm, tn = TOK_BLOCK, 512
    nbn = D_FF // tn
    conv_w9 = conv_w.reshape(DEPTH, 9, 2 * D_FF)
    conv_b2 = conv_b.reshape(DEPTH, 1, 2 * D_FF)
    u_scratch = pltpu.VMEM((tm + 2 * SUBLANES, FFN_COL_GROUP), F32)
    s_scratch = pltpu.VMEM((3, tm + 2 * GRID_W, FFN_COL_GROUP), F32)
    return pl.pallas_call(
        _ffn_up_kernel,
        grid=(nbn, N_TOK // tm),
        in_specs=[pl.BlockSpec((tm, D_MODEL), lambda j, i: (i, 0)),
                  pl.BlockSpec((None, D_MODEL, tn), lambda j, i: (layer, 0, j)),
                  pl.BlockSpec((None, D_MODEL, tn), lambda j, i: (layer, 0, nbn + j)),
                  pl.BlockSpec((None, 9, tn), lambda j, i: (layer, 0, j)),
                  pl.BlockSpec((None, 9, tn), lambda j, i: (layer, 0, nbn + j)),
                  pl.BlockSpec((None, 1, tn), lambda j, i: (layer, 0, j)),
                  pl.BlockSpec((None, 1, tn), lambda j, i: (layer, 0, nbn + j))],
        out_specs=pl.BlockSpec((tm, tn), lambda j, i: (i, j)),
        out_shape=jax.ShapeDtypeStruct((N_TOK, D_FF), BF16),
        scratch_shapes=[pltpu.VMEM((D_MODEL, tn), BF16), pltpu.VMEM((D_MODEL, tn), BF16),
                        u_scratch, u_scratch, s_scratch, s_scratch],
        compiler_params=_params(2),
        name="ffn_up",
    )(h, w_up, w_up, conv_w9, conv_w9, conv_b2, conv_b2)


def kernel(x_prompt, x_sample, state_hgrn, c, c_ctx, w_mod, b_mod, g_pre_mix, g_post_mix, g_pre_ffn,
           g_post_ffn, w_in, hgrn_lower_bounds, hgrn_norm, hy_conv_w, hy_conv_b, hy_w1, hy_b1, hy_w2,
           hy_b2, hy_w3, hy_decay, hy_bias, w_branch_a, w_branch_b, w_out, ffn_w_up, ffn_conv_w,
           ffn_conv_b, ffn_w_down):
    p_lb = jax.nn.softmax(hgrn_lower_bounds.astype(F32), axis=0)
    cs = jnp.cumsum(p_lb, axis=0)
    lbs = cs - cs[:1]

    cond = jnp.concatenate([jnp.broadcast_to(c_ctx[None, :], (N_CTX_BLOCKS, D_MODEL)), c], axis=0)
    mod4 = _modulation(cond, w_mod, b_mod).reshape(DEPTH, N_TOK_BLOCKS, 1, N_MOD)

    g_pre_mix, g_post_mix, g_pre_ffn, g_post_ffn = (
        g.reshape(DEPTH, 1, D_MODEL) for g in (g_pre_mix, g_post_mix, g_pre_ffn, g_post_ffn))

    h, x = _prenorm(x_prompt.reshape(N_CTX, D_MODEL), x_sample.reshape(N_LAT, D_MODEL), g_pre_mix,
                    mod4, 0)
    new_states = []
    for l in range(DEPTH):
        proj = _in_proj(h, w_in, l)
        oa_ctx, s_ctx = _hgrn_mixer(proj, lbs[l], hgrn_norm[l], None, l,
                                    row_block0=0, batch=BATCH, seq_len=SEQ)
        oa_lat, _ = _hgrn_mixer(proj, lbs[l], hgrn_norm[l], state_hgrn, l,
                                row_block0=N_CTX // DEC_SEQ, batch=DEC_BATCH, seq_len=DEC_SEQ)
        new_states.append(s_ctx)
        ob = []
        for row0, n_rows, seq_len in ((0, N_CTX, SEQ), (N_CTX, N_LAT, DEC_SEQ)):
            spectrum = _hyena_spectrum(seq_len, hy_w1[l], hy_b1[l], hy_w2[l], hy_b2[l], hy_w3[l],
                                       hy_decay[l])
            ob.append(_hyena_mixer(proj, hy_conv_w[l], hy_conv_b[l], spectrum, hy_bias[l],
                                   row0=row0, n_rows=n_rows, seq_len=seq_len))
        merged = _merge(oa_ctx, oa_lat, ob[0], ob[1], proj, w_branch_a, w_branch_b, l)
        x, h2 = _proj_residual(merged, w_out, x, g_post_mix, mod4, l, 2, (g_pre_ffn, l, 3, 4))
        g = _ffn_up(h2, ffn_w_up, ffn_conv_w, ffn_conv_b, l)
        if l + 1 < DEPTH:
            x, h = _proj_residual(g, ffn_w_down, x, g_post_ffn, mod4, l, 5, (g_pre_mix, l + 1, 0, 1))
        else:
            y_ctx, _ = _proj_residual(g, ffn_w_down, x, g_post_ffn, mod4, l, 5, None,
                                      blocks=(0, N_CTX_BLOCKS))
            y_lat, _ = _proj_residual(g, ffn_w_down, x, g_post_ffn, mod4, l, 5, None,
                                      blocks=(N_CTX_BLOCKS, N_TOK_BLOCKS - N_CTX_BLOCKS))

    y_prompt = y_ctx.reshape(BATCH, SEQ, D_MODEL)
    y_sample = y_lat.reshape(DEC_BATCH, DEC_SEQ, D_MODEL)
    new_state = jnp.stack(new_states, axis=1).astype(x_prompt.dtype)
    return (y_prompt, y_sample, new_state)
```

```python
import functools
import math

import numpy as np
import jax
import jax.numpy as jnp
from jax import lax
from jax.experimental import pallas as pl
from jax.experimental.pallas import tpu as pltpu

F32 = jnp.float32
BF16 = jnp.bfloat16

D_MODEL = 2048
BATCH = 16
SEQ = 256
DEPTH = 2
DEC_BATCH = 4
DEC_SEQ = 1024
GRID_W = 64
D_A = D_MODEL // 2
D_B = D_MODEL // 2
HGRN_EXPAND = 128
H_A = D_A // HGRN_EXPAND
DK_A = HGRN_EXPAND
DV_A = D_A // H_A
HY_ORDER = 2
HY_EMB = 33
HY_BANDS = (HY_EMB - 1) // 2
HY_HID = 64
HY_SIN_W = 1.0
D_FF = 5632
N_MOD = 6 * D_MODEL
N_PROJ = 5 * D_A + 3 * D_B + 2 * D_MODEL
EPS = 1e-6

N_CTX = BATCH * SEQ
N_LAT = DEC_BATCH * DEC_SEQ
N_TOK = N_CTX + N_LAT
TOK_BLOCK = 1024
N_TOK_BLOCKS = N_TOK // TOK_BLOCK
N_CTX_BLOCKS = N_CTX // TOK_BLOCK

COL_Q, COL_FA, COL_FB, COL_I, COL_G = 0, D_A, 2 * D_A, 3 * D_A, 4 * D_A
COL_HY = 5 * D_A
COL_GATE_A = 5 * D_A + 3 * D_B
COL_GATE_B = COL_GATE_A + D_MODEL

LANES = 128
SUBLANES = 8
FFN_ROW_TILE = 128
FFN_TOK_BLOCKS = 2
SCAN_CHUNK = 32
SCAN_UNROLL = 8
HY_CB = 256
HY_ROWS = 2048
FFN_COL_GROUP = 256
EPILOGUE_ROWS = 128
VMEM_LIMIT = 60 * 1024 * 1024


def _params(n_axes):
    return pltpu.CompilerParams(dimension_semantics=("arbitrary",) * n_axes,
                                vmem_limit_bytes=VMEM_LIMIT)


def _silu(x):
    return x * jax.nn.sigmoid(x)


def _rms(x):
    return x * lax.rsqrt(jnp.mean(x * x, axis=-1, keepdims=True) + EPS)


def _mod_spec(layer, chunk, tm, block0=0):
    return pl.BlockSpec((None, None, 1, D_MODEL),
                        lambda i, *_: (layer, ((block0 + i) * tm) // TOK_BLOCK, 0, chunk))


def _gain_spec(layer):
    return pl.BlockSpec((None, 1, D_MODEL), lambda *_: (layer, 0, 0))


def _mod_kernel(cond_ref, w_ref, b_ref, o_ref):
    s = _silu(cond_ref[...]).astype(BF16)
    o_ref[...] = jnp.dot(s, w_ref[...].astype(BF16), preferred_element_type=F32) + b_ref[...]


def _modulation(cond, w_mod, b_mod):
    tn = 1024
    return pl.pallas_call(
        _mod_kernel,
        grid=(DEPTH, N_MOD // tn),
        in_specs=[pl.BlockSpec((N_TOK_BLOCKS, D_MODEL), lambda l, j: (0, 0)),
                  pl.BlockSpec((None, D_MODEL, tn), lambda l, j: (l, 0, j)),
                  pl.BlockSpec((None, 1, tn), lambda l, j: (l, 0, j))],
        out_specs=pl.BlockSpec((None, N_TOK_BLOCKS, tn), lambda l, j: (l, 0, j)),
        out_shape=jax.ShapeDtypeStruct((DEPTH, N_TOK_BLOCKS, N_MOD), F32),
        compiler_params=_params(2),
        name="modulation",
    )(cond, w_mod, b_mod.reshape(DEPTH, 1, N_MOD))


def _prenorm_kernel(xc_ref, xl_ref, g_ref, sh_ref, sc_ref, h_ref, x_ref, *, n_ctx_blocks):
    x = jnp.where(pl.program_id(0) < n_ctx_blocks, xc_ref[...], xl_ref[...])
    x_ref[...] = x
    h = (_rms(x) * g_ref[...]) * (1.0 + sc_ref[...]) + sh_ref[...]
    h_ref[...] = h.astype(BF16)


def _prenorm(x_ctx, x_lat, g, mod4, layer):
    tm = 512
    nc = N_CTX // tm
    row_spec = pl.BlockSpec((tm, D_MODEL), lambda i: (i, 0))
    return pl.pallas_call(
        functools.partial(_prenorm_kernel, n_ctx_blocks=nc),
        grid=(N_TOK // tm,),
        in_specs=[pl.BlockSpec((tm, D_MODEL), lambda i: (jnp.minimum(i, nc - 1), 0)),
                  pl.BlockSpec((tm, D_MODEL), lambda i: (jnp.maximum(i - nc, 0), 0)),
                  _gain_spec(layer), _mod_spec(layer, 0, tm), _mod_spec(layer, 1, tm)],
        out_specs=[row_spec, row_spec],
        out_shape=[jax.ShapeDtypeStruct((N_TOK, D_MODEL), BF16),
                   jax.ShapeDtypeStruct((N_TOK, D_MODEL), F32)],
        compiler_params=_params(1),
        name="prenorm",
    )(x_ctx, x_lat, g, mod4, mod4)


def _inproj_kernel(h_ref, w_ref, o_ref, wbf_ref):
    @pl.when(pl.program_id(1) == 0)
    def _():
        wbf_ref[...] = w_ref[...].astype(BF16)

    o_ref[...] = jnp.dot(h_ref[...], wbf_ref[...], preferred_element_type=F32)


def _in_proj(h, w_in, layer):
    tm, tn = 1024, 1024
    return pl.pallas_call(
        _inproj_kernel,
        grid=(N_PROJ // tn, N_TOK // tm),
        in_specs=[pl.BlockSpec((tm, D_MODEL), lambda j, i: (i, 0)),
                  pl.BlockSpec((None, D_MODEL, tn), lambda j, i: (layer, 0, j))],
        out_specs=pl.BlockSpec((tm, tn), lambda j, i: (i, j)),
        out_shape=jax.ShapeDtypeStruct((N_TOK, N_PROJ), F32),
        scratch_shapes=[pltpu.VMEM((D_MODEL, tn), BF16)],
        compiler_params=_params(2),
        name="in_proj",
    )(h, w_in)


def _scan_levels(chunk):
    m, levels = chunk // 2, []
    while m >= 1:
        levels.append(m)
        m //= 2
    return tuple(levels)


@functools.lru_cache(maxsize=None)
def _scan_tables(chunk):
    levels = _scan_levels(chunk)
    amats, rowsels, pairs = [], [], []
    for rev in (False, True):
        blocks, sels = [], []
        for m in levels:
            a = np.zeros((chunk, chunk), np.float32)
            sel = np.zeros((chunk,), np.float32)
            for t in range(chunk):
                mid = 2 * m * (t // (2 * m)) + m - 1
                upper = t > mid
                if not rev:
                    if upper:
                        a[t, mid + 1:t + 1] = 1.0
                    else:
                        a[t, t + 1:mid + 1] = 1.0
                    sel[t] = 1.0 if upper else 0.0
                else:
                    if upper:
                        a[t, mid + 1:t] = 1.0
                    else:
                        a[t, t:mid + 1] = 1.0
                    sel[t] = 0.0 if upper else 1.0
            blocks.append(a)
            sels.append(np.repeat(sel[:, None], LANES, axis=1))
        idx = np.arange(chunk)
        if not rev:
            inter = (idx[None, :] <= idx[:, None]).astype(np.float32)
            carry = (idx[None, :] > idx[:, None]).astype(np.float32)
        else:
            inter = (idx[None, :] >= idx[:, None]).astype(np.float32)
            carry = (idx[None, :] < idx[:, None]).astype(np.float32)
        blocks += [inter, carry, np.ones((16, chunk), np.float32)]
        amats.append(np.concatenate(blocks, axis=0))
        rowsels.append(np.stack(sels))
        masks = []
        for m, sel in zip(levels, sels):
            same = idx[:, None] // (2 * m) == idx[None, :] // (2 * m)
            masks.append((same & (sel[:, :1] > 0.5) & (sel[:, :1].T < 0.5)).astype(np.float32))
        masks.append(np.eye(chunk, dtype=np.float32))
        pairs.append(np.stack(masks))
    return np.stack(amats), np.stack(rowsels), np.stack(pairs)


def _dot_exact_rhs(a_bf16, x):
    x1 = x.astype(BF16)
    x2 = (x - x1.astype(F32)).astype(BF16)
    n = x.shape[1]
    y = jnp.dot(a_bf16, jnp.concatenate([x1, x2], axis=1), preferred_element_type=F32)
    return y[:, :n] + y[:, n:]


_NT = (((1,), (1,)), ((), ()))
_TN = (((0,), (0,)), ((), ()))


def _hgrn_kernel(*refs, seq_len, chunk, has_s0):
    if has_s0:
        (q_ref, fa_ref, fb_ref, i_ref, g_ref, lb_ref, nw_ref, amat_ref, rowsel_ref, pair_ref, s0_ref,
         o_ref, sfin_ref, of_scr, ob_scr, st_scr) = refs
    else:
        (q_ref, fa_ref, fb_ref, i_ref, g_ref, lb_ref, nw_ref, amat_ref, rowsel_ref, pair_ref,
         o_ref, sfin_ref, of_scr, ob_scr, st_scr) = refs
    n_chunks = seq_len // chunk
    n_lev = len(_scan_levels(chunk))

    for d in range(2):
        if has_s0:
            st_scr[d] = s0_ref[d].T
        else:
            st_scr[d] = jnp.zeros((DV_A, DK_A), F32)

    f_refs = (fa_ref, fb_ref)
    o_scrs = (of_scr, ob_scr)

    def gates(d, rows):
        z = f_refs[d][rows, :]
        q = _silu(q_ref[rows, :])
        lbd = lb_ref[d:d + 1, :]
        e = jnp.exp(-jnp.abs(z))
        den = 1.0 + e
        a1 = jnp.log(lbd)
        a2 = jnp.log1p(-lbd) + (jnp.minimum(z, 0.0) - jnp.log(den))
        logf = jnp.maximum(a1, a2) + jnp.log(1.0 + jnp.exp(-jnp.abs(a1 - a2)))
        k = (1.0 - lbd) * (jnp.where(z >= 0.0, e, 1.0) / den)
        return q, k, logf

    def body(jj, carry):
        chains = []
        for d in range(2):
            for u in range(SCAN_UNROLL):
                j = jj * SCAN_UNROLL + u
                if d == 1:
                    j = n_chunks - 1 - j
                chains.append((d, pl.ds(pl.multiple_of(j * chunk, chunk), chunk)))
        qkf = [gates(d, rows) for d, rows in chains]
        ws = [jnp.exp(_dot_exact_rhs(amat_ref[d], logf))
              for (d, _), (_, _, logf) in zip(chains, qkf)]
        vbs = [i_ref[rows, :].astype(BF16) for _, rows in chains]
        atts, dsts = [], []
        for (d, _), (q, k, _), w, vb in zip(chains, qkf, ws, vbs):
            att = lax.dot_general(q.astype(BF16), k.astype(BF16), _NT,
                                  preferred_element_type=F32) * pair_ref[d, n_lev]
            for lv in range(n_lev):
                x = (jnp.where(rowsel_ref[d, lv] > 0.5, q, k)
                     * w[lv * chunk:(lv + 1) * chunk]).astype(BF16)
                att = att + lax.dot_general(x, x, _NT, preferred_element_type=F32) * pair_ref[d, lv]
            atts.append(att.astype(BF16))
            wk = w[(n_lev + 1) * chunk:(n_lev + 2) * chunk]
            dsts.append(lax.dot_general(vb, (k * wk).astype(BF16), _TN, preferred_element_type=F32))
        for d in range(2):
            st = st_scr[d]
            for u in range(SCAN_UNROLL):
                c = d * SCAN_UNROLL + u
                q, w = qkf[c][0], ws[c]
                wq = w[n_lev * chunk:(n_lev + 1) * chunk]
                wall = w[(n_lev + 2) * chunk:(n_lev + 2) * chunk + 1]
                o = lax.dot_general((q * wq).astype(BF16), st.astype(BF16), _NT,
                                    preferred_element_type=F32)
                o = o + jnp.dot(atts[c], vbs[c], preferred_element_type=F32)
                o_scrs[d][chains[c][1], :] = o
                st = st * wall + dsts[c]
            st_scr[d] = st
        return carry

    lax.fori_loop(0, n_chunks // SCAN_UNROLL, body, 0)

    o = of_scr[...] + ob_scr[...]
    y = _rms(o) * nw_ref[...]
    o_ref[...] = (y * _silu(g_ref[...])).astype(BF16)
    for d in range(2):
        sfin_ref[d] = st_scr[d].T


def _hgrn_mixer(proj, lb, norm_w, s0, layer, *, row_block0, batch, seq_len):
    chunk = SCAN_CHUNK
    amat, rowsel, pair = _scan_tables(chunk)
    amat = jnp.asarray(amat).astype(BF16)
    rowsel = jnp.asarray(rowsel)
    pair = jnp.asarray(pair)
    has_s0 = s0 is not None

    def col_spec(col):
        return pl.BlockSpec((seq_len, LANES), lambda b, h: (row_block0 + b, col // LANES + h))

    def const_spec(a):
        nd = a.ndim
        return pl.BlockSpec(a.shape, lambda b, h: (0,) * nd)

    in_specs = [col_spec(COL_Q), col_spec(COL_FA), col_spec(COL_FB), col_spec(COL_I), col_spec(COL_G),
                pl.BlockSpec((2, LANES), lambda b, h: (0, h)),
                pl.BlockSpec((1, LANES), lambda b, h: (0, 0)),
                const_spec(amat), const_spec(rowsel), const_spec(pair)]
    args = [proj, proj, proj, proj, proj, lb, norm_w.reshape(1, DV_A), amat, rowsel, pair]
    if has_s0:
        in_specs.append(pl.BlockSpec((None, None, 2, None, DK_A, DV_A),
                                     lambda b, h: (b, layer, 0, h, 0, 0)))
        args.append(s0)
    kern = functools.partial(_hgrn_kernel, seq_len=seq_len, chunk=chunk, has_s0=has_s0)
    return pl.pallas_call(
        kern,
        grid=(batch, H_A),
        in_specs=in_specs,
        out_specs=[pl.BlockSpec((seq_len, LANES), lambda b, h: (b, h)),
                   pl.BlockSpec((None, 2, None, DK_A, DV_A), lambda b, h: (b, 0, h, 0, 0))],
        out_shape=[jax.ShapeDtypeStruct((batch * seq_len, D_A), BF16),
                   jax.ShapeDtypeStruct((batch, 2, H_A, DK_A, DV_A), F32)],
        scratch_shapes=[pltpu.VMEM((seq_len, DV_A), F32), pltpu.VMEM((seq_len, DV_A), F32),
                        pltpu.VMEM((2, DV_A, DK_A), F32)],
        compiler_params=_params(2),
        name="hgrn_mixer",
    )(*args)


@functools.lru_cache(maxsize=None)
def _hyena_tables(seq_len):
    n = seq_len
    t = np.arange(n, dtype=np.float64)
    t01 = t / max(n - 1, 1)
    bands = np.linspace(1e-4, HY_BANDS - 1, HY_BANDS)
    ang = (2.0 * math.pi / n) * t[:, None] * bands[None, :]
    feats = np.concatenate([t01[:, None], np.cos(ang), -np.sin(ang)], axis=-1)
    feats = np.pad(feats, ((0, 0), (0, LANES - HY_EMB))).astype(np.float32)
    ft = np.outer(np.arange(n), np.arange(n)) % (2 * n)
    fc = np.cos(math.pi * ft / n)
    fs = -np.sin(math.pi * ft / n)
    fs[0, :] = 1.0 - 2.0 * (np.arange(n) % 2)
    return feats, fc.astype(np.float32), fs.astype(np.float32), fs.T.astype(np.float32).copy()


def _hyena_filter_kernel(feats_ref, w1_ref, b1_ref, w2_ref, b2_ref, w3_ref0, w3_ref1, w3_ref2, w3_ref3,
                         decay_ref, fc_ref, fs_ref, p_ref, h_scr, *, seq_len):
    n = seq_len
    hp = lax.Precision.HIGHEST

    @pl.when(pl.program_id(0) == 0)
    def _():
        h1 = jnp.sin(HY_SIN_W * (jnp.dot(feats_ref[...], w1_ref[...], precision=hp,
                                         preferred_element_type=F32) + b1_ref[...]))
        h_scr[...] = jnp.sin(HY_SIN_W * (jnp.dot(h1, w2_ref[...], precision=hp,
                                                 preferred_element_type=F32) + b2_ref[...]))

    h = h_scr[...]
    w3 = ((w3_ref0, w3_ref1), (w3_ref2, w3_ref3))
    cb = p_ref.shape[-1]
    row = lax.broadcasted_iota(jnp.int32, (n, cb), 0)
    t01 = row.astype(F32) / float(max(n - 1, 1))
    scale = 1.0 / (2.0 * n)
    wts = jnp.where(row == 0, scale, 2.0 * scale)
    sgn = (1 - 2 * (row & 1)).astype(F32)
    for o in range(HY_ORDER):
        win = jnp.exp(-t01 * jnp.abs(decay_ref[o:o + 1, :]))
        hf = jnp.dot(h, w3[o][0][...], precision=hp, preferred_element_type=F32) * win
        hb = jnp.dot(h, w3[o][1][...], precision=hp, preferred_element_type=F32) * win
        mag = jnp.where(row == 0, jnp.abs(hf + hb), jnp.abs(hf) + jnp.abs(hb))
        inv = 1.0 / (jnp.sum(mag, axis=0, keepdims=True) + EPS)
        even = (hf + hb) * inv
        odd = (hf - hb) * inv
        kre = jnp.dot(fc_ref[...], even.astype(BF16), preferred_element_type=F32)
        kim = jnp.dot(fs_ref[...], odd.astype(BF16), preferred_element_type=F32)
        knyq = jnp.sum(even * sgn, axis=0, keepdims=True)
        p_ref[3 * o + 0] = kre * wts
        p_ref[3 * o + 1] = jnp.where(row == 0, 0.0, kim * wts)
        p_ref[3 * o + 2] = jnp.where(row == 0, knyq * scale, kre * wts)


def _hyena_spectrum(seq_len, w1, b1, w2, b2, w3, decay):
    feats, fc, fs, _ = _hyena_tables(seq_len)
    feats = jnp.asarray(feats)
    mats = [jnp.asarray(m).astype(BF16) for m in (fc, fs)]
    w1p = jnp.pad(w1, ((0, LANES - HY_EMB), (0, 0)))
    cb = HY_CB
    nb = D_B // cb

    def w3_spec(k):
        return pl.BlockSpec((HY_HID, cb), lambda c: (0, k * nb + c))

    full2 = lambda a: pl.BlockSpec(a.shape, lambda c: (0, 0))
    b1r, b2r = b1.reshape(1, HY_HID), b2.reshape(1, HY_HID)
    return pl.pallas_call(
        functools.partial(_hyena_filter_kernel, seq_len=seq_len),
        grid=(nb,),
        in_specs=[full2(feats), full2(w1p), full2(b1r), full2(w2), full2(b2r),
                  w3_spec(0), w3_spec(1), w3_spec(2), w3_spec(3),
                  pl.BlockSpec((HY_ORDER, cb), lambda c: (0, c))] + [full2(m) for m in mats],
        out_specs=pl.BlockSpec((3 * HY_ORDER, seq_len, cb), lambda c: (0, 0, c)),
        out_shape=jax.ShapeDtypeStruct((3 * HY_ORDER, seq_len, D_B), F32),
        scratch_shapes=[pltpu.VMEM((seq_len, HY_HID), F32)],
        compiler_params=_params(1),
        name="hyena_filter",
    )(feats, w1p, b1r, w2, b2r, w3, w3, w3, w3, decay, *mats)


def _hyena_conv_kernel(v_ref, x1_ref, x2_ref, cwv_ref, cw1_ref, cw2_ref, cbv_ref, cb1_ref, cb2_ref,
                       p_ref, bias_ref, fc_ref, fs_ref, fst_ref, o_ref, *, seq_len):
    n = seq_len
    rows, cb = o_ref.shape
    nseq = rows // n
    pos = lax.broadcasted_iota(jnp.int32, (rows, cb), 0) & (n - 1)

    def dwconv(x_ref, w_ref, b_ref):
        x = x_ref[...]
        xm = jnp.where(pos == 0, 0.0, pltpu.roll(x, 1, axis=0))
        xp = jnp.where(pos == n - 1, 0.0, pltpu.roll(x, rows - 1, axis=0))
        y = xm * w_ref[0:1, :] + x * w_ref[1:2, :] + xp * w_ref[2:3, :] + b_ref[...]
        return jnp.concatenate([y[s * n:(s + 1) * n] for s in range(nseq)], axis=1)

    def tiled(a):
        return jnp.concatenate([a] * nseq, axis=1)

    def fftconv(z, o):
        zb16 = z.astype(BF16)
        zt = jnp.dot(fc_ref[...], zb16, preferred_element_type=F32)
        zb = jnp.dot(fs_ref[...], zb16, preferred_element_type=F32)
        p1, p3, p4 = tiled(p_ref[3 * o]), tiled(p_ref[3 * o + 1]), tiled(p_ref[3 * o + 2])
        yt = (zt * p1 - zb * p3).astype(BF16)
        yb = (zt * p3 + zb * p4).astype(BF16)
        y = (jnp.dot(fc_ref[...], yt, preferred_element_type=F32)
             + jnp.dot(fst_ref[...], yb, preferred_element_type=F32))
        return y + z * tiled(bias_ref[o:o + 1, :])

    v = dwconv(v_ref, cwv_ref, cbv_ref)
    x1 = dwconv(x1_ref, cw1_ref, cb1_ref)
    z2 = x1 * fftconv(v, 0)
    x2 = dwconv(x2_ref, cw2_ref, cb2_ref)
    out = x2 * fftconv(z2, 1)
    for s in range(nseq):
        o_ref[s * n:(s + 1) * n, :] = out[:, s * cb:(s + 1) * cb].astype(BF16)


def _hyena_mixer(proj, conv_w, conv_b, spectrum, bias, *, row0, n_rows, seq_len):
    assert seq_len & (seq_len - 1) == 0 and HY_ROWS % seq_len == 0 and row0 % HY_ROWS == 0
    _, fc, fs, fst = _hyena_tables(seq_len)
    mats = [jnp.asarray(m).astype(BF16) for m in (fc, fs, fst)]
    cb = HY_CB
    nb = D_B // cb
    rb0 = row0 // HY_ROWS

    def col_spec(k):
        return pl.BlockSpec((HY_ROWS, cb), lambda c, g: (rb0 + g, (COL_HY + k * D_B) // cb + c))

    def cw_spec(k):
        return pl.BlockSpec((3, cb), lambda c, g: (0, k * nb + c))

    def cb_spec(k):
        return pl.BlockSpec((1, cb), lambda c, g: (0, k * nb + c))

    mat_spec = pl.BlockSpec((seq_len, seq_len), lambda c, g: (0, 0))
    conv_b2 = conv_b.reshape(1, 3 * D_B)
    return pl.pallas_call(
        functools.partial(_hyena_conv_kernel, seq_len=seq_len),
        grid=(nb, n_rows // HY_ROWS),
        in_specs=[col_spec(0), col_spec(1), col_spec(2), cw_spec(0), cw_spec(1), cw_spec(2),
                  cb_spec(0), cb_spec(1), cb_spec(2),
                  pl.BlockSpec((3 * HY_ORDER, seq_len, cb), lambda c, g: (0, 0, c)),
                  pl.BlockSpec((HY_ORDER, cb), lambda c, g: (0, c))] + [mat_spec] * 3,
        out_specs=pl.BlockSpec((HY_ROWS, cb), lambda c, g: (g, c)),
        out_shape=jax.ShapeDtypeStruct((n_rows, D_B), BF16),
        compiler_params=_params(2),
        name="hyena_conv",
    )(proj, proj, proj, conv_w, conv_w, conv_w, conv_b2, conv_b2, conv_b2, spectrum, bias, *mats)


def _merge_kernel(oac_ref, oal_ref, obc_ref, obl_ref, ga_ref, gb_ref, wa_ref, wb_ref, o_ref,
                  wabf, wbbf):
    i = pl.program_id(1)

    @pl.when(i == 0)
    def _():
        wabf[...] = wa_ref[...].astype(BF16)
        wbbf[...] = wb_ref[...].astype(BF16)

    is_ctx = i < N_CTX_BLOCKS
    oa = jnp.where(is_ctx, oac_ref[...], oal_ref[...])
    ob = jnp.where(is_ctx, obc_ref[...], obl_ref[...])
    ya = jnp.dot(oa, wabf[...], preferred_element_type=F32)
    yb = jnp.dot(ob, wbbf[...], preferred_element_type=F32)
    o_ref[...] = (jax.nn.sigmoid(ga_ref[...]) * ya + jax.nn.sigmoid(gb_ref[...]) * yb).astype(BF16)


def _merge(oa_ctx, oa_lat, ob_ctx, ob_lat, proj, w_a, w_b, layer):
    tm, tn = TOK_BLOCK, 512
    nbn = D_MODEL // tn
    ctx_blk = lambda j, i: (jnp.minimum(i, N_CTX_BLOCKS - 1), 0)
    lat_blk = lambda j, i: (jnp.maximum(i - N_CTX_BLOCKS, 0), 0)
    return pl.pallas_call(
        _merge_kernel,
        grid=(nbn, N_TOK // tm),
        in_specs=[pl.BlockSpec((tm, D_A), ctx_blk), pl.BlockSpec((tm, D_A), lat_blk),
                  pl.BlockSpec((tm, D_B), ctx_blk), pl.BlockSpec((tm, D_B), lat_blk),
                  pl.BlockSpec((tm, tn), lambda j, i: (i, COL_GATE_A // tn + j)),
                  pl.BlockSpec((tm, tn), lambda j, i: (i, COL_GATE_B // tn + j)),
                  pl.BlockSpec((None, D_A, tn), lambda j, i: (layer, 0, j)),
                  pl.BlockSpec((None, D_B, tn), lambda j, i: (layer, 0, j))],
        out_specs=pl.BlockSpec((tm, tn), lambda j, i: (i, j)),
        out_shape=jax.ShapeDtypeStruct((N_TOK, D_MODEL), BF16),
        scratch_shapes=[pltpu.VMEM((D_A, tn), BF16), pltpu.VMEM((D_B, tn), BF16)],
        compiler_params=_params(2),
        name="merge",
    )(oa_ctx, oa_lat, ob_ctx, ob_lat, proj, proj, w_a, w_b)


def _resid_kernel(*refs, has_next, n_act):
    a_refs, refs = refs[:n_act], refs[n_act:]
    if has_next:
        (w_ref, x_ref, gpost_ref, gate_ref, gnext_ref, sh_ref, sc_ref, xo_ref, h_ref) = refs
    else:
        w_ref, x_ref, gpost_ref, gate_ref, xo_ref = refs
    k = pl.program_id(1)

    @pl.when(k == 0)
    def _():
        xo_ref[...] = jnp.zeros_like(xo_ref)

    part = w_ref.shape[0] // n_act
    y = None
    for p, a_ref in enumerate(a_refs):
        yp = jnp.dot(a_ref[...], w_ref[p * part:(p + 1) * part, :].astype(BF16),
                     preferred_element_type=F32)
        y = yp if y is None else y + yp
    xo_ref[...] += y

    @pl.when(k == pl.num_programs(1) - 1)
    def _():
        def rows_step(r, carry):
            rows = pl.ds(pl.multiple_of(r * EPILOGUE_ROWS, EPILOGUE_ROWS), EPILOGUE_ROWS)
            xn = x_ref[rows, :] + gate_ref[...] * (_rms(xo_ref[rows, :]) * gpost_ref[...])
            xo_ref[rows, :] = xn
            if has_next:
                h = (_rms(xn) * gnext_ref[...]) * (1.0 + sc_ref[...]) + sh_ref[...]
                h_ref[rows, :] = h.astype(BF16)
            return carry

        lax.fori_loop(0, xo_ref.shape[0] // EPILOGUE_ROWS, rows_step, 0)


def _proj_residual(a, w, x, g_post, mod4, layer, gate_chunk, nxt, blocks=(0, N_TOK_BLOCKS)):
    tm, tk = TOK_BLOCK, 512
    acts = a if isinstance(a, tuple) else (a,)
    n_act = len(acts)
    kdim = sum(p.shape[1] for p in acts)
    has_next = nxt is not None
    b0, nblk = blocks
    in_row_spec = pl.BlockSpec((tm, D_MODEL), lambda i, k: (b0 + i, 0))
    row_spec = pl.BlockSpec((tm, D_MODEL), lambda i, k: (i, 0))
    in_specs = [pl.BlockSpec((tm, tk // n_act), lambda i, k: (b0 + i, k)) for _ in acts]
    in_specs += [pl.BlockSpec((None, tk, D_MODEL), lambda i, k: (layer, k, 0)),
                 in_row_spec,
                 _gain_spec(layer), _mod_spec(layer, gate_chunk, tm, b0)]
    args = [*acts, w, x, g_post, mod4]
    out_specs = [row_spec]
    out_shape = [jax.ShapeDtypeStruct((nblk * tm, D_MODEL), F32)]
    if has_next:
        g_next, nl, sh_chunk, sc_chunk = nxt
        in_specs += [_gain_spec(nl), _mod_spec(nl, sh_chunk, tm, b0), _mod_spec(nl, sc_chunk, tm, b0)]
        args += [g_next, mod4, mod4]
        out_specs.append(row_spec)
        out_shape.append(jax.ShapeDtypeStruct((nblk * tm, D_MODEL), BF16))
    res = pl.pallas_call(
        functools.partial(_resid_kernel, has_next=has_next, n_act=n_act),
        grid=(nblk, kdim // tk),
        in_specs=in_specs,
        out_specs=out_specs,
        out_shape=out_shape,
        compiler_params=_params(2),
        name="proj_residual",
    )(*args)
    return res if has_next else (res[0], None)


def _ffn_up_kernel(h_ref, wa_ref, wv_ref, cwa_ref, cwv_ref, cba_ref, cbv_ref, o_ref,
                   wabf, wvbf, ua_scr, uv_scr, sa_scr, sv_scr):
    j, i = pl.program_id(0), pl.program_id(1)
    n_sub, tn = o_ref.shape[0] // TOK_BLOCK, o_ref.shape[1]
    tm = TOK_BLOCK
    grp, rt, halo = FFN_COL_GROUP, FFN_ROW_TILE, SUBLANES

    @pl.when(i == 0)
    def _():
        wabf[...] = wa_ref[...].astype(BF16)
        wvbf[...] = wv_ref[...].astype(BF16)

    @pl.when((i == 0) & (j == 0))
    def _():
        for u_scr in (ua_scr, uv_scr):
            u_scr[0:halo, :] = jnp.zeros((halo, grp), F32)
            u_scr[halo + tm:, :] = jnp.zeros((halo, grp), F32)
        for s_scr in (sa_scr, sv_scr):
            s_scr[:, 0:GRID_W, :] = jnp.zeros((3, GRID_W, grp), F32)
            s_scr[:, GRID_W + tm:, :] = jnp.zeros((3, GRID_W, grp), F32)

    def tile_taps(u_scr, r0, row_len):
        piece = u_scr[r0:r0 + rt + 2 * halo, :]
        t = r0 + lax.broadcasted_iota(jnp.int32, (rt, LANES), 0)
        pos = t & (row_len - 1)
        keep_prev = jnp.concatenate([(pos != 0).astype(F32)] * (grp // LANES), axis=1)
        keep_next = jnp.concatenate([(pos != row_len - 1).astype(F32)] * (grp // LANES), axis=1)
        n = rt + 2 * halo
        um = pltpu.roll(piece, 1, axis=0)[halo:halo + rt] * keep_prev
        up = pltpu.roll(piece, n - 1, axis=0)[halo:halo + rt] * keep_next
        return um, piece[halo:halo + rt], up

    def kernel_row(taps, cw_ref, dr, cols):
        um, u, up = taps
        r = 3 * (dr + 1)
        return um * cw_ref[r:r + 1, cols] + u * cw_ref[r + 1:r + 2, cols] + up * cw_ref[r + 2:r + 3, cols]

    def project(dst, cols, h):
        dst[0][halo:halo + tm, :] = jnp.dot(h, wabf[:, cols], preferred_element_type=F32)
        dst[1][halo:halo + tm, :] = jnp.dot(h, wvbf[:, cols], preferred_element_type=F32)

    def conv_glu(src, params, cols, latent, row_off):
        ua_scr, uv_scr = src
        wa, wv, ba, bv = params
        if not latent:
            for r0 in range(0, tm, rt):
                a = kernel_row(tile_taps(ua_scr, r0, SEQ), wa, 0, cols) + ba[:, cols]
                vv = kernel_row(tile_taps(uv_scr, r0, SEQ), wv, 0, cols) + bv[:, cols]
                o_ref[row_off + r0:row_off + r0 + rt, cols] = (_silu(a) * vv).astype(BF16)
        else:
            for r0 in range(0, tm, rt):
                ta, tv = tile_taps(ua_scr, r0, GRID_W), tile_taps(uv_scr, r0, GRID_W)
                for dr in (-1, 0, 1):
                    sa_scr[dr + 1, GRID_W + r0:GRID_W + r0 + rt, :] = kernel_row(ta, wa, dr, cols)
                    sv_scr[dr + 1, GRID_W + r0:GRID_W + r0 + rt, :] = kernel_row(tv, wv, dr, cols)
            for r0 in range(0, tm, rt):
                def gather(s_scr, b_ref):
                    return (b_ref[:, cols] + s_scr[0, r0:r0 + rt, :]
                            + s_scr[1, GRID_W + r0:GRID_W + r0 + rt, :]
                            + s_scr[2, 2 * GRID_W + r0:2 * GRID_W + r0 + rt, :])
                a, vv = gather(sa_scr, ba), gather(sv_scr, bv)
                o_ref[row_off + r0:row_off + r0 + rt, cols] = (_silu(a) * vv).astype(BF16)

    def block(latent):
        for sub in range(n_sub):
            h = h_ref[sub * tm:(sub + 1) * tm, :]
            for c in range(tn // grp):
                cols = slice(c * grp, (c + 1) * grp)
                project((ua_scr, uv_scr), cols, h)
                conv_glu((ua_scr, uv_scr), (cwa_ref, cwv_ref, cba_ref, cbv_ref), cols, latent,
                         sub * tm)

    @pl.when(i * n_sub < N_CTX_BLOCKS)
    def _():
        block(False)

    @pl.when(i * n_sub >= N_CTX_BLOCKS)
    def _():
        block(True)


def _ffn_up(h, w_up, conv_w, conv_b, layer):
    assert N_CTX_BLOCKS % FFN_TOK_BLOCKS == 0
    tm, tn = FFN_TOK_BLOCKS * TOK_BLOCK, 512
    nbn = D_FF // tn
    conv_w9 = conv_w.reshape(DEPTH, 9, 2 * D_FF)
    conv_b2 = conv_b.reshape(DEPTH, 1, 2 * D_FF)
    u_scratch = pltpu.VMEM((TOK_BLOCK + 2 * SUBLANES, FFN_COL_GROUP), F32)
    s_scratch = pltpu.VMEM((3, TOK_BLOCK + 2 * GRID_W, FFN_COL_GROUP), F32)
    return pl.pallas_call(
        _ffn_up_kernel,
        grid=(nbn, N_TOK // tm),
        in_specs=[pl.BlockSpec((tm, D_MODEL), lambda j, i: (i, 0)),
                  pl.BlockSpec((None, D_MODEL, tn), lambda j, i: (layer, 0, j)),
                  pl.BlockSpec((None, D_MODEL, tn), lambda j, i: (layer, 0, nbn + j)),
                  pl.BlockSpec((None, 9, tn), lambda j, i: (layer, 0, j)),
                  pl.BlockSpec((None, 9, tn), lambda j, i: (layer, 0, nbn + j)),
                  pl.BlockSpec((None, 1, tn), lambda j, i: (layer, 0, j)),
                  pl.BlockSpec((None, 1, tn), lambda j, i: (layer, 0, nbn + j))],
        out_specs=pl.BlockSpec((tm, tn), lambda j, i: (i, j)),
        out_shape=jax.ShapeDtypeStruct((N_TOK, D_FF), BF16),
        scratch_shapes=[pltpu.VMEM((D_MODEL, tn), BF16), pltpu.VMEM((D_MODEL, tn), BF16),
                        u_scratch, u_scratch, s_scratch, s_scratch],
        compiler_params=_params(2),
        name="ffn_up",
    )(h, w_up, w_up, conv_w9, conv_w9, conv_b2, conv_b2)


def kernel(x_prompt, x_sample, state_hgrn, c, c_ctx, w_mod, b_mod, g_pre_mix, g_post_mix, g_pre_ffn,
           g_post_ffn, w_in, hgrn_lower_bounds, hgrn_norm, hy_conv_w, hy_conv_b, hy_w1, hy_b1, hy_w2,
           hy_b2, hy_w3, hy_decay, hy_bias, w_branch_a, w_branch_b, w_out, ffn_w_up, ffn_conv_w,
           ffn_conv_b, ffn_w_down):
    p_lb = jax.nn.softmax(hgrn_lower_bounds.astype(F32), axis=0)
    cs = jnp.cumsum(p_lb, axis=0)
    lbs = cs - cs[:1]

    cond = jnp.concatenate([jnp.broadcast_to(c_ctx[None, :], (N_CTX_BLOCKS, D_MODEL)), c], axis=0)
    mod4 = _modulation(cond, w_mod, b_mod).reshape(DEPTH, N_TOK_BLOCKS, 1, N_MOD)

    g_pre_mix, g_post_mix, g_pre_ffn, g_post_ffn = (
        g.reshape(DEPTH, 1, D_MODEL) for g in (g_pre_mix, g_post_mix, g_pre_ffn, g_post_ffn))

    h, x = _prenorm(x_prompt.reshape(N_CTX, D_MODEL), x_sample.reshape(N_LAT, D_MODEL), g_pre_mix,
                    mod4, 0)
    new_states = []
    for l in range(DEPTH):
        proj = _in_proj(h, w_in, l)
        oa_ctx, s_ctx = _hgrn_mixer(proj, lbs[l], hgrn_norm[l], None, l,
                                    row_block0=0, batch=BATCH, seq_len=SEQ)
        oa_lat, _ = _hgrn_mixer(proj, lbs[l], hgrn_norm[l], state_hgrn, l,
                                row_block0=N_CTX // DEC_SEQ, batch=DEC_BATCH, seq_len=DEC_SEQ)
        new_states.append(s_ctx)
        ob = []
        for row0, n_rows, seq_len in ((0, N_CTX, SEQ), (N_CTX, N_LAT, DEC_SEQ)):
            spectrum = _hyena_spectrum(seq_len, hy_w1[l], hy_b1[l], hy_w2[l], hy_b2[l], hy_w3[l],
                                       hy_decay[l])
            ob.append(_hyena_mixer(proj, hy_conv_w[l], hy_conv_b[l], spectrum, hy_bias[l],
                                   row0=row0, n_rows=n_rows, seq_len=seq_len))
        merged = _merge(oa_ctx, oa_lat, ob[0], ob[1], proj, w_branch_a, w_branch_b, l)
        x, h2 = _proj_residual(merged, w_out, x, g_post_mix, mod4, l, 2, (g_pre_ffn, l, 3, 4))
        g = _ffn_up(h2, ffn_w_up, ffn_conv_w, ffn_conv_b, l)
        if l + 1 < DEPTH:
            x, h = _proj_residual(g, ffn_w_down, x, g_post_ffn, mod4, l, 5, (g_pre_mix, l + 1, 0, 1))
        else:
            y_ctx, _ = _proj_residual(g, ffn_w_down, x, g_post_ffn, mod4, l, 5, None,
                                      blocks=(0, N_CTX_BLOCKS))
            y_lat, _ = _proj_residual(g, ffn_w_down, x, g_post_ffn, mod4, l, 5, None,
                                      blocks=(N_CTX_BLOCKS, N_TOK_BLOCKS - N_CTX_BLOCKS))

    y_prompt = y_ctx.reshape(BATCH, SEQ, D_MODEL)
    y_sample = y_lat.reshape(DEC_BATCH, DEC_SEQ, D_MODEL)
    new_state = jnp.stack(new_states, axis=1).astype(x_prompt.dtype)
    return (y_prompt, y_sample, new_state)
```

```python
import functools
import math

import numpy as np
import jax
import jax.numpy as jnp
from jax import lax
from jax.experimental import pallas as pl
from jax.experimental.pallas import tpu as pltpu

F32 = jnp.float32
BF16 = jnp.bfloat16

D_MODEL = 2048
BATCH = 16
SEQ = 256
DEPTH = 2
DEC_BATCH = 4
DEC_SEQ = 1024
GRID_W = 64
D_A = D_MODEL // 2
D_B = D_MODEL // 2
HGRN_EXPAND = 128
H_A = D_A // HGRN_EXPAND
DK_A = HGRN_EXPAND
DV_A = D_A // H_A
HY_ORDER = 2
HY_EMB = 33
HY_BANDS = (HY_EMB - 1) // 2
HY_HID = 64
HY_SIN_W = 1.0
D_FF = 5632
N_MOD = 6 * D_MODEL
N_PROJ = 5 * D_A + 3 * D_B + 2 * D_MODEL
EPS = 1e-6

N_CTX = BATCH * SEQ
N_LAT = DEC_BATCH * DEC_SEQ
N_TOK = N_CTX + N_LAT
TOK_BLOCK = 1024
N_TOK_BLOCKS = N_TOK // TOK_BLOCK
N_CTX_BLOCKS = N_CTX // TOK_BLOCK

COL_Q, COL_FA, COL_FB, COL_I, COL_G = 0, D_A, 2 * D_A, 3 * D_A, 4 * D_A
COL_HY = 5 * D_A
COL_GATE_A = 5 * D_A + 3 * D_B
COL_GATE_B = COL_GATE_A + D_MODEL

LANES = 128
SUBLANES = 8
FFN_ROW_TILE = 128
FFN_TOK_BLOCKS = 2
SCAN_CHUNK = 32
SCAN_UNROLL = 8
HY_CB = 256
HY_ROWS = 2048
FFN_COL_GROUP = 256
EPILOGUE_ROWS = 128
VMEM_LIMIT = 60 * 1024 * 1024


def _params(n_axes):
    return pltpu.CompilerParams(dimension_semantics=("arbitrary",) * n_axes,
                                vmem_limit_bytes=VMEM_LIMIT)


def _silu(x):
    return x * jax.nn.sigmoid(x)


def _rms(x):
    return x * lax.rsqrt(jnp.mean(x * x, axis=-1, keepdims=True) + EPS)


def _mod_spec(layer, chunk, tm, block0=0):
    return pl.BlockSpec((None, None, 1, D_MODEL),
                        lambda i, *_: (layer, ((block0 + i) * tm) // TOK_BLOCK, 0, chunk))


def _gain_spec(layer):
    return pl.BlockSpec((None, 1, D_MODEL), lambda *_: (layer, 0, 0))


def _mod_kernel(cond_ref, w_ref, b_ref, o_ref):
    s = _silu(cond_ref[...]).astype(BF16)
    o_ref[...] = jnp.dot(s, w_ref[...].astype(BF16), preferred_element_type=F32) + b_ref[...]


def _modulation(cond, w_mod, b_mod):
    tn = 1024
    return pl.pallas_call(
        _mod_kernel,
        grid=(DEPTH, N_MOD // tn),
        in_specs=[pl.BlockSpec((N_TOK_BLOCKS, D_MODEL), lambda l, j: (0, 0)),
                  pl.BlockSpec((None, D_MODEL, tn), lambda l, j: (l, 0, j)),
                  pl.BlockSpec((None, 1, tn), lambda l, j: (l, 0, j))],
        out_specs=pl.BlockSpec((None, N_TOK_BLOCKS, tn), lambda l, j: (l, 0, j)),
        out_shape=jax.ShapeDtypeStruct((DEPTH, N_TOK_BLOCKS, N_MOD), F32),
        compiler_params=_params(2),
        name="modulation",
    )(cond, w_mod, b_mod.reshape(DEPTH, 1, N_MOD))


def _prenorm_kernel(xc_ref, xl_ref, g_ref, sh_ref, sc_ref, h_ref, x_ref, *, n_ctx_blocks):
    x = jnp.where(pl.program_id(0) < n_ctx_blocks, xc_ref[...], xl_ref[...])
    x_ref[...] = x
    h = (_rms(x) * g_ref[...]) * (1.0 + sc_ref[...]) + sh_ref[...]
    h_ref[...] = h.astype(BF16)


def _prenorm(x_ctx, x_lat, g, mod4, layer):
    tm = 512
    nc = N_CTX // tm
    row_spec = pl.BlockSpec((tm, D_MODEL), lambda i: (i, 0))
    return pl.pallas_call(
        functools.partial(_prenorm_kernel, n_ctx_blocks=nc),
        grid=(N_TOK // tm,),
        in_specs=[pl.BlockSpec((tm, D_MODEL), lambda i: (jnp.minimum(i, nc - 1), 0)),
                  pl.BlockSpec((tm, D_MODEL), lambda i: (jnp.maximum(i - nc, 0), 0)),
                  _gain_spec(layer), _mod_spec(layer, 0, tm), _mod_spec(layer, 1, tm)],
        out_specs=[row_spec, row_spec],
        out_shape=[jax.ShapeDtypeStruct((N_TOK, D_MODEL), BF16),
                   jax.ShapeDtypeStruct((N_TOK, D_MODEL), F32)],
        compiler_params=_params(1),
        name="prenorm",
    )(x_ctx, x_lat, g, mod4, mod4)


def _inproj_kernel(h_ref, w_ref, o_ref, wbf_ref):
    @pl.when(pl.program_id(1) == 0)
    def _():
        wbf_ref[...] = w_ref[...].astype(BF16)

    o_ref[...] = jnp.dot(h_ref[...], wbf_ref[...], preferred_element_type=F32)


def _in_proj(h, w_in, layer):
    tm, tn = 1024, 1024
    return pl.pallas_call(
        _inproj_kernel,
        grid=(N_PROJ // tn, N_TOK // tm),
        in_specs=[pl.BlockSpec((tm, D_MODEL), lambda j, i: (i, 0)),
                  pl.BlockSpec((None, D_MODEL, tn), lambda j, i: (layer, 0, j))],
        out_specs=pl.BlockSpec((tm, tn), lambda j, i: (i, j)),
        out_shape=jax.ShapeDtypeStruct((N_TOK, N_PROJ), F32),
        scratch_shapes=[pltpu.VMEM((D_MODEL, tn), BF16)],
        compiler_params=_params(2),
        name="in_proj",
    )(h, w_in)


def _scan_levels(chunk):
    m, levels = chunk // 2, []
    while m >= 1:
        levels.append(m)
        m //= 2
    return tuple(levels)


@functools.lru_cache(maxsize=None)
def _scan_tables(chunk):
    levels = _scan_levels(chunk)
    amats, rowsels, pairs = [], [], []
    for rev in (False, True):
        blocks, sels = [], []
        for m in levels:
            a = np.zeros((chunk, chunk), np.float32)
            sel = np.zeros((chunk,), np.float32)
            for t in range(chunk):
                mid = 2 * m * (t // (2 * m)) + m - 1
                upper = t > mid
                if not rev:
                    if upper:
                        a[t, mid + 1:t + 1] = 1.0
                    else:
                        a[t, t + 1:mid + 1] = 1.0
                    sel[t] = 1.0 if upper else 0.0
                else:
                    if upper:
                        a[t, mid + 1:t] = 1.0
                    else:
                        a[t, t:mid + 1] = 1.0
                    sel[t] = 0.0 if upper else 1.0
            blocks.append(a)
            sels.append(np.repeat(sel[:, None], LANES, axis=1))
        idx = np.arange(chunk)
        if not rev:
            inter = (idx[None, :] <= idx[:, None]).astype(np.float32)
            carry = (idx[None, :] > idx[:, None]).astype(np.float32)
        else:
            inter = (idx[None, :] >= idx[:, None]).astype(np.float32)
            carry = (idx[None, :] < idx[:, None]).astype(np.float32)
        blocks += [inter, carry, np.ones((16, chunk), np.float32)]
        amats.append(np.concatenate(blocks, axis=0))
        rowsels.append(np.stack(sels))
        masks = []
        for m, sel in zip(levels, sels):
            same = idx[:, None] // (2 * m) == idx[None, :] // (2 * m)
            masks.append((same & (sel[:, :1] > 0.5) & (sel[:, :1].T < 0.5)).astype(np.float32))
        masks.append(np.eye(chunk, dtype=np.float32))
        pairs.append(np.stack(masks))
    return np.stack(amats), np.stack(rowsels), np.stack(pairs)


def _dot_exact_rhs(a_bf16, x):
    x1 = x.astype(BF16)
    x2 = (x - x1.astype(F32)).astype(BF16)
    n = x.shape[1]
    y = jnp.dot(a_bf16, jnp.concatenate([x1, x2], axis=1), preferred_element_type=F32)
    return y[:, :n] + y[:, n:]


_NT = (((1,), (1,)), ((), ()))
_TN = (((0,), (0,)), ((), ()))


def _hgrn_kernel(*refs, seq_len, chunk, has_s0):
    if has_s0:
        (q_ref, fa_ref, fb_ref, i_ref, g_ref, lb_ref, nw_ref, amat_ref, rowsel_ref, pair_ref, s0_ref,
         o_ref, sfin_ref, of_scr, ob_scr, st_scr) = refs
    else:
        (q_ref, fa_ref, fb_ref, i_ref, g_ref, lb_ref, nw_ref, amat_ref, rowsel_ref, pair_ref,
         o_ref, sfin_ref, of_scr, ob_scr, st_scr) = refs
    n_chunks = seq_len // chunk
    n_lev = len(_scan_levels(chunk))

    for d in range(2):
        if has_s0:
            st_scr[d] = s0_ref[d].T
        else:
            st_scr[d] = jnp.zeros((DV_A, DK_A), F32)

    f_refs = (fa_ref, fb_ref)
    o_scrs = (of_scr, ob_scr)

    def gates(d, rows):
        z = f_refs[d][rows, :]
        q = _silu(q_ref[rows, :])
        lbd = lb_ref[d:d + 1, :]
        e = jnp.exp(-jnp.abs(z))
        den = 1.0 + e
        a1 = jnp.log(lbd)
        a2 = jnp.log1p(-lbd) + (jnp.minimum(z, 0.0) - jnp.log(den))
        logf = jnp.maximum(a1, a2) + jnp.log(1.0 + jnp.exp(-jnp.abs(a1 - a2)))
        k = (1.0 - lbd) * (jnp.where(z >= 0.0, e, 1.0) / den)
        return q, k, logf

    def body(jj, carry):
        chains = []
        for d in range(2):
            for u in range(SCAN_UNROLL):
                j = jj * SCAN_UNROLL + u
                if d == 1:
                    j = n_chunks - 1 - j
                chains.append((d, pl.ds(pl.multiple_of(j * chunk, chunk), chunk)))
        qkf = [gates(d, rows) for d, rows in chains]
        ws = [jnp.exp(_dot_exact_rhs(amat_ref[d], logf))
              for (d, _), (_, _, logf) in zip(chains, qkf)]
        vbs = [i_ref[rows, :].astype(BF16) for _, rows in chains]
        atts, dsts = [], []
        for (d, _), (q, k, _), w, vb in zip(chains, qkf, ws, vbs):
            att = lax.dot_general(q.astype(BF16), k.astype(BF16), _NT,
                                  preferred_element_type=F32) * pair_ref[d, n_lev]
            for lv in range(n_lev):
                x = (jnp.where(rowsel_ref[d, lv] > 0.5, q, k)
                     * w[lv * chunk:(lv + 1) * chunk]).astype(BF16)
                att = att + lax.dot_general(x, x, _NT, preferred_element_type=F32) * pair_ref[d, lv]
            atts.append(att.astype(BF16))
            wk = w[(n_lev + 1) * chunk:(n_lev + 2) * chunk]
            dsts.append(lax.dot_general(vb, (k * wk).astype(BF16), _TN, preferred_element_type=F32))
        for d in range(2):
            st = st_scr[d]
            for u in range(SCAN_UNROLL):
                c = d * SCAN_UNROLL + u
                q, w = qkf[c][0], ws[c]
                wq = w[n_lev * chunk:(n_lev + 1) * chunk]
                wall = w[(n_lev + 2) * chunk:(n_lev + 2) * chunk + 1]
                o = lax.dot_general((q * wq).astype(BF16), st.astype(BF16), _NT,
                                    preferred_element_type=F32)
                o = o + jnp.dot(atts[c], vbs[c], preferred_element_type=F32)
                o_scrs[d][chains[c][1], :] = o
                st = st * wall + dsts[c]
            st_scr[d] = st
        return carry

    lax.fori_loop(0, n_chunks // SCAN_UNROLL, body, 0)

    o = of_scr[...] + ob_scr[...]
    y = _rms(o) * nw_ref[...]
    o_ref[...] = (y * _silu(g_ref[...])).astype(BF16)
    for d in range(2):
        sfin_ref[d] = st_scr[d].T


def _hgrn_mixer(proj, lb, norm_w, s0, layer, *, row_block0, batch, seq_len):
    chunk = SCAN_CHUNK
    amat, rowsel, pair = _scan_tables(chunk)
    amat = jnp.asarray(amat).astype(BF16)
    rowsel = jnp.asarray(rowsel)
    pair = jnp.asarray(pair)
    has_s0 = s0 is not None

    def col_spec(col):
        return pl.BlockSpec((seq_len, LANES), lambda b, h: (row_block0 + b, col // LANES + h))

    def const_spec(a):
        nd = a.ndim
        return pl.BlockSpec(a.shape, lambda b, h: (0,) * nd)

    in_specs = [col_spec(COL_Q), col_spec(COL_FA), col_spec(COL_FB), col_spec(COL_I), col_spec(COL_G),
                pl.BlockSpec((2, LANES), lambda b, h: (0, h)),
                pl.BlockSpec((1, LANES), lambda b, h: (0, 0)),
                const_spec(amat), const_spec(rowsel), const_spec(pair)]
    args = [proj, proj, proj, proj, proj, lb, norm_w.reshape(1, DV_A), amat, rowsel, pair]
    if has_s0:
        in_specs.append(pl.BlockSpec((None, None, 2, None, DK_A, DV_A),
                                     lambda b, h: (b, layer, 0, h, 0, 0)))
        args.append(s0)
    kern = functools.partial(_hgrn_kernel, seq_len=seq_len, chunk=chunk, has_s0=has_s0)
    return pl.pallas_call(
        kern,
        grid=(batch, H_A),
        in_specs=in_specs,
        out_specs=[pl.BlockSpec((seq_len, LANES), lambda b, h: (b, h)),
                   pl.BlockSpec((None, 2, None, DK_A, DV_A), lambda b, h: (b, 0, h, 0, 0))],
        out_shape=[jax.ShapeDtypeStruct((batch * seq_len, D_A), BF16),
                   jax.ShapeDtypeStruct((batch, 2, H_A, DK_A, DV_A), F32)],
        scratch_shapes=[pltpu.VMEM((seq_len, DV_A), F32), pltpu.VMEM((seq_len, DV_A), F32),
                        pltpu.VMEM((2, DV_A, DK_A), F32)],
        compiler_params=_params(2),
        name="hgrn_mixer",
    )(*args)


@functools.lru_cache(maxsize=None)
def _hyena_tables(seq_len):
    n = seq_len
    t = np.arange(n, dtype=np.float64)
    t01 = t / max(n - 1, 1)
    bands = np.linspace(1e-4, HY_BANDS - 1, HY_BANDS)
    ang = (2.0 * math.pi / n) * t[:, None] * bands[None, :]
    feats = np.concatenate([t01[:, None], np.cos(ang), -np.sin(ang)], axis=-1)
    feats = np.pad(feats, ((0, 0), (0, LANES - HY_EMB))).astype(np.float32)
    ft = np.outer(np.arange(n), np.arange(n)) % (2 * n)
    fc = np.cos(math.pi * ft / n)
    fs = -np.sin(math.pi * ft / n)
    fs[0, :] = 1.0 - 2.0 * (np.arange(n) % 2)
    return feats, fc.astype(np.float32), fs.astype(np.float32), fs.T.astype(np.float32).copy()


def _hyena_filter_kernel(feats_ref, w1_ref, b1_ref, w2_ref, b2_ref, w3_ref0, w3_ref1, w3_ref2, w3_ref3,
                         decay_ref, fc_ref, fs_ref, p_ref, h_scr, *, seq_len):
    n = seq_len
    hp = lax.Precision.HIGHEST

    @pl.when(pl.program_id(0) == 0)
    def _():
        h1 = jnp.sin(HY_SIN_W * (jnp.dot(feats_ref[...], w1_ref[...], precision=hp,
                                         preferred_element_type=F32) + b1_ref[...]))
        h_scr[...] = jnp.sin(HY_SIN_W * (jnp.dot(h1, w2_ref[...], precision=hp,
                                                 preferred_element_type=F32) + b2_ref[...]))

    h = h_scr[...]
    w3 = ((w3_ref0, w3_ref1), (w3_ref2, w3_ref3))
    cb = p_ref.shape[-1]
    row = lax.broadcasted_iota(jnp.int32, (n, cb), 0)
    t01 = row.astype(F32) / float(max(n - 1, 1))
    scale = 1.0 / (2.0 * n)
    wts = jnp.where(row == 0, scale, 2.0 * scale)
    sgn = (1 - 2 * (row & 1)).astype(F32)
    for o in range(HY_ORDER):
        win = jnp.exp(-t01 * jnp.abs(decay_ref[o:o + 1, :]))
        hf = jnp.dot(h, w3[o][0][...], precision=hp, preferred_element_type=F32) * win
        hb = jnp.dot(h, w3[o][1][...], precision=hp, preferred_element_type=F32) * win
        mag = jnp.where(row == 0, jnp.abs(hf + hb), jnp.abs(hf) + jnp.abs(hb))
        inv = 1.0 / (jnp.sum(mag, axis=0, keepdims=True) + EPS)
        even = (hf + hb) * inv
        odd = (hf - hb) * inv
        kre = jnp.dot(fc_ref[...], even.astype(BF16), preferred_element_type=F32)
        kim = jnp.dot(fs_ref[...], odd.astype(BF16), preferred_element_type=F32)
        knyq = jnp.sum(even * sgn, axis=0, keepdims=True)
        p_ref[3 * o + 0] = kre * wts
        p_ref[3 * o + 1] = jnp.where(row == 0, 0.0, kim * wts)
        p_ref[3 * o + 2] = jnp.where(row == 0, knyq * scale, kre * wts)


def _hyena_spectrum(seq_len, w1, b1, w2, b2, w3, decay):
    feats, fc, fs, _ = _hyena_tables(seq_len)
    feats = jnp.asarray(feats)
    mats = [jnp.asarray(m).astype(BF16) for m in (fc, fs)]
    w1p = jnp.pad(w1, ((0, LANES - HY_EMB), (0, 0)))
    cb = HY_CB
    nb = D_B // cb

    def w3_spec(k):
        return pl.BlockSpec((HY_HID, cb), lambda c: (0, k * nb + c))

    full2 = lambda a: pl.BlockSpec(a.shape, lambda c: (0, 0))
    b1r, b2r = b1.reshape(1, HY_HID), b2.reshape(1, HY_HID)
    return pl.pallas_call(
        functools.partial(_hyena_filter_kernel, seq_len=seq_len),
        grid=(nb,),
        in_specs=[full2(feats), full2(w1p), full2(b1r), full2(w2), full2(b2r),
                  w3_spec(0), w3_spec(1), w3_spec(2), w3_spec(3),
                  pl.BlockSpec((HY_ORDER, cb), lambda c: (0, c))] + [full2(m) for m in mats],
        out_specs=pl.BlockSpec((3 * HY_ORDER, seq_len, cb), lambda c: (0, 0, c)),
        out_shape=jax.ShapeDtypeStruct((3 * HY_ORDER, seq_len, D_B), F32),
        scratch_shapes=[pltpu.VMEM((seq_len, HY_HID), F32)],
        compiler_params=_params(1),
        name="hyena_filter",
    )(feats, w1p, b1r, w2, b2r, w3, w3, w3, w3, decay, *mats)


def _hyena_conv_kernel(v_ref, x1_ref, x2_ref, cwv_ref, cw1_ref, cw2_ref, cbv_ref, cb1_ref, cb2_ref,
                       p_ref, bias_ref, fc_ref, fs_ref, fst_ref, o_ref, *, seq_len):
    n = seq_len
    rows, cb = o_ref.shape
    nseq = rows // n
    pos = lax.broadcasted_iota(jnp.int32, (rows, cb), 0) & (n - 1)

    def dwconv(x_ref, w_ref, b_ref):
        x = x_ref[...]
        xm = jnp.where(pos == 0, 0.0, pltpu.roll(x, 1, axis=0))
        xp = jnp.where(pos == n - 1, 0.0, pltpu.roll(x, rows - 1, axis=0))
        y = xm * w_ref[0:1, :] + x * w_ref[1:2, :] + xp * w_ref[2:3, :] + b_ref[...]
        return jnp.concatenate([y[s * n:(s + 1) * n] for s in range(nseq)], axis=1)

    def tiled(a):
        return jnp.concatenate([a] * nseq, axis=1)

    def fftconv(z, o):
        zb16 = z.astype(BF16)
        zt = jnp.dot(fc_ref[...], zb16, preferred_element_type=F32)
        zb = jnp.dot(fs_ref[...], zb16, preferred_element_type=F32)
        p1, p3, p4 = tiled(p_ref[3 * o]), tiled(p_ref[3 * o + 1]), tiled(p_ref[3 * o + 2])
        yt = (zt * p1 - zb * p3).astype(BF16)
        yb = (zt * p3 + zb * p4).astype(BF16)
        y = (jnp.dot(fc_ref[...], yt, preferred_element_type=F32)
             + jnp.dot(fst_ref[...], yb, preferred_element_type=F32))
        return y + z * tiled(bias_ref[o:o + 1, :])

    v = dwconv(v_ref, cwv_ref, cbv_ref)
    x1 = dwconv(x1_ref, cw1_ref, cb1_ref)
    z2 = x1 * fftconv(v, 0)
    x2 = dwconv(x2_ref, cw2_ref, cb2_ref)
    out = x2 * fftconv(z2, 1)
    for s in range(nseq):
        o_ref[s * n:(s + 1) * n, :] = out[:, s * cb:(s + 1) * cb].astype(BF16)


def _hyena_mixer(proj, conv_w, conv_b, spectrum, bias, *, row0, n_rows, seq_len):
    assert seq_len & (seq_len - 1) == 0 and HY_ROWS % seq_len == 0 and row0 % HY_ROWS == 0
    _, fc, fs, fst = _hyena_tables(seq_len)
    mats = [jnp.asarray(m).astype(BF16) for m in (fc, fs, fst)]
    cb = HY_CB
    nb = D_B // cb
    rb0 = row0 // HY_ROWS

    def col_spec(k):
        return pl.BlockSpec((HY_ROWS, cb), lambda c, g: (rb0 + g, (COL_HY + k * D_B) // cb + c))

    def cw_spec(k):
        return pl.BlockSpec((3, cb), lambda c, g: (0, k * nb + c))

    def cb_spec(k):
        return pl.BlockSpec((1, cb), lambda c, g: (0, k * nb + c))

    mat_spec = pl.BlockSpec((seq_len, seq_len), lambda c, g: (0, 0))
    conv_b2 = conv_b.reshape(1, 3 * D_B)
    return pl.pallas_call(
        functools.partial(_hyena_conv_kernel, seq_len=seq_len),
        grid=(nb, n_rows // HY_ROWS),
        in_specs=[col_spec(0), col_spec(1), col_spec(2), cw_spec(0), cw_spec(1), cw_spec(2),
                  cb_spec(0), cb_spec(1), cb_spec(2),
                  pl.BlockSpec((3 * HY_ORDER, seq_len, cb), lambda c, g: (0, 0, c)),
                  pl.BlockSpec((HY_ORDER, cb), lambda c, g: (0, c))] + [mat_spec] * 3,
        out_specs=pl.BlockSpec((HY_ROWS, cb), lambda c, g: (g, c)),
        out_shape=jax.ShapeDtypeStruct((n_rows, D_B), BF16),
        compiler_params=_params(2),
        name="hyena_conv",
    )(proj, proj, proj, conv_w, conv_w, conv_w, conv_b2, conv_b2, conv_b2, spectrum, bias, *mats)


def _merge_kernel(oac_ref, oal_ref, obc_ref, obl_ref, ga_ref, gb_ref, wa_ref, wb_ref, o_ref,
                  wabf, wbbf):
    i = pl.program_id(1)

    @pl.when(i == 0)
    def _():
        wabf[...] = wa_ref[...].astype(BF16)
        wbbf[...] = wb_ref[...].astype(BF16)

    is_ctx = i < N_CTX_BLOCKS
    oa = jnp.where(is_ctx, oac_ref[...], oal_ref[...])
    ob = jnp.where(is_ctx, obc_ref[...], obl_ref[...])
    ya = jnp.dot(oa, wabf[...], preferred_element_type=F32)
    yb = jnp.dot(ob, wbbf[...], preferred_element_type=F32)
    o_ref[...] = (jax.nn.sigmoid(ga_ref[...]) * ya + jax.nn.sigmoid(gb_ref[...]) * yb).astype(BF16)


def _merge(oa_ctx, oa_lat, ob_ctx, ob_lat, proj, w_a, w_b, layer):
    tm, tn = TOK_BLOCK, 512
    nbn = D_MODEL // tn
    ctx_blk = lambda j, i: (jnp.minimum(i, N_CTX_BLOCKS - 1), 0)
    lat_blk = lambda j, i: (jnp.maximum(i - N_CTX_BLOCKS, 0), 0)
    return pl.pallas_call(
        _merge_kernel,
        grid=(nbn, N_TOK // tm),
        in_specs=[pl.BlockSpec((tm, D_A), ctx_blk), pl.BlockSpec((tm, D_A), lat_blk),
                  pl.BlockSpec((tm, D_B), ctx_blk), pl.BlockSpec((tm, D_B), lat_blk),
                  pl.BlockSpec((tm, tn), lambda j, i: (i, COL_GATE_A // tn + j)),
                  pl.BlockSpec((tm, tn), lambda j, i: (i, COL_GATE_B // tn + j)),
                  pl.BlockSpec((None, D_A, tn), lambda j, i: (layer, 0, j)),
                  pl.BlockSpec((None, D_B, tn), lambda j, i: (layer, 0, j))],
        out_specs=pl.BlockSpec((tm, tn), lambda j, i: (i, j)),
        out_shape=jax.ShapeDtypeStruct((N_TOK, D_MODEL), BF16),
        scratch_shapes=[pltpu.VMEM((D_A, tn), BF16), pltpu.VMEM((D_B, tn), BF16)],
        compiler_params=_params(2),
        name="merge",
    )(oa_ctx, oa_lat, ob_ctx, ob_lat, proj, proj, w_a, w_b)


def _resid_kernel(*refs, has_next, n_act):
    a_refs, refs = refs[:n_act], refs[n_act:]
    if has_next:
        (w_ref, x_ref, gpost_ref, gate_ref, gnext_ref, sh_ref, sc_ref, xo_ref, h_ref) = refs
    else:
        w_ref, x_ref, gpost_ref, gate_ref, xo_ref = refs
    k = pl.program_id(1)

    @pl.when(k == 0)
    def _():
        xo_ref[...] = jnp.zeros_like(xo_ref)

    part = w_ref.shape[0] // n_act
    y = None
    for p, a_ref in enumerate(a_refs):
        yp = jnp.dot(a_ref[...], w_ref[p * part:(p + 1) * part, :].astype(BF16),
                     preferred_element_type=F32)
        y = yp if y is None else y + yp
    xo_ref[...] += y

    @pl.when(k == pl.num_programs(1) - 1)
    def _():
        post_scale = gate_ref[...] * gpost_ref[...]
        next_scale = gnext_ref[...] * (1.0 + sc_ref[...]) if has_next else None

        def rows_step(r, carry):
            rows = pl.ds(pl.multiple_of(r * EPILOGUE_ROWS, EPILOGUE_ROWS), EPILOGUE_ROWS)
            xn = x_ref[rows, :] + _rms(xo_ref[rows, :]) * post_scale
            xo_ref[rows, :] = xn
            if has_next:
                h_ref[rows, :] = (_rms(xn) * next_scale + sh_ref[...]).astype(BF16)
            return carry

        lax.fori_loop(0, xo_ref.shape[0] // EPILOGUE_ROWS, rows_step, 0)


def _proj_residual(a, w, x, g_post, mod4, layer, gate_chunk, nxt, blocks=(0, N_TOK_BLOCKS)):
    tm, tk = TOK_BLOCK, 512
    acts = a if isinstance(a, tuple) else (a,)
    n_act = len(acts)
    kdim = sum(p.shape[1] for p in acts)
    has_next = nxt is not None
    b0, nblk = blocks
    in_row_spec = pl.BlockSpec((tm, D_MODEL), lambda i, k: (b0 + i, 0))
    row_spec = pl.BlockSpec((tm, D_MODEL), lambda i, k: (i, 0))
    in_specs = [pl.BlockSpec((tm, tk // n_act), lambda i, k: (b0 + i, k)) for _ in acts]
    in_specs += [pl.BlockSpec((None, tk, D_MODEL), lambda i, k: (layer, k, 0)),
                 in_row_spec,
                 _gain_spec(layer), _mod_spec(layer, gate_chunk, tm, b0)]
    args = [*acts, w, x, g_post, mod4]
    out_specs = [row_spec]
    out_shape = [jax.ShapeDtypeStruct((nblk * tm, D_MODEL), F32)]
    if has_next:
        g_next, nl, sh_chunk, sc_chunk = nxt
        in_specs += [_gain_spec(nl), _mod_spec(nl, sh_chunk, tm, b0), _mod_spec(nl, sc_chunk, tm, b0)]
        args += [g_next, mod4, mod4]
        out_specs.append(row_spec)
        out_shape.append(jax.ShapeDtypeStruct((nblk * tm, D_MODEL), BF16))
    res = pl.pallas_call(
        functools.partial(_resid_kernel, has_next=has_next, n_act=n_act),
        grid=(nblk, kdim // tk),
        in_specs=in_specs,
        out_specs=out_specs,
        out_shape=out_shape,
        compiler_params=_params(2),
        name="proj_residual",
    )(*args)
    return res if has_next else (res[0], None)


def _ffn_up_kernel(h_ref, wa_ref, wv_ref, cwa_ref, cwv_ref, cba_ref, cbv_ref, o_ref,
                   wabf, wvbf, ua_scr, uv_scr, sa_scr, sv_scr):
    j, i = pl.program_id(0), pl.program_id(1)
    n_sub, tn = o_ref.shape[0] // TOK_BLOCK, o_ref.shape[1]
    tm = TOK_BLOCK
    grp, rt, halo = FFN_COL_GROUP, FFN_ROW_TILE, SUBLANES

    @pl.when(i == 0)
    def _():
        wabf[...] = wa_ref[...].astype(BF16)
        wvbf[...] = wv_ref[...].astype(BF16)

    @pl.when((i == 0) & (j == 0))
    def _():
        for u_scr in (ua_scr, uv_scr):
            u_scr[0:halo, :] = jnp.zeros((halo, grp), F32)
            u_scr[halo + tm:, :] = jnp.zeros((halo, grp), F32)
        for s_scr in (sa_scr, sv_scr):
            s_scr[:, 0:GRID_W, :] = jnp.zeros((3, GRID_W, grp), F32)
            s_scr[:, GRID_W + tm:, :] = jnp.zeros((3, GRID_W, grp), F32)

    def tile_taps(u_scr, r0, row_len):
        piece = u_scr[r0:r0 + rt + 2 * halo, :]
        t = r0 + lax.broadcasted_iota(jnp.int32, (rt, LANES), 0)
        pos = t & (row_len - 1)
        keep_prev = jnp.concatenate([(pos != 0).astype(F32)] * (grp // LANES), axis=1)
        keep_next = jnp.concatenate([(pos != row_len - 1).astype(F32)] * (grp // LANES), axis=1)
        n = rt + 2 * halo
        um = pltpu.roll(piece, 1, axis=0)[halo:halo + rt] * keep_prev
        up = pltpu.roll(piece, n - 1, axis=0)[halo:halo + rt] * keep_next
        return um, piece[halo:halo + rt], up

    def kernel_row(taps, cw_ref, dr, cols):
        um, u, up = taps
        r = 3 * (dr + 1)
        return um * cw_ref[r:r + 1, cols] + u * cw_ref[r + 1:r + 2, cols] + up * cw_ref[r + 2:r + 3, cols]

    def project(dst, cols, h):
        dst[0][halo:halo + tm, :] = jnp.dot(h, wabf[:, cols], preferred_element_type=F32)
        dst[1][halo:halo + tm, :] = jnp.dot(h, wvbf[:, cols], preferred_element_type=F32)

    def conv_glu(src, params, cols, latent, row_off):
        ua_scr, uv_scr = src
        wa, wv, ba, bv = params
        if not latent:
            for r0 in range(0, tm, rt):
                a = kernel_row(tile_taps(ua_scr, r0, SEQ), wa, 0, cols) + ba[:, cols]
                vv = kernel_row(tile_taps(uv_scr, r0, SEQ), wv, 0, cols) + bv[:, cols]
                o_ref[row_off + r0:row_off + r0 + rt, cols] = (_silu(a) * vv).astype(BF16)
        else:
            for r0 in range(0, tm, rt):
                ta, tv = tile_taps(ua_scr, r0, GRID_W), tile_taps(uv_scr, r0, GRID_W)
                for dr in (-1, 0, 1):
                    sa_scr[dr + 1, GRID_W + r0:GRID_W + r0 + rt, :] = kernel_row(ta, wa, dr, cols)
                    sv_scr[dr + 1, GRID_W + r0:GRID_W + r0 + rt, :] = kernel_row(tv, wv, dr, cols)
            for r0 in range(0, tm, rt):
                def gather(s_scr, b_ref):
                    return (b_ref[:, cols] + s_scr[0, r0:r0 + rt, :]
                            + s_scr[1, GRID_W + r0:GRID_W + r0 + rt, :]
                            + s_scr[2, 2 * GRID_W + r0:2 * GRID_W + r0 + rt, :])
                a, vv = gather(sa_scr, ba), gather(sv_scr, bv)
                o_ref[row_off + r0:row_off + r0 + rt, cols] = (_silu(a) * vv).astype(BF16)

    def block(latent):
        for sub in range(n_sub):
            h = h_ref[sub * tm:(sub + 1) * tm, :]
            for c in range(tn // grp):
                cols = slice(c * grp, (c + 1) * grp)
                project((ua_scr, uv_scr), cols, h)
                conv_glu((ua_scr, uv_scr), (cwa_ref, cwv_ref, cba_ref, cbv_ref), cols, latent,
                         sub * tm)

    @pl.when(i * n_sub < N_CTX_BLOCKS)
    def _():
        block(False)

    @pl.when(i * n_sub >= N_CTX_BLOCKS)
    def _():
        block(True)


def _ffn_up(h, w_up, conv_w, conv_b, layer):
    assert N_CTX_BLOCKS % FFN_TOK_BLOCKS == 0
    tm, tn = FFN_TOK_BLOCKS * TOK_BLOCK, 512
    nbn = D_FF // tn
    conv_w9 = conv_w.reshape(DEPTH, 9, 2 * D_FF)
    conv_b2 = conv_b.reshape(DEPTH, 1, 2 * D_FF)
    u_scratch = pltpu.VMEM((TOK_BLOCK + 2 * SUBLANES, FFN_COL_GROUP), F32)
    s_scratch = pltpu.VMEM((3, TOK_BLOCK + 2 * GRID_W, FFN_COL_GROUP), F32)
    return pl.pallas_call(
        _ffn_up_kernel,
        grid=(nbn, N_TOK // tm),
        in_specs=[pl.BlockSpec((tm, D_MODEL), lambda j, i: (i, 0)),
                  pl.BlockSpec((None, D_MODEL, tn), lambda j, i: (layer, 0, j)),
                  pl.BlockSpec((None, D_MODEL, tn), lambda j, i: (layer, 0, nbn + j)),
                  pl.BlockSpec((None, 9, tn), lambda j, i: (layer, 0, j)),
                  pl.BlockSpec((None, 9, tn), lambda j, i: (layer, 0, nbn + j)),
                  pl.BlockSpec((None, 1, tn), lambda j, i: (layer, 0, j)),
                  pl.BlockSpec((None, 1, tn), lambda j, i: (layer, 0, nbn + j))],
        out_specs=pl.BlockSpec((tm, tn), lambda j, i: (i, j)),
        out_shape=jax.ShapeDtypeStruct((N_TOK, D_FF), BF16),
        scratch_shapes=[pltpu.VMEM((D_MODEL, tn), BF16), pltpu.VMEM((D_MODEL, tn), BF16),
                        u_scratch, u_scratch, s_scratch, s_scratch],
        compiler_params=_params(2),
        name="ffn_up",
    )(h, w_up, w_up, conv_w9, conv_w9, conv_b2, conv_b2)


def kernel(x_prompt, x_sample, state_hgrn, c, c_ctx, w_mod, b_mod, g_pre_mix, g_post_mix, g_pre_ffn,
           g_post_ffn, w_in, hgrn_lower_bounds, hgrn_norm, hy_conv_w, hy_conv_b, hy_w1, hy_b1, hy_w2,
           hy_b2, hy_w3, hy_decay, hy_bias, w_branch_a, w_branch_b, w_out, ffn_w_up, ffn_conv_w,
           ffn_conv_b, ffn_w_down):
    p_lb = jax.nn.softmax(hgrn_lower_bounds.astype(F32), axis=0)
    cs = jnp.cumsum(p_lb, axis=0)
    lbs = cs - cs[:1]

    cond = jnp.concatenate([jnp.broadcast_to(c_ctx[None, :], (N_CTX_BLOCKS, D_MODEL)), c], axis=0)
    mod4 = _modulation(cond, w_mod, b_mod).reshape(DEPTH, N_TOK_BLOCKS, 1, N_MOD)

    g_pre_mix, g_post_mix, g_pre_ffn, g_post_ffn = (
        g.reshape(DEPTH, 1, D_MODEL) for g in (g_pre_mix, g_post_mix, g_pre_ffn, g_post_ffn))

    h, x = _prenorm(x_prompt.reshape(N_CTX, D_MODEL), x_sample.reshape(N_LAT, D_MODEL), g_pre_mix,
                    mod4, 0)
    new_states = []
    for l in range(DEPTH):
        proj = _in_proj(h, w_in, l)
        oa_ctx, s_ctx = _hgrn_mixer(proj, lbs[l], hgrn_norm[l], None, l,
                                    row_block0=0, batch=BATCH, seq_len=SEQ)
        oa_lat, _ = _hgrn_mixer(proj, lbs[l], hgrn_norm[l], state_hgrn, l,
                                row_block0=N_CTX // DEC_SEQ, batch=DEC_BATCH, seq_len=DEC_SEQ)
        new_states.append(s_ctx)
        ob = []
        for row0, n_rows, seq_len in ((0, N_CTX, SEQ), (N_CTX, N_LAT, DEC_SEQ)):
            spectrum = _hyena_spectrum(seq_len, hy_w1[l], hy_b1[l], hy_w2[l], hy_b2[l], hy_w3[l],
                                       hy_decay[l])
            ob.append(_hyena_mixer(proj, hy_conv_w[l], hy_conv_b[l], spectrum, hy_bias[l],
                                   row0=row0, n_rows=n_rows, seq_len=seq_len))
        merged = _merge(oa_ctx, oa_lat, ob[0], ob[1], proj, w_branch_a, w_branch_b, l)
        x, h2 = _proj_residual(merged, w_out, x, g_post_mix, mod4, l, 2, (g_pre_ffn, l, 3, 4))
        g = _ffn_up(h2, ffn_w_up, ffn_conv_w, ffn_conv_b, l)
        if l + 1 < DEPTH:
            x, h = _proj_residual(g, ffn_w_down, x, g_post_ffn, mod4, l, 5, (g_pre_mix, l + 1, 0, 1))
        else:
            y_ctx, _ = _proj_residual(g, ffn_w_down, x, g_post_ffn, mod4, l, 5, None,
                                      blocks=(0, N_CTX_BLOCKS))
            y_lat, _ = _proj_residual(g, ffn_w_down, x, g_post_ffn, mod4, l, 5, None,
                                      blocks=(N_CTX_BLOCKS, N_TOK_BLOCKS - N_CTX_BLOCKS))

    y_prompt = y_ctx.reshape(BATCH, SEQ, D_MODEL)
    y_sample = y_lat.reshape(DEC_BATCH, DEC_SEQ, D_MODEL)
    new_state = jnp.stack(new_states, axis=1).astype(x_prompt.dtype)
    return (y_prompt, y_sample, new_state)
```

```python
import functools
import math

import numpy as np
import jax
import jax.numpy as jnp
from jax import lax
from jax.experimental import pallas as pl
from jax.experimental.pallas import tpu as pltpu

F32 = jnp.float32
BF16 = jnp.bfloat16

D_MODEL = 2048
BATCH = 16
SEQ = 256
DEPTH = 2
DEC_BATCH = 4
DEC_SEQ = 1024
GRID_W = 64
D_A = D_MODEL // 2
D_B = D_MODEL // 2
HGRN_EXPAND = 128
H_A = D_A // HGRN_EXPAND
DK_A = HGRN_EXPAND
DV_A = D_A // H_A
HY_ORDER = 2
HY_EMB = 33
HY_BANDS = (HY_EMB - 1) // 2
HY_HID = 64
HY_SIN_W = 1.0
D_FF = 5632
N_MOD = 6 * D_MODEL
N_PROJ = 5 * D_A + 3 * D_B + 2 * D_MODEL
EPS = 1e-6

N_CTX = BATCH * SEQ
N_LAT = DEC_BATCH * DEC_SEQ
N_TOK = N_CTX + N_LAT
TOK_BLOCK = 1024
N_TOK_BLOCKS = N_TOK // TOK_BLOCK
N_CTX_BLOCKS = N_CTX // TOK_BLOCK

COL_Q, COL_FA, COL_FB, COL_I, COL_G = 0, D_A, 2 * D_A, 3 * D_A, 4 * D_A
COL_HY = 5 * D_A
COL_GATE_A = 5 * D_A + 3 * D_B
COL_GATE_B = COL_GATE_A + D_MODEL

LANES = 128
SUBLANES = 8
FFN_ROW_TILE = 128
FFN_TOK_BLOCKS = 2
SCAN_CHUNK = 32
SCAN_UNROLL = 16
HY_CB = 256
HY_ROWS = 2048
FFN_COL_GROUP = 256
EPILOGUE_ROWS = 128
VMEM_LIMIT = 60 * 1024 * 1024


def _params(n_axes):
    return pltpu.CompilerParams(dimension_semantics=("arbitrary",) * n_axes,
                                vmem_limit_bytes=VMEM_LIMIT)


def _silu(x):
    return x * jax.nn.sigmoid(x)


def _rms(x):
    return x * lax.rsqrt(jnp.mean(x * x, axis=-1, keepdims=True) + EPS)


def _mod_spec(layer, chunk, tm, block0=0):
    return pl.BlockSpec((None, None, 1, D_MODEL),
                        lambda i, *_: (layer, ((block0 + i) * tm) // TOK_BLOCK, 0, chunk))


def _gain_spec(layer):
    return pl.BlockSpec((None, 1, D_MODEL), lambda *_: (layer, 0, 0))


def _mod_kernel(cond_ref, w_ref, b_ref, o_ref):
    s = _silu(cond_ref[...]).astype(BF16)
    o_ref[...] = jnp.dot(s, w_ref[...].astype(BF16), preferred_element_type=F32) + b_ref[...]


def _modulation(cond, w_mod, b_mod):
    tn = 1024
    return pl.pallas_call(
        _mod_kernel,
        grid=(DEPTH, N_MOD // tn),
        in_specs=[pl.BlockSpec((N_TOK_BLOCKS, D_MODEL), lambda l, j: (0, 0)),
                  pl.BlockSpec((None, D_MODEL, tn), lambda l, j: (l, 0, j)),
                  pl.BlockSpec((None, 1, tn), lambda l, j: (l, 0, j))],
        out_specs=pl.BlockSpec((None, N_TOK_BLOCKS, tn), lambda l, j: (l, 0, j)),
        out_shape=jax.ShapeDtypeStruct((DEPTH, N_TOK_BLOCKS, N_MOD), F32),
        compiler_params=_params(2),
        name="modulation",
    )(cond, w_mod, b_mod.reshape(DEPTH, 1, N_MOD))


def _prenorm_kernel(xc_ref, xl_ref, g_ref, sh_ref, sc_ref, h_ref, x_ref, *, n_ctx_blocks):
    x = jnp.where(pl.program_id(0) < n_ctx_blocks, xc_ref[...], xl_ref[...])
    x_ref[...] = x
    h = (_rms(x) * g_ref[...]) * (1.0 + sc_ref[...]) + sh_ref[...]
    h_ref[...] = h.astype(BF16)


def _prenorm(x_ctx, x_lat, g, mod4, layer):
    tm = 512
    nc = N_CTX // tm
    row_spec = pl.BlockSpec((tm, D_MODEL), lambda i: (i, 0))
    return pl.pallas_call(
        functools.partial(_prenorm_kernel, n_ctx_blocks=nc),
        grid=(N_TOK // tm,),
        in_specs=[pl.BlockSpec((tm, D_MODEL), lambda i: (jnp.minimum(i, nc - 1), 0)),
                  pl.BlockSpec((tm, D_MODEL), lambda i: (jnp.maximum(i - nc, 0), 0)),
                  _gain_spec(layer), _mod_spec(layer, 0, tm), _mod_spec(layer, 1, tm)],
        out_specs=[row_spec, row_spec],
        out_shape=[jax.ShapeDtypeStruct((N_TOK, D_MODEL), BF16),
                   jax.ShapeDtypeStruct((N_TOK, D_MODEL), F32)],
        compiler_params=_params(1),
        name="prenorm",
    )(x_ctx, x_lat, g, mod4, mod4)


def _inproj_kernel(h_ref, w_ref, o_ref, wbf_ref):
    @pl.when(pl.program_id(1) == 0)
    def _():
        wbf_ref[...] = w_ref[...].astype(BF16)

    o_ref[...] = jnp.dot(h_ref[...], wbf_ref[...], preferred_element_type=F32)


def _in_proj(h, w_in, layer):
    tm, tn = 1024, 1024
    return pl.pallas_call(
        _inproj_kernel,
        grid=(N_PROJ // tn, N_TOK // tm),
        in_specs=[pl.BlockSpec((tm, D_MODEL), lambda j, i: (i, 0)),
                  pl.BlockSpec((None, D_MODEL, tn), lambda j, i: (layer, 0, j))],
        out_specs=pl.BlockSpec((tm, tn), lambda j, i: (i, j)),
        out_shape=jax.ShapeDtypeStruct((N_TOK, N_PROJ), F32),
        scratch_shapes=[pltpu.VMEM((D_MODEL, tn), BF16)],
        compiler_params=_params(2),
        name="in_proj",
    )(h, w_in)


def _scan_levels(chunk):
    m, levels = chunk // 2, []
    while m >= 1:
        levels.append(m)
        m //= 2
    return tuple(levels)


@functools.lru_cache(maxsize=None)
def _scan_tables(chunk):
    levels = _scan_levels(chunk)
    amats, rowsels, pairs = [], [], []
    for rev in (False, True):
        blocks, sels = [], []
        for m in levels:
            a = np.zeros((chunk, chunk), np.float32)
            sel = np.zeros((chunk,), np.float32)
            for t in range(chunk):
                mid = 2 * m * (t // (2 * m)) + m - 1
                upper = t > mid
                if not rev:
                    if upper:
                        a[t, mid + 1:t + 1] = 1.0
                    else:
                        a[t, t + 1:mid + 1] = 1.0
                    sel[t] = 1.0 if upper else 0.0
                else:
                    if upper:
                        a[t, mid + 1:t] = 1.0
                    else:
                        a[t, t:mid + 1] = 1.0
                    sel[t] = 0.0 if upper else 1.0
            blocks.append(a)
            sels.append(np.repeat(sel[:, None], LANES, axis=1))
        idx = np.arange(chunk)
        if not rev:
            inter = (idx[None, :] <= idx[:, None]).astype(np.float32)
            carry = (idx[None, :] > idx[:, None]).astype(np.float32)
        else:
            inter = (idx[None, :] >= idx[:, None]).astype(np.float32)
            carry = (idx[None, :] < idx[:, None]).astype(np.float32)
        blocks += [inter, carry, np.ones((16, chunk), np.float32)]
        amats.append(np.concatenate(blocks, axis=0))
        rowsels.append(np.stack(sels))
        masks = []
        for m, sel in zip(levels, sels):
            same = idx[:, None] // (2 * m) == idx[None, :] // (2 * m)
            masks.append((same & (sel[:, :1] > 0.5) & (sel[:, :1].T < 0.5)).astype(np.float32))
        masks.append(np.eye(chunk, dtype=np.float32))
        pairs.append(np.stack(masks))
    return np.stack(amats), np.stack(rowsels), np.stack(pairs)


def _dot_exact_rhs(a_bf16, x):
    x1 = x.astype(BF16)
    x2 = (x - x1.astype(F32)).astype(BF16)
    n = x.shape[1]
    y = jnp.dot(a_bf16, jnp.concatenate([x1, x2], axis=1), preferred_element_type=F32)
    return y[:, :n] + y[:, n:]


_NT = (((1,), (1,)), ((), ()))
_TN = (((0,), (0,)), ((), ()))


def _hgrn_kernel(*refs, seq_len, chunk, has_s0):
    if has_s0:
        (q_ref, fa_ref, fb_ref, i_ref, g_ref, lb_ref, nw_ref, amat_ref, rowsel_ref, pair_ref, s0_ref,
         o_ref, sfin_ref, of_scr, ob_scr, st_scr) = refs
    else:
        (q_ref, fa_ref, fb_ref, i_ref, g_ref, lb_ref, nw_ref, amat_ref, rowsel_ref, pair_ref,
         o_ref, sfin_ref, of_scr, ob_scr, st_scr) = refs
    n_chunks = seq_len // chunk
    n_lev = len(_scan_levels(chunk))
    unroll = min(SCAN_UNROLL, n_chunks)

    for d in range(2):
        if has_s0:
            st_scr[d] = s0_ref[d].T
        else:
            st_scr[d] = jnp.zeros((DV_A, DK_A), F32)

    f_refs = (fa_ref, fb_ref)
    o_scrs = (of_scr, ob_scr)

    def gates(d, rows):
        z = f_refs[d][rows, :]
        q = _silu(q_ref[rows, :])
        lbd = lb_ref[d:d + 1, :]
        e = jnp.exp(-jnp.abs(z))
        den = 1.0 + e
        a1 = jnp.log(lbd)
        a2 = jnp.log1p(-lbd) + (jnp.minimum(z, 0.0) - jnp.log(den))
        logf = jnp.maximum(a1, a2) + jnp.log(1.0 + jnp.exp(-jnp.abs(a1 - a2)))
        k = (1.0 - lbd) * (jnp.where(z >= 0.0, e, 1.0) / den)
        return q, k, logf

    def body(jj, carry):
        chains = []
        for d in range(2):
            for u in range(unroll):
                j = jj * unroll + u
                if d == 1:
                    j = n_chunks - 1 - j
                chains.append((d, pl.ds(pl.multiple_of(j * chunk, chunk), chunk)))
        qkf = [gates(d, rows) for d, rows in chains]
        ws = [jnp.exp(_dot_exact_rhs(amat_ref[d], logf))
              for (d, _), (_, _, logf) in zip(chains, qkf)]
        vbs = [i_ref[rows, :].astype(BF16) for _, rows in chains]
        atts, dsts = [], []
        for (d, _), (q, k, _), w, vb in zip(chains, qkf, ws, vbs):
            att = lax.dot_general(q.astype(BF16), k.astype(BF16), _NT,
                                  preferred_element_type=F32) * pair_ref[d, n_lev]
            for lv in range(n_lev):
                x = (jnp.where(rowsel_ref[d, lv] > 0.5, q, k)
                     * w[lv * chunk:(lv + 1) * chunk]).astype(BF16)
                att = att + lax.dot_general(x, x, _NT, preferred_element_type=F32) * pair_ref[d, lv]
            atts.append(att.astype(BF16))
            wk = w[(n_lev + 1) * chunk:(n_lev + 2) * chunk]
            dsts.append(lax.dot_general(vb, (k * wk).astype(BF16), _TN, preferred_element_type=F32))
        for d in range(2):
            st = st_scr[d]
            for u in range(unroll):
                c = d * unroll + u
                q, w = qkf[c][0], ws[c]
                wq = w[n_lev * chunk:(n_lev + 1) * chunk]
                wall = w[(n_lev + 2) * chunk:(n_lev + 2) * chunk + 1]
                o = lax.dot_general((q * wq).astype(BF16), st.astype(BF16), _NT,
                                    preferred_element_type=F32)
                o = o + jnp.dot(atts[c], vbs[c], preferred_element_type=F32)
                o_scrs[d][chains[c][1], :] = o
                st = st * wall + dsts[c]
            st_scr[d] = st
        return carry

    lax.fori_loop(0, n_chunks // unroll, body, 0)

    o = of_scr[...] + ob_scr[...]
    y = _rms(o) * nw_ref[...]
    o_ref[...] = (y * _silu(g_ref[...])).astype(BF16)
    for d in range(2):
        sfin_ref[d] = st_scr[d].T


def _hgrn_mixer(proj, lb, norm_w, s0, layer, *, row_block0, batch, seq_len):
    chunk = SCAN_CHUNK
    amat, rowsel, pair = _scan_tables(chunk)
    amat = jnp.asarray(amat).astype(BF16)
    rowsel = jnp.asarray(rowsel)
    pair = jnp.asarray(pair)
    has_s0 = s0 is not None

    def col_spec(col):
        return pl.BlockSpec((seq_len, LANES), lambda b, h: (row_block0 + b, col // LANES + h))

    def const_spec(a):
        nd = a.ndim
        return pl.BlockSpec(a.shape, lambda b, h: (0,) * nd)

    in_specs = [col_spec(COL_Q), col_spec(COL_FA), col_spec(COL_FB), col_spec(COL_I), col_spec(COL_G),
                pl.BlockSpec((2, LANES), lambda b, h: (0, h)),
                pl.BlockSpec((1, LANES), lambda b, h: (0, 0)),
                const_spec(amat), const_spec(rowsel), const_spec(pair)]
    args = [proj, proj, proj, proj, proj, lb, norm_w.reshape(1, DV_A), amat, rowsel, pair]
    if has_s0:
        in_specs.append(pl.BlockSpec((None, None, 2, None, DK_A, DV_A),
                                     lambda b, h: (b, layer, 0, h, 0, 0)))
        args.append(s0)
    kern = functools.partial(_hgrn_kernel, seq_len=seq_len, chunk=chunk, has_s0=has_s0)
    return pl.pallas_call(
        kern,
        grid=(batch, H_A),
        in_specs=in_specs,
        out_specs=[pl.BlockSpec((seq_len, LANES), lambda b, h: (b, h)),
                   pl.BlockSpec((None, 2, None, DK_A, DV_A), lambda b, h: (b, 0, h, 0, 0))],
        out_shape=[jax.ShapeDtypeStruct((batch * seq_len, D_A), BF16),
                   jax.ShapeDtypeStruct((batch, 2, H_A, DK_A, DV_A), F32)],
        scratch_shapes=[pltpu.VMEM((seq_len, DV_A), F32), pltpu.VMEM((seq_len, DV_A), F32),
                        pltpu.VMEM((2, DV_A, DK_A), F32)],
        compiler_params=_params(2),
        name="hgrn_mixer",
    )(*args)


@functools.lru_cache(maxsize=None)
def _hyena_tables(seq_len):
    n = seq_len
    t = np.arange(n, dtype=np.float64)
    t01 = t / max(n - 1, 1)
    bands = np.linspace(1e-4, HY_BANDS - 1, HY_BANDS)
    ang = (2.0 * math.pi / n) * t[:, None] * bands[None, :]
    feats = np.concatenate([t01[:, None], np.cos(ang), -np.sin(ang)], axis=-1)
    feats = np.pad(feats, ((0, 0), (0, LANES - HY_EMB))).astype(np.float32)
    ft = np.outer(np.arange(n), np.arange(n)) % (2 * n)
    fc = np.cos(math.pi * ft / n)
    fs = -np.sin(math.pi * ft / n)
    fs[0, :] = 1.0 - 2.0 * (np.arange(n) % 2)
    return feats, fc.astype(np.float32), fs.astype(np.float32), fs.T.astype(np.float32).copy()


def _dot_split(a, b):
    ah = a.astype(BF16)
    al = (a - ah.astype(F32)).astype(BF16)
    bh = b.astype(BF16)
    bl = (b - bh.astype(F32)).astype(BF16)
    dot = functools.partial(jnp.dot, preferred_element_type=F32)
    return dot(ah, bh) + dot(ah, bl) + dot(al, bh)


def _hyena_filter_kernel(feats_ref, w1_ref, b1_ref, w2_ref, b2_ref, w3_ref0, w3_ref1, w3_ref2, w3_ref3,
                         decay_ref, fc_ref, fs_ref, p_ref, h_scr, *, seq_len):
    n = seq_len
    hp = lax.Precision.HIGHEST

    @pl.when(pl.program_id(0) == 0)
    def _():
        h1 = jnp.sin(HY_SIN_W * (jnp.dot(feats_ref[...], w1_ref[...], precision=hp,
                                         preferred_element_type=F32) + b1_ref[...]))
        h_scr[...] = jnp.sin(HY_SIN_W * (jnp.dot(h1, w2_ref[...], precision=hp,
                                                 preferred_element_type=F32) + b2_ref[...]))

    h = h_scr[...]
    w3 = ((w3_ref0, w3_ref1), (w3_ref2, w3_ref3))
    cb = p_ref.shape[-1]
    row = lax.broadcasted_iota(jnp.int32, (n, cb), 0)
    t01 = row.astype(F32) / float(max(n - 1, 1))
    scale = 1.0 / (2.0 * n)
    wts = jnp.where(row == 0, scale, 2.0 * scale)
    sgn = (1 - 2 * (row & 1)).astype(F32)
    for o in range(HY_ORDER):
        win = jnp.exp(-t01 * jnp.abs(decay_ref[o:o + 1, :]))
        hf = _dot_split(h, w3[o][0][...]) * win
        hb = _dot_split(h, w3[o][1][...]) * win
        mag = jnp.where(row == 0, jnp.abs(hf + hb), jnp.abs(hf) + jnp.abs(hb))
        inv = 1.0 / (jnp.sum(mag, axis=0, keepdims=True) + EPS)
        even = (hf + hb) * inv
        odd = (hf - hb) * inv
        kre = jnp.dot(fc_ref[...], even.astype(BF16), preferred_element_type=F32)
        kim = jnp.dot(fs_ref[...], odd.astype(BF16), preferred_element_type=F32)
        knyq = jnp.sum(even * sgn, axis=0, keepdims=True)
        p_ref[3 * o + 0] = kre * wts
        p_ref[3 * o + 1] = jnp.where(row == 0, 0.0, kim * wts)
        p_ref[3 * o + 2] = jnp.where(row == 0, knyq * scale, kre * wts)


def _hyena_spectrum(seq_len, w1, b1, w2, b2, w3, decay):
    feats, fc, fs, _ = _hyena_tables(seq_len)
    feats = jnp.asarray(feats)
    mats = [jnp.asarray(m).astype(BF16) for m in (fc, fs)]
    w1p = jnp.pad(w1, ((0, LANES - HY_EMB), (0, 0)))
    cb = HY_CB
    nb = D_B // cb

    def w3_spec(k):
        return pl.BlockSpec((HY_HID, cb), lambda c: (0, k * nb + c))

    full2 = lambda a: pl.BlockSpec(a.shape, lambda c: (0, 0))
    b1r, b2r = b1.reshape(1, HY_HID), b2.reshape(1, HY_HID)
    return pl.pallas_call(
        functools.partial(_hyena_filter_kernel, seq_len=seq_len),
        grid=(nb,),
        in_specs=[full2(feats), full2(w1p), full2(b1r), full2(w2), full2(b2r),
                  w3_spec(0), w3_spec(1), w3_spec(2), w3_spec(3),
                  pl.BlockSpec((HY_ORDER, cb), lambda c: (0, c))] + [full2(m) for m in mats],
        out_specs=pl.BlockSpec((3 * HY_ORDER, seq_len, cb), lambda c: (0, 0, c)),
        out_shape=jax.ShapeDtypeStruct((3 * HY_ORDER, seq_len, D_B), F32),
        scratch_shapes=[pltpu.VMEM((seq_len, HY_HID), F32)],
        compiler_params=_params(1),
        name="hyena_filter",
    )(feats, w1p, b1r, w2, b2r, w3, w3, w3, w3, decay, *mats)


def _hyena_conv_kernel(v_ref, x1_ref, x2_ref, cwv_ref, cw1_ref, cw2_ref, cbv_ref, cb1_ref, cb2_ref,
                       p_ref, bias_ref, fc_ref, fs_ref, fst_ref, o_ref, *, seq_len):
    n = seq_len
    rows, cb = o_ref.shape
    nseq = rows // n
    pos = lax.broadcasted_iota(jnp.int32, (rows, cb), 0) & (n - 1)

    def dwconv(x_ref, w_ref, b_ref):
        x = x_ref[...]
        xm = jnp.where(pos == 0, 0.0, pltpu.roll(x, 1, axis=0))
        xp = jnp.where(pos == n - 1, 0.0, pltpu.roll(x, rows - 1, axis=0))
        y = xm * w_ref[0:1, :] + x * w_ref[1:2, :] + xp * w_ref[2:3, :] + b_ref[...]
        return jnp.concatenate([y[s * n:(s + 1) * n] for s in range(nseq)], axis=1)

    def tiled(a):
        return jnp.concatenate([a] * nseq, axis=1)

    def fftconv(z, o):
        zb16 = z.astype(BF16)
        zt = jnp.dot(fc_ref[...], zb16, preferred_element_type=F32)
        zb = jnp.dot(fs_ref[...], zb16, preferred_element_type=F32)
        p1, p3, p4 = tiled(p_ref[3 * o]), tiled(p_ref[3 * o + 1]), tiled(p_ref[3 * o + 2])
        yt = (zt * p1 - zb * p3).astype(BF16)
        yb = (zt * p3 + zb * p4).astype(BF16)
        y = (jnp.dot(fc_ref[...], yt, preferred_element_type=F32)
             + jnp.dot(fst_ref[...], yb, preferred_element_type=F32))
        return y + z * tiled(bias_ref[o:o + 1, :])

    v = dwconv(v_ref, cwv_ref, cbv_ref)
    x1 = dwconv(x1_ref, cw1_ref, cb1_ref)
    z2 = x1 * fftconv(v, 0)
    x2 = dwconv(x2_ref, cw2_ref, cb2_ref)
    out = x2 * fftconv(z2, 1)
    for s in range(nseq):
        o_ref[s * n:(s + 1) * n, :] = out[:, s * cb:(s + 1) * cb].astype(BF16)


def _hyena_mixer(proj, conv_w, conv_b, spectrum, bias, *, row0, n_rows, seq_len):
    assert seq_len & (seq_len - 1) == 0 and HY_ROWS % seq_len == 0 and row0 % HY_ROWS == 0
    _, fc, fs, fst = _hyena_tables(seq_len)
    mats = [jnp.asarray(m).astype(BF16) for m in (fc, fs, fst)]
    cb = HY_CB
    nb = D_B // cb
    rb0 = row0 // HY_ROWS

    def col_spec(k):
        return pl.BlockSpec((HY_ROWS, cb), lambda c, g: (rb0 + g, (COL_HY + k * D_B) // cb + c))

    def cw_spec(k):
        return pl.BlockSpec((3, cb), lambda c, g: (0, k * nb + c))

    def cb_spec(k):
        return pl.BlockSpec((1, cb), lambda c, g: (0, k * nb + c))

    mat_spec = pl.BlockSpec((seq_len, seq_len), lambda c, g: (0, 0))
    conv_b2 = conv_b.reshape(1, 3 * D_B)
    return pl.pallas_call(
        functools.partial(_hyena_conv_kernel, seq_len=seq_len),
        grid=(nb, n_rows // HY_ROWS),
        in_specs=[col_spec(0), col_spec(1), col_spec(2), cw_spec(0), cw_spec(1), cw_spec(2),
                  cb_spec(0), cb_spec(1), cb_spec(2),
                  pl.BlockSpec((3 * HY_ORDER, seq_len, cb), lambda c, g: (0, 0, c)),
                  pl.BlockSpec((HY_ORDER, cb), lambda c, g: (0, c))] + [mat_spec] * 3,
        out_specs=pl.BlockSpec((HY_ROWS, cb), lambda c, g: (g, c)),
        out_shape=jax.ShapeDtypeStruct((n_rows, D_B), BF16),
        compiler_params=_params(2),
        name="hyena_conv",
    )(proj, proj, proj, conv_w, conv_w, conv_w, conv_b2, conv_b2, conv_b2, spectrum, bias, *mats)


def _merge_kernel(oac_ref, oal_ref, obc_ref, obl_ref, ga_ref, gb_ref, wa_ref, wb_ref, o_ref,
                  wabf, wbbf):
    i = pl.program_id(1)

    @pl.when(i == 0)
    def _():
        wabf[...] = wa_ref[...].astype(BF16)
        wbbf[...] = wb_ref[...].astype(BF16)

    is_ctx = i < N_CTX_BLOCKS
    oa = jnp.where(is_ctx, oac_ref[...], oal_ref[...])
    ob = jnp.where(is_ctx, obc_ref[...], obl_ref[...])
    ya = jnp.dot(oa, wabf[...], preferred_element_type=F32)
    yb = jnp.dot(ob, wbbf[...], preferred_element_type=F32)
    o_ref[...] = (jax.nn.sigmoid(ga_ref[...]) * ya + jax.nn.sigmoid(gb_ref[...]) * yb).astype(BF16)


def _merge(oa_ctx, oa_lat, ob_ctx, ob_lat, proj, w_a, w_b, layer):
    tm, tn = TOK_BLOCK, 512
    nbn = D_MODEL // tn
    ctx_blk = lambda j, i: (jnp.minimum(i, N_CTX_BLOCKS - 1), 0)
    lat_blk = lambda j, i: (jnp.maximum(i - N_CTX_BLOCKS, 0), 0)
    return pl.pallas_call(
        _merge_kernel,
        grid=(nbn, N_TOK // tm),
        in_specs=[pl.BlockSpec((tm, D_A), ctx_blk), pl.BlockSpec((tm, D_A), lat_blk),
                  pl.BlockSpec((tm, D_B), ctx_blk), pl.BlockSpec((tm, D_B), lat_blk),
                  pl.BlockSpec((tm, tn), lambda j, i: (i, COL_GATE_A // tn + j)),
                  pl.BlockSpec((tm, tn), lambda j, i: (i, COL_GATE_B // tn + j)),
                  pl.BlockSpec((None, D_A, tn), lambda j, i: (layer, 0, j)),
                  pl.BlockSpec((None, D_B, tn), lambda j, i: (layer, 0, j))],
        out_specs=pl.BlockSpec((tm, tn), lambda j, i: (i, j)),
        out_shape=jax.ShapeDtypeStruct((N_TOK, D_MODEL), BF16),
        scratch_shapes=[pltpu.VMEM((D_A, tn), BF16), pltpu.VMEM((D_B, tn), BF16)],
        compiler_params=_params(2),
        name="merge",
    )(oa_ctx, oa_lat, ob_ctx, ob_lat, proj, proj, w_a, w_b)


def _resid_kernel(*refs, has_next, n_act):
    a_refs, refs = refs[:n_act], refs[n_act:]
    if has_next:
        (w_ref, x_ref, gpost_ref, gate_ref, gnext_ref, sh_ref, sc_ref, xo_ref, h_ref) = refs
    else:
        w_ref, x_ref, gpost_ref, gate_ref, xo_ref = refs
    k = pl.program_id(1)

    @pl.when(k == 0)
    def _():
        xo_ref[...] = jnp.zeros_like(xo_ref)

    part = w_ref.shape[0] // n_act
    y = None
    for p, a_ref in enumerate(a_refs):
        yp = jnp.dot(a_ref[...], w_ref[p * part:(p + 1) * part, :].astype(BF16),
                     preferred_element_type=F32)
        y = yp if y is None else y + yp
    xo_ref[...] += y

    @pl.when(k == pl.num_programs(1) - 1)
    def _():
        post_scale = gate_ref[...] * gpost_ref[...]
        next_scale = gnext_ref[...] * (1.0 + sc_ref[...]) if has_next else None

        def rows_step(r, carry):
            rows = pl.ds(pl.multiple_of(r * EPILOGUE_ROWS, EPILOGUE_ROWS), EPILOGUE_ROWS)
            xn = x_ref[rows, :] + _rms(xo_ref[rows, :]) * post_scale
            xo_ref[rows, :] = xn
            if has_next:
                h_ref[rows, :] = (_rms(xn) * next_scale + sh_ref[...]).astype(BF16)
            return carry

        lax.fori_loop(0, xo_ref.shape[0] // EPILOGUE_ROWS, rows_step, 0)


def _proj_residual(a, w, x, g_post, mod4, layer, gate_chunk, nxt, blocks=(0, N_TOK_BLOCKS)):
    tm, tk = TOK_BLOCK, 512
    acts = a if isinstance(a, tuple) else (a,)
    n_act = len(acts)
    kdim = sum(p.shape[1] for p in acts)
    has_next = nxt is not None
    b0, nblk = blocks
    in_row_spec = pl.BlockSpec((tm, D_MODEL), lambda i, k: (b0 + i, 0))
    row_spec = pl.BlockSpec((tm, D_MODEL), lambda i, k: (i, 0))
    in_specs = [pl.BlockSpec((tm, tk // n_act), lambda i, k: (b0 + i, k)) for _ in acts]
    in_specs += [pl.BlockSpec((None, tk, D_MODEL), lambda i, k: (layer, k, 0)),
                 in_row_spec,
                 _gain_spec(layer), _mod_spec(layer, gate_chunk, tm, b0)]
    args = [*acts, w, x, g_post, mod4]
    out_specs = [row_spec]
    out_shape = [jax.ShapeDtypeStruct((nblk * tm, D_MODEL), F32)]
    if has_next:
        g_next, nl, sh_chunk, sc_chunk = nxt
        in_specs += [_gain_spec(nl), _mod_spec(nl, sh_chunk, tm, b0), _mod_spec(nl, sc_chunk, tm, b0)]
        args += [g_next, mod4, mod4]
        out_specs.append(row_spec)
        out_shape.append(jax.ShapeDtypeStruct((nblk * tm, D_MODEL), BF16))
    res = pl.pallas_call(
        functools.partial(_resid_kernel, has_next=has_next, n_act=n_act),
        grid=(nblk, kdim // tk),
        in_specs=in_specs,
        out_specs=out_specs,
        out_shape=out_shape,
        compiler_params=_params(2),
        name="proj_residual",
    )(*args)
    return res if has_next else (res[0], None)


def _ffn_up_kernel(h_ref, wa_ref, wv_ref, cwa_ref, cwv_ref, cba_ref, cbv_ref, o_ref,
                   wabf, wvbf, ua_scr, uv_scr, sa_scr, sv_scr):
    j, i = pl.program_id(0), pl.program_id(1)
    n_sub, tn = o_ref.shape[0] // TOK_BLOCK, o_ref.shape[1]
    tm = TOK_BLOCK
    grp, rt, halo = FFN_COL_GROUP, FFN_ROW_TILE, SUBLANES

    @pl.when(i == 0)
    def _():
        wabf[...] = wa_ref[...].astype(BF16)
        wvbf[...] = wv_ref[...].astype(BF16)

    @pl.when((i == 0) & (j == 0))
    def _():
        for u_scr in (ua_scr, uv_scr):
            u_scr[0:halo, :] = jnp.zeros((halo, grp), F32)
            u_scr[halo + tm:, :] = jnp.zeros((halo, grp), F32)
        for s_scr in (sa_scr, sv_scr):
            s_scr[:, 0:GRID_W, :] = jnp.zeros((3, GRID_W, grp), F32)
            s_scr[:, GRID_W + tm:, :] = jnp.zeros((3, GRID_W, grp), F32)

    def tile_taps(u_scr, r0, row_len):
        piece = u_scr[r0:r0 + rt + 2 * halo, :]
        t = r0 + lax.broadcasted_iota(jnp.int32, (rt, LANES), 0)
        pos = t & (row_len - 1)
        keep_prev = jnp.concatenate([(pos != 0).astype(F32)] * (grp // LANES), axis=1)
        keep_next = jnp.concatenate([(pos != row_len - 1).astype(F32)] * (grp // LANES), axis=1)
        n = rt + 2 * halo
        um = pltpu.roll(piece, 1, axis=0)[halo:halo + rt] * keep_prev
        up = pltpu.roll(piece, n - 1, axis=0)[halo:halo + rt] * keep_next
        return um, piece[halo:halo + rt], up

    def kernel_row(taps, cw_ref, dr, cols):
        um, u, up = taps
        r = 3 * (dr + 1)
        return um * cw_ref[r:r + 1, cols] + u * cw_ref[r + 1:r + 2, cols] + up * cw_ref[r + 2:r + 3, cols]

    def project(dst, cols, h):
        dst[0][halo:halo + tm, :] = jnp.dot(h, wabf[:, cols], preferred_element_type=F32)
        dst[1][halo:halo + tm, :] = jnp.dot(h, wvbf[:, cols], preferred_element_type=F32)

    def conv_glu(src, params, cols, latent, row_off):
        ua_scr, uv_scr = src
        wa, wv, ba, bv = params
        if not latent:
            for r0 in range(0, tm, rt):
                a = kernel_row(tile_taps(ua_scr, r0, SEQ), wa, 0, cols) + ba[:, cols]
                vv = kernel_row(tile_taps(uv_scr, r0, SEQ), wv, 0, cols) + bv[:, cols]
                o_ref[row_off + r0:row_off + r0 + rt, cols] = (_silu(a) * vv).astype(BF16)
        else:
            for r0 in range(0, tm, rt):
                ta, tv = tile_taps(ua_scr, r0, GRID_W), tile_taps(uv_scr, r0, GRID_W)
                for dr in (-1, 0, 1):
                    sa_scr[dr + 1, GRID_W + r0:GRID_W + r0 + rt, :] = kernel_row(ta, wa, dr, cols)
                    sv_scr[dr + 1, GRID_W + r0:GRID_W + r0 + rt, :] = kernel_row(tv, wv, dr, cols)
            for r0 in range(0, tm, rt):
                def gather(s_scr, b_ref):
                    return (b_ref[:, cols] + s_scr[0, r0:r0 + rt, :]
                            + s_scr[1, GRID_W + r0:GRID_W + r0 + rt, :]
                            + s_scr[2, 2 * GRID_W + r0:2 * GRID_W + r0 + rt, :])
                a, vv = gather(sa_scr, ba), gather(sv_scr, bv)
                o_ref[row_off + r0:row_off + r0 + rt, cols] = (_silu(a) * vv).astype(BF16)

    def block(latent):
        for sub in range(n_sub):
            h = h_ref[sub * tm:(sub + 1) * tm, :]
            for c in range(tn // grp):
                cols = slice(c * grp, (c + 1) * grp)
                project((ua_scr, uv_scr), cols, h)
                conv_glu((ua_scr, uv_scr), (cwa_ref, cwv_ref, cba_ref, cbv_ref), cols, latent,
                         sub * tm)

    @pl.when(i * n_sub < N_CTX_BLOCKS)
    def _():
        block(False)

    @pl.when(i * n_sub >= N_CTX_BLOCKS)
    def _():
        block(True)


def _ffn_up(h, w_up, conv_w, conv_b, layer):
    assert N_CTX_BLOCKS % FFN_TOK_BLOCKS == 0
    tm, tn = FFN_TOK_BLOCKS * TOK_BLOCK, 512
    nbn = D_FF // tn
    conv_w9 = conv_w.reshape(DEPTH, 9, 2 * D_FF)
    conv_b2 = conv_b.reshape(DEPTH, 1, 2 * D_FF)
    u_scratch = pltpu.VMEM((TOK_BLOCK + 2 * SUBLANES, FFN_COL_GROUP), F32)
    s_scratch = pltpu.VMEM((3, TOK_BLOCK + 2 * GRID_W, FFN_COL_GROUP), F32)
    return pl.pallas_call(
        _ffn_up_kernel,
        grid=(nbn, N_TOK // tm),
        in_specs=[pl.BlockSpec((tm, D_MODEL), lambda j, i: (i, 0)),
                  pl.BlockSpec((None, D_MODEL, tn), lambda j, i: (layer, 0, j)),
                  pl.BlockSpec((None, D_MODEL, tn), lambda j, i: (layer, 0, nbn + j)),
                  pl.BlockSpec((None, 9, tn), lambda j, i: (layer, 0, j)),
                  pl.BlockSpec((None, 9, tn), lambda j, i: (layer, 0, nbn + j)),
                  pl.BlockSpec((None, 1, tn), lambda j, i: (layer, 0, j)),
                  pl.BlockSpec((None, 1, tn), lambda j, i: (layer, 0, nbn + j))],
        out_specs=pl.BlockSpec((tm, tn), lambda j, i: (i, j)),
        out_shape=jax.ShapeDtypeStruct((N_TOK, D_FF), BF16),
        scratch_shapes=[pltpu.VMEM((D_MODEL, tn), BF16), pltpu.VMEM((D_MODEL, tn), BF16),
                        u_scratch, u_scratch, s_scratch, s_scratch],
        compiler_params=_params(2),
        name="ffn_up",
    )(h, w_up, w_up, conv_w9, conv_w9, conv_b2, conv_b2)


def kernel(x_prompt, x_sample, state_hgrn, c, c_ctx, w_mod, b_mod, g_pre_mix, g_post_mix, g_pre_ffn,
           g_post_ffn, w_in, hgrn_lower_bounds, hgrn_norm, hy_conv_w, hy_conv_b, hy_w1, hy_b1, hy_w2,
           hy_b2, hy_w3, hy_decay, hy_bias, w_branch_a, w_branch_b, w_out, ffn_w_up, ffn_conv_w,
           ffn_conv_b, ffn_w_down):
    p_lb = jax.nn.softmax(hgrn_lower_bounds.astype(F32), axis=0)
    cs = jnp.cumsum(p_lb, axis=0)
    lbs = cs - cs[:1]

    cond = jnp.concatenate([jnp.broadcast_to(c_ctx[None, :], (N_CTX_BLOCKS, D_MODEL)), c], axis=0)
    mod4 = _modulation(cond, w_mod, b_mod).reshape(DEPTH, N_TOK_BLOCKS, 1, N_MOD)

    g_pre_mix, g_post_mix, g_pre_ffn, g_post_ffn = (
        g.reshape(DEPTH, 1, D_MODEL) for g in (g_pre_mix, g_post_mix, g_pre_ffn, g_post_ffn))

    h, x = _prenorm(x_prompt.reshape(N_CTX, D_MODEL), x_sample.reshape(N_LAT, D_MODEL), g_pre_mix,
                    mod4, 0)
    new_states = []
    for l in range(DEPTH):
        proj = _in_proj(h, w_in, l)
        oa_ctx, s_ctx = _hgrn_mixer(proj, lbs[l], hgrn_norm[l], None, l,
                                    row_block0=0, batch=BATCH, seq_len=SEQ)
        oa_lat, _ = _hgrn_mixer(proj, lbs[l], hgrn_norm[l], state_hgrn, l,
                                row_block0=N_CTX // DEC_SEQ, batch=DEC_BATCH, seq_len=DEC_SEQ)
        new_states.append(s_ctx)
        ob = []
        for row0, n_rows, seq_len in ((0, N_CTX, SEQ), (N_CTX, N_LAT, DEC_SEQ)):
            spectrum = _hyena_spectrum(seq_len, hy_w1[l], hy_b1[l], hy_w2[l], hy_b2[l], hy_w3[l],
                                       hy_decay[l])
            ob.append(_hyena_mixer(proj, hy_conv_w[l], hy_conv_b[l], spectrum, hy_bias[l],
                                   row0=row0, n_rows=n_rows, seq_len=seq_len))
        merged = _merge(oa_ctx, oa_lat, ob[0], ob[1], proj, w_branch_a, w_branch_b, l)
        x, h2 = _proj_residual(merged, w_out, x, g_post_mix, mod4, l, 2, (g_pre_ffn, l, 3, 4))
        g = _ffn_up(h2, ffn_w_up, ffn_conv_w, ffn_conv_b, l)
        if l + 1 < DEPTH:
            x, h = _proj_residual(g, ffn_w_down, x, g_post_ffn, mod4, l, 5, (g_pre_mix, l + 1, 0, 1))
        else:
            y_ctx, _ = _proj_residual(g, ffn_w_down, x, g_post_ffn, mod4, l, 5, None,
                                      blocks=(0, N_CTX_BLOCKS))
            y_lat, _ = _proj_residual(g, ffn_w_down, x, g_post_ffn, mod4, l, 5, None,
                                      blocks=(N_CTX_BLOCKS, N_TOK_BLOCKS - N_CTX_BLOCKS))

    y_prompt = y_ctx.reshape(BATCH, SEQ, D_MODEL)
    y_sample = y_lat.reshape(DEC_BATCH, DEC_SEQ, D_MODEL)
    new_state = jnp.stack(new_states, axis=1).astype(x_prompt.dtype)
    return (y_prompt, y_sample, new_state)
```

```python
import functools
import math

import numpy as np
import jax
import jax.numpy as jnp
from jax import lax
from jax.experimental import pallas as pl
from jax.experimental.pallas import tpu as pltpu

F32 = jnp.float32
BF16 = jnp.bfloat16

D_MODEL = 2048
BATCH = 16
SEQ = 256
DEPTH = 2
DEC_BATCH = 4
DEC_SEQ = 1024
GRID_W = 64
D_A = D_MODEL // 2
D_B = D_MODEL // 2
HGRN_EXPAND = 128
H_A = D_A // HGRN_EXPAND
DK_A = HGRN_EXPAND
DV_A = D_A // H_A
HY_ORDER = 2
HY_EMB = 33
HY_BANDS = (HY_EMB - 1) // 2
HY_HID = 64
HY_SIN_W = 1.0
D_FF = 5632
N_MOD = 6 * D_MODEL
N_PROJ = 5 * D_A + 3 * D_B + 2 * D_MODEL
EPS = 1e-6

N_CTX = BATCH * SEQ
N_LAT = DEC_BATCH * DEC_SEQ
N_TOK = N_CTX + N_LAT
TOK_BLOCK = 1024
N_TOK_BLOCKS = N_TOK // TOK_BLOCK
N_CTX_BLOCKS = N_CTX // TOK_BLOCK

COL_Q, COL_FA, COL_FB, COL_I, COL_G = 0, D_A, 2 * D_A, 3 * D_A, 4 * D_A
COL_HY = 5 * D_A
COL_GATE_A = 5 * D_A + 3 * D_B
COL_GATE_B = COL_GATE_A + D_MODEL

LANES = 128
SUBLANES = 8
FFN_ROW_TILE = 128
FFN_TOK_BLOCKS = 2
SCAN_CHUNK = 64
SCAN_UNROLL = 16
HY_CB = 256
HY_ROWS = 2048
FFN_COL_GROUP = 256
EPILOGUE_ROWS = 128
VMEM_LIMIT = 60 * 1024 * 1024


def _params(n_axes):
    return pltpu.CompilerParams(dimension_semantics=("arbitrary",) * n_axes,
                                vmem_limit_bytes=VMEM_LIMIT)


def _silu(x):
    return x * jax.nn.sigmoid(x)


def _rms(x):
    return x * lax.rsqrt(jnp.mean(x * x, axis=-1, keepdims=True) + EPS)


def _mod_spec(layer, chunk, tm, block0=0):
    return pl.BlockSpec((None, None, 1, D_MODEL),
                        lambda i, *_: (layer, ((block0 + i) * tm) // TOK_BLOCK, 0, chunk))


def _gain_spec(layer):
    return pl.BlockSpec((None, 1, D_MODEL), lambda *_: (layer, 0, 0))


def _mod_kernel(cond_ref, w_ref, b_ref, o_ref):
    s = _silu(cond_ref[...]).astype(BF16)
    o_ref[...] = jnp.dot(s, w_ref[...].astype(BF16), preferred_element_type=F32) + b_ref[...]


def _modulation(cond, w_mod, b_mod):
    tn = 1024
    return pl.pallas_call(
        _mod_kernel,
        grid=(DEPTH, N_MOD // tn),
        in_specs=[pl.BlockSpec((N_TOK_BLOCKS, D_MODEL), lambda l, j: (0, 0)),
                  pl.BlockSpec((None, D_MODEL, tn), lambda l, j: (l, 0, j)),
                  pl.BlockSpec((None, 1, tn), lambda l, j: (l, 0, j))],
        out_specs=pl.BlockSpec((None, N_TOK_BLOCKS, tn), lambda l, j: (l, 0, j)),
        out_shape=jax.ShapeDtypeStruct((DEPTH, N_TOK_BLOCKS, N_MOD), F32),
        compiler_params=_params(2),
        name="modulation",
    )(cond, w_mod, b_mod.reshape(DEPTH, 1, N_MOD))


def _prenorm_kernel(xc_ref, xl_ref, g_ref, sh_ref, sc_ref, h_ref, x_ref, *, n_ctx_blocks):
    x = jnp.where(pl.program_id(0) < n_ctx_blocks, xc_ref[...], xl_ref[...])
    x_ref[...] = x
    h = (_rms(x) * g_ref[...]) * (1.0 + sc_ref[...]) + sh_ref[...]
    h_ref[...] = h.astype(BF16)


def _prenorm(x_ctx, x_lat, g, mod4, layer):
    tm = 512
    nc = N_CTX // tm
    row_spec = pl.BlockSpec((tm, D_MODEL), lambda i: (i, 0))
    return pl.pallas_call(
        functools.partial(_prenorm_kernel, n_ctx_blocks=nc),
        grid=(N_TOK // tm,),
        in_specs=[pl.BlockSpec((tm, D_MODEL), lambda i: (jnp.minimum(i, nc - 1), 0)),
                  pl.BlockSpec((tm, D_MODEL), lambda i: (jnp.maximum(i - nc, 0), 0)),
                  _gain_spec(layer), _mod_spec(layer, 0, tm), _mod_spec(layer, 1, tm)],
        out_specs=[row_spec, row_spec],
        out_shape=[jax.ShapeDtypeStruct((N_TOK, D_MODEL), BF16),
                   jax.ShapeDtypeStruct((N_TOK, D_MODEL), F32)],
        compiler_params=_params(1),
        name="prenorm",
    )(x_ctx, x_lat, g, mod4, mod4)


def _inproj_kernel(h_ref, w_ref, o_ref, wbf_ref):
    @pl.when(pl.program_id(1) == 0)
    def _():
        wbf_ref[...] = w_ref[...].astype(BF16)

    o_ref[...] = jnp.dot(h_ref[...], wbf_ref[...], preferred_element_type=F32)


def _in_proj(h, w_in, layer):
    tm, tn = 1024, 1024
    return pl.pallas_call(
        _inproj_kernel,
        grid=(N_PROJ // tn, N_TOK // tm),
        in_specs=[pl.BlockSpec((tm, D_MODEL), lambda j, i: (i, 0)),
                  pl.BlockSpec((None, D_MODEL, tn), lambda j, i: (layer, 0, j))],
        out_specs=pl.BlockSpec((tm, tn), lambda j, i: (i, j)),
        out_shape=jax.ShapeDtypeStruct((N_TOK, N_PROJ), F32),
        scratch_shapes=[pltpu.VMEM((D_MODEL, tn), BF16)],
        compiler_params=_params(2),
        name="in_proj",
    )(h, w_in)


def _scan_levels(chunk):
    m, levels = chunk // 2, []
    while m >= 1:
        levels.append(m)
        m //= 2
    return tuple(levels)


@functools.lru_cache(maxsize=None)
def _scan_tables(chunk):
    levels = _scan_levels(chunk)
    amats, rowsels, pairs = [], [], []
    for rev in (False, True):
        blocks, sels = [], []
        for m in levels:
            a = np.zeros((chunk, chunk), np.float32)
            sel = np.zeros((chunk,), np.float32)
            for t in range(chunk):
                mid = 2 * m * (t // (2 * m)) + m - 1
                upper = t > mid
                if not rev:
                    if upper:
                        a[t, mid + 1:t + 1] = 1.0
                    else:
                        a[t, t + 1:mid + 1] = 1.0
                    sel[t] = 1.0 if upper else 0.0
                else:
                    if upper:
                        a[t, mid + 1:t] = 1.0
                    else:
                        a[t, t:mid + 1] = 1.0
                    sel[t] = 0.0 if upper else 1.0
            blocks.append(a)
            sels.append(np.repeat(sel[:, None], LANES, axis=1))
        idx = np.arange(chunk)
        if not rev:
            inter = (idx[None, :] <= idx[:, None]).astype(np.float32)
            carry = (idx[None, :] > idx[:, None]).astype(np.float32)
        else:
            inter = (idx[None, :] >= idx[:, None]).astype(np.float32)
            carry = (idx[None, :] < idx[:, None]).astype(np.float32)
        blocks += [inter, carry, np.ones((16, chunk), np.float32)]
        amats.append(np.concatenate(blocks, axis=0))
        rowsels.append(np.stack(sels))
        masks = []
        for m, sel in zip(levels, sels):
            same = idx[:, None] // (2 * m) == idx[None, :] // (2 * m)
            masks.append((same & (sel[:, :1] > 0.5) & (sel[:, :1].T < 0.5)).astype(np.float32))
        masks.append(np.eye(chunk, dtype=np.float32))
        pairs.append(np.stack(masks))
    return np.stack(amats), np.stack(rowsels), np.stack(pairs)


def _dot_exact_rhs(a_bf16, x):
    x1 = x.astype(BF16)
    x2 = (x - x1.astype(F32)).astype(BF16)
    n = x.shape[1]
    y = jnp.dot(a_bf16, jnp.concatenate([x1, x2], axis=1), preferred_element_type=F32)
    return y[:, :n] + y[:, n:]


_NT = (((1,), (1,)), ((), ()))
_TN = (((0,), (0,)), ((), ()))


def _hgrn_kernel(*refs, seq_len, chunk, has_s0):
    if has_s0:
        (q_ref, fa_ref, fb_ref, i_ref, g_ref, lb_ref, nw_ref, amat_ref, rowsel_ref, pair_ref, s0_ref,
         o_ref, sfin_ref, of_scr, ob_scr, st_scr) = refs
    else:
        (q_ref, fa_ref, fb_ref, i_ref, g_ref, lb_ref, nw_ref, amat_ref, rowsel_ref, pair_ref,
         o_ref, sfin_ref, of_scr, ob_scr, st_scr) = refs
    n_chunks = seq_len // chunk
    n_lev = len(_scan_levels(chunk))
    unroll = min(SCAN_UNROLL, n_chunks)

    for d in range(2):
        if has_s0:
            st_scr[d] = s0_ref[d].T
        else:
            st_scr[d] = jnp.zeros((DV_A, DK_A), F32)

    f_refs = (fa_ref, fb_ref)
    o_scrs = (of_scr, ob_scr)

    def gates(d, rows):
        z = f_refs[d][rows, :]
        q = _silu(q_ref[rows, :])
        lbd = lb_ref[d:d + 1, :]
        e = jnp.exp(-jnp.abs(z))
        den = 1.0 + e
        a1 = jnp.log(lbd)
        a2 = jnp.log1p(-lbd) + (jnp.minimum(z, 0.0) - jnp.log(den))
        logf = jnp.maximum(a1, a2) + jnp.log(1.0 + jnp.exp(-jnp.abs(a1 - a2)))
        k = (1.0 - lbd) * (jnp.where(z >= 0.0, e, 1.0) / den)
        return q, k, logf

    def body(jj, carry):
        chains = []
        for d in range(2):
            for u in range(unroll):
                j = jj * unroll + u
                if d == 1:
                    j = n_chunks - 1 - j
                chains.append((d, pl.ds(pl.multiple_of(j * chunk, chunk), chunk)))
        qkf = [gates(d, rows) for d, rows in chains]
        ws = [jnp.exp(_dot_exact_rhs(amat_ref[d], logf))
              for (d, _), (_, _, logf) in zip(chains, qkf)]
        vbs = [i_ref[rows, :].astype(BF16) for _, rows in chains]
        atts, dsts = [], []
        for (d, _), (q, k, _), w, vb in zip(chains, qkf, ws, vbs):
            att = lax.dot_general(q.astype(BF16), k.astype(BF16), _NT,
                                  preferred_element_type=F32) * pair_ref[d, n_lev]
            for lv in range(n_lev):
                x = (jnp.where(rowsel_ref[d, lv] > 0.5, q, k)
                     * w[lv * chunk:(lv + 1) * chunk]).astype(BF16)
                att = att + lax.dot_general(x, x, _NT, preferred_element_type=F32) * pair_ref[d, lv]
            atts.append(att.astype(BF16))
            wk = w[(n_lev + 1) * chunk:(n_lev + 2) * chunk]
            dsts.append(lax.dot_general(vb, (k * wk).astype(BF16), _TN, preferred_element_type=F32))
        for d in range(2):
            st = st_scr[d]
            for u in range(unroll):
                c = d * unroll + u
                q, w = qkf[c][0], ws[c]
                wq = w[n_lev * chunk:(n_lev + 1) * chunk]
                wall = w[(n_lev + 2) * chunk:(n_lev + 2) * chunk + 1]
                o = lax.dot_general((q * wq).astype(BF16), st.astype(BF16), _NT,
                                    preferred_element_type=F32)
                o = o + jnp.dot(atts[c], vbs[c], preferred_element_type=F32)
                o_scrs[d][chains[c][1], :] = o
                st = st * wall + dsts[c]
            st_scr[d] = st
        return carry

    lax.fori_loop(0, n_chunks // unroll, body, 0)

    o = of_scr[...] + ob_scr[...]
    y = _rms(o) * nw_ref[...]
    o_ref[...] = (y * _silu(g_ref[...])).astype(BF16)
    for d in range(2):
        sfin_ref[d] = st_scr[d].T


def _hgrn_mixer(proj, lb, norm_w, s0, layer, *, row_block0, batch, seq_len):
    chunk = SCAN_CHUNK
    amat, rowsel, pair = _scan_tables(chunk)
    amat = jnp.asarray(amat).astype(BF16)
    rowsel = jnp.asarray(rowsel)
    pair = jnp.asarray(pair)
    has_s0 = s0 is not None

    def col_spec(col):
        return pl.BlockSpec((seq_len, LANES), lambda b, h: (row_block0 + b, col // LANES + h))

    def const_spec(a):
        nd = a.ndim
        return pl.BlockSpec(a.shape, lambda b, h: (0,) * nd)

    in_specs = [col_spec(COL_Q), col_spec(COL_FA), col_spec(COL_FB), col_spec(COL_I), col_spec(COL_G),
                pl.BlockSpec((2, LANES), lambda b, h: (0, h)),
                pl.BlockSpec((1, LANES), lambda b, h: (0, 0)),
                const_spec(amat), const_spec(rowsel), const_spec(pair)]
    args = [proj, proj, proj, proj, proj, lb, norm_w.reshape(1, DV_A), amat, rowsel, pair]
    if has_s0:
        in_specs.append(pl.BlockSpec((None, None, 2, None, DK_A, DV_A),
                                     lambda b, h: (b, layer, 0, h, 0, 0)))
        args.append(s0)
    kern = functools.partial(_hgrn_kernel, seq_len=seq_len, chunk=chunk, has_s0=has_s0)
    return pl.pallas_call(
        kern,
        grid=(batch, H_A),
        in_specs=in_specs,
        out_specs=[pl.BlockSpec((seq_len, LANES), lambda b, h: (b, h)),
                   pl.BlockSpec((None, 2, None, DK_A, DV_A), lambda b, h: (b, 0, h, 0, 0))],
        out_shape=[jax.ShapeDtypeStruct((batch * seq_len, D_A), BF16),
                   jax.ShapeDtypeStruct((batch, 2, H_A, DK_A, DV_A), F32)],
        scratch_shapes=[pltpu.VMEM((seq_len, DV_A), F32), pltpu.VMEM((seq_len, DV_A), F32),
                        pltpu.VMEM((2, DV_A, DK_A), F32)],
        compiler_params=_params(2),
        name="hgrn_mixer",
    )(*args)


@functools.lru_cache(maxsize=None)
def _hyena_tables(seq_len):
    n = seq_len
    t = np.arange(n, dtype=np.float64)
    t01 = t / max(n - 1, 1)
    bands = np.linspace(1e-4, HY_BANDS - 1, HY_BANDS)
    ang = (2.0 * math.pi / n) * t[:, None] * bands[None, :]
    feats = np.concatenate([t01[:, None], np.cos(ang), -np.sin(ang)], axis=-1)
    feats = np.pad(feats, ((0, 0), (0, LANES - HY_EMB))).astype(np.float32)
    ft = np.outer(np.arange(n), np.arange(n)) % (2 * n)
    fc = np.cos(math.pi * ft / n)
    fs = -np.sin(math.pi * ft / n)
    fs[0, :] = 1.0 - 2.0 * (np.arange(n) % 2)
    return feats, fc.astype(np.float32), fs.astype(np.float32), fs.T.astype(np.float32).copy()


def _dot_split(a, b):
    ah = a.astype(BF16)
    al = (a - ah.astype(F32)).astype(BF16)
    bh = b.astype(BF16)
    bl = (b - bh.astype(F32)).astype(BF16)
    dot = functools.partial(jnp.dot, preferred_element_type=F32)
    return dot(ah, bh) + dot(ah, bl) + dot(al, bh)


def _hyena_filter_kernel(feats_ref, w1_ref, b1_ref, w2_ref, b2_ref, w3_ref0, w3_ref1, w3_ref2, w3_ref3,
                         decay_ref, fc_ref, fs_ref, p_ref, h_scr, *, seq_len):
    n = seq_len
    hp = lax.Precision.HIGHEST

    @pl.when(pl.program_id(0) == 0)
    def _():
        h1 = jnp.sin(HY_SIN_W * (jnp.dot(feats_ref[...], w1_ref[...], precision=hp,
                                         preferred_element_type=F32) + b1_ref[...]))
        h_scr[...] = jnp.sin(HY_SIN_W * (jnp.dot(h1, w2_ref[...], precision=hp,
                                                 preferred_element_type=F32) + b2_ref[...]))

    h = h_scr[...]
    w3 = ((w3_ref0, w3_ref1), (w3_ref2, w3_ref3))
    cb = p_ref.shape[-1]
    row = lax.broadcasted_iota(jnp.int32, (n, cb), 0)
    t01 = row.astype(F32) / float(max(n - 1, 1))
    scale = 1.0 / (2.0 * n)
    wts = jnp.where(row == 0, scale, 2.0 * scale)
    sgn = (1 - 2 * (row & 1)).astype(F32)
    for o in range(HY_ORDER):
        win = jnp.exp(-t01 * jnp.abs(decay_ref[o:o + 1, :]))
        hf = _dot_split(h, w3[o][0][...]) * win
        hb = _dot_split(h, w3[o][1][...]) * win
        mag = jnp.where(row == 0, jnp.abs(hf + hb), jnp.abs(hf) + jnp.abs(hb))
        inv = 1.0 / (jnp.sum(mag, axis=0, keepdims=True) + EPS)
        even = (hf + hb) * inv
        odd = (hf - hb) * inv
        kre = jnp.dot(fc_ref[...], even.astype(BF16), preferred_element_type=F32)
        kim = jnp.dot(fs_ref[...], odd.astype(BF16), preferred_element_type=F32)
        knyq = jnp.sum(even * sgn, axis=0, keepdims=True)
        p_ref[3 * o + 0] = kre * wts
        p_ref[3 * o + 1] = jnp.where(row == 0, 0.0, kim * wts)
        p_ref[3 * o + 2] = jnp.where(row == 0, knyq * scale, kre * wts)


def _hyena_spectrum(seq_len, w1, b1, w2, b2, w3, decay):
    feats, fc, fs, _ = _hyena_tables(seq_len)
    feats = jnp.asarray(feats)
    mats = [jnp.asarray(m).astype(BF16) for m in (fc, fs)]
    w1p = jnp.pad(w1, ((0, LANES - HY_EMB), (0, 0)))
    cb = HY_CB
    nb = D_B // cb

    def w3_spec(k):
        return pl.BlockSpec((HY_HID, cb), lambda c: (0, k * nb + c))

    full2 = lambda a: pl.BlockSpec(a.shape, lambda c: (0, 0))
    b1r, b2r = b1.reshape(1, HY_HID), b2.reshape(1, HY_HID)
    return pl.pallas_call(
        functools.partial(_hyena_filter_kernel, seq_len=seq_len),
        grid=(nb,),
        in_specs=[full2(feats), full2(w1p), full2(b1r), full2(w2), full2(b2r),
                  w3_spec(0), w3_spec(1), w3_spec(2), w3_spec(3),
                  pl.BlockSpec((HY_ORDER, cb), lambda c: (0, c))] + [full2(m) for m in mats],
        out_specs=pl.BlockSpec((3 * HY_ORDER, seq_len, cb), lambda c: (0, 0, c)),
        out_shape=jax.ShapeDtypeStruct((3 * HY_ORDER, seq_len, D_B), F32),
        scratch_shapes=[pltpu.VMEM((seq_len, HY_HID), F32)],
        compiler_params=_params(1),
        name="hyena_filter",
    )(feats, w1p, b1r, w2, b2r, w3, w3, w3, w3, decay, *mats)


def _hyena_conv_kernel(v_ref, x1_ref, x2_ref, cwv_ref, cw1_ref, cw2_ref, cbv_ref, cb1_ref, cb2_ref,
                       p_ref, bias_ref, fc_ref, fs_ref, fst_ref, o_ref, *, seq_len):
    n = seq_len
    rows, cb = o_ref.shape
    nseq = rows // n
    pos = lax.broadcasted_iota(jnp.int32, (rows, cb), 0) & (n - 1)

    def dwconv(x_ref, w_ref, b_ref):
        x = x_ref[...]
        xm = jnp.where(pos == 0, 0.0, pltpu.roll(x, 1, axis=0))
        xp = jnp.where(pos == n - 1, 0.0, pltpu.roll(x, rows - 1, axis=0))
        y = xm * w_ref[0:1, :] + x * w_ref[1:2, :] + xp * w_ref[2:3, :] + b_ref[...]
        return jnp.concatenate([y[s * n:(s + 1) * n] for s in range(nseq)], axis=1)

    def tiled(a):
        return jnp.concatenate([a] * nseq, axis=1)

    def fftconv(z, o):
        zb16 = z.astype(BF16)
        zt = jnp.dot(fc_ref[...], zb16, preferred_element_type=F32)
        zb = jnp.dot(fs_ref[...], zb16, preferred_element_type=F32)
        p1, p3, p4 = tiled(p_ref[3 * o]), tiled(p_ref[3 * o + 1]), tiled(p_ref[3 * o + 2])
        yt = (zt * p1 - zb * p3).astype(BF16)
        yb = (zt * p3 + zb * p4).astype(BF16)
        y = (jnp.dot(fc_ref[...], yt, preferred_element_type=F32)
             + jnp.dot(fst_ref[...], yb, preferred_element_type=F32))
        return y + z * tiled(bias_ref[o:o + 1, :])

    v = dwconv(v_ref, cwv_ref, cbv_ref)
    x1 = dwconv(x1_ref, cw1_ref, cb1_ref)
    z2 = x1 * fftconv(v, 0)
    x2 = dwconv(x2_ref, cw2_ref, cb2_ref)
    out = x2 * fftconv(z2, 1)
    for s in range(nseq):
        o_ref[s * n:(s + 1) * n, :] = out[:, s * cb:(s + 1) * cb].astype(BF16)


def _hyena_mixer(proj, conv_w, conv_b, spectrum, bias, *, row0, n_rows, seq_len):
    assert seq_len & (seq_len - 1) == 0 and HY_ROWS % seq_len == 0 and row0 % HY_ROWS == 0
    _, fc, fs, fst = _hyena_tables(seq_len)
    mats = [jnp.asarray(m).astype(BF16) for m in (fc, fs, fst)]
    cb = HY_CB
    nb = D_B // cb
    rb0 = row0 // HY_ROWS

    def col_spec(k):
        return pl.BlockSpec((HY_ROWS, cb), lambda c, g: (rb0 + g, (COL_HY + k * D_B) // cb + c))

    def cw_spec(k):
        return pl.BlockSpec((3, cb), lambda c, g: (0, k * nb + c))

    def cb_spec(k):
        return pl.BlockSpec((1, cb), lambda c, g: (0, k * nb + c))

    mat_spec = pl.BlockSpec((seq_len, seq_len), lambda c, g: (0, 0))
    conv_b2 = conv_b.reshape(1, 3 * D_B)
    return pl.pallas_call(
        functools.partial(_hyena_conv_kernel, seq_len=seq_len),
        grid=(nb, n_rows // HY_ROWS),
        in_specs=[col_spec(0), col_spec(1), col_spec(2), cw_spec(0), cw_spec(1), cw_spec(2),
                  cb_spec(0), cb_spec(1), cb_spec(2),
                  pl.BlockSpec((3 * HY_ORDER, seq_len, cb), lambda c, g: (0, 0, c)),
                  pl.BlockSpec((HY_ORDER, cb), lambda c, g: (0, c))] + [mat_spec] * 3,
        out_specs=pl.BlockSpec((HY_ROWS, cb), lambda c, g: (g, c)),
        out_shape=jax.ShapeDtypeStruct((n_rows, D_B), BF16),
        compiler_params=_params(2),
        name="hyena_conv",
    )(proj, proj, proj, conv_w, conv_w, conv_w, conv_b2, conv_b2, conv_b2, spectrum, bias, *mats)


def _merge_kernel(oac_ref, oal_ref, obc_ref, obl_ref, ga_ref, gb_ref, wa_ref, wb_ref, o_ref,
                  wabf, wbbf):
    i = pl.program_id(1)

    @pl.when(i == 0)
    def _():
        wabf[...] = wa_ref[...].astype(BF16)
        wbbf[...] = wb_ref[...].astype(BF16)

    is_ctx = i < N_CTX_BLOCKS
    oa = jnp.where(is_ctx, oac_ref[...], oal_ref[...])
    ob = jnp.where(is_ctx, obc_ref[...], obl_ref[...])
    ya = jnp.dot(oa, wabf[...], preferred_element_type=F32)
    yb = jnp.dot(ob, wbbf[...], preferred_element_type=F32)
    o_ref[...] = (jax.nn.sigmoid(ga_ref[...]) * ya + jax.nn.sigmoid(gb_ref[...]) * yb).astype(BF16)


def _merge(oa_ctx, oa_lat, ob_ctx, ob_lat, proj, w_a, w_b, layer):
    tm, tn = TOK_BLOCK, 512
    nbn = D_MODEL // tn
    ctx_blk = lambda j, i: (jnp.minimum(i, N_CTX_BLOCKS - 1), 0)
    lat_blk = lambda j, i: (jnp.maximum(i - N_CTX_BLOCKS, 0), 0)
    return pl.pallas_call(
        _merge_kernel,
        grid=(nbn, N_TOK // tm),
        in_specs=[pl.BlockSpec((tm, D_A), ctx_blk), pl.BlockSpec((tm, D_A), lat_blk),
                  pl.BlockSpec((tm, D_B), ctx_blk), pl.BlockSpec((tm, D_B), lat_blk),
                  pl.BlockSpec((tm, tn), lambda j, i: (i, COL_GATE_A // tn + j)),
                  pl.BlockSpec((tm, tn), lambda j, i: (i, COL_GATE_B // tn + j)),
                  pl.BlockSpec((None, D_A, tn), lambda j, i: (layer, 0, j)),
                  pl.BlockSpec((None, D_B, tn), lambda j, i: (layer, 0, j))],
        out_specs=pl.BlockSpec((tm, tn), lambda j, i: (i, j)),
        out_shape=jax.ShapeDtypeStruct((N_TOK, D_MODEL), BF16),
        scratch_shapes=[pltpu.VMEM((D_A, tn), BF16), pltpu.VMEM((D_B, tn), BF16)],
        compiler_params=_params(2),
        name="merge",
    )(oa_ctx, oa_lat, ob_ctx, ob_lat, proj, proj, w_a, w_b)


def _resid_kernel(*refs, has_next, n_act):
    a_refs, refs = refs[:n_act], refs[n_act:]
    if has_next:
        (w_ref, x_ref, gpost_ref, gate_ref, gnext_ref, sh_ref, sc_ref, xo_ref, h_ref) = refs
    else:
        w_ref, x_ref, gpost_ref, gate_ref, xo_ref = refs
    k = pl.program_id(1)

    @pl.when(k == 0)
    def _():
        xo_ref[...] = jnp.zeros_like(xo_ref)

    part = w_ref.shape[0] // n_act
    y = None
    for p, a_ref in enumerate(a_refs):
        yp = jnp.dot(a_ref[...], w_ref[p * part:(p + 1) * part, :].astype(BF16),
                     preferred_element_type=F32)
        y = yp if y is None else y + yp
    xo_ref[...] += y

    @pl.when(k == pl.num_programs(1) - 1)
    def _():
        post_scale = gate_ref[...] * gpost_ref[...]
        next_scale = gnext_ref[...] * (1.0 + sc_ref[...]) if has_next else None

        def rows_step(r, carry):
            rows = pl.ds(pl.multiple_of(r * EPILOGUE_ROWS, EPILOGUE_ROWS), EPILOGUE_ROWS)
            xn = x_ref[rows, :] + _rms(xo_ref[rows, :]) * post_scale
            xo_ref[rows, :] = xn
            if has_next:
                h_ref[rows, :] = (_rms(xn) * next_scale + sh_ref[...]).astype(BF16)
            return carry

        lax.fori_loop(0, xo_ref.shape[0] // EPILOGUE_ROWS, rows_step, 0)


def _proj_residual(a, w, x, g_post, mod4, layer, gate_chunk, nxt, blocks=(0, N_TOK_BLOCKS)):
    tm, tk = TOK_BLOCK, 512
    acts = a if isinstance(a, tuple) else (a,)
    n_act = len(acts)
    kdim = sum(p.shape[1] for p in acts)
    has_next = nxt is not None
    b0, nblk = blocks
    in_row_spec = pl.BlockSpec((tm, D_MODEL), lambda i, k: (b0 + i, 0))
    row_spec = pl.BlockSpec((tm, D_MODEL), lambda i, k: (i, 0))
    in_specs = [pl.BlockSpec((tm, tk // n_act), lambda i, k: (b0 + i, k)) for _ in acts]
    in_specs += [pl.BlockSpec((None, tk, D_MODEL), lambda i, k: (layer, k, 0)),
                 in_row_spec,
                 _gain_spec(layer), _mod_spec(layer, gate_chunk, tm, b0)]
    args = [*acts, w, x, g_post, mod4]
    out_specs = [row_spec]
    out_shape = [jax.ShapeDtypeStruct((nblk * tm, D_MODEL), F32)]
    if has_next:
        g_next, nl, sh_chunk, sc_chunk = nxt
        in_specs += [_gain_spec(nl), _mod_spec(nl, sh_chunk, tm, b0), _mod_spec(nl, sc_chunk, tm, b0)]
        args += [g_next, mod4, mod4]
        out_specs.append(row_spec)
        out_shape.append(jax.ShapeDtypeStruct((nblk * tm, D_MODEL), BF16))
    res = pl.pallas_call(
        functools.partial(_resid_kernel, has_next=has_next, n_act=n_act),
        grid=(nblk, kdim // tk),
        in_specs=in_specs,
        out_specs=out_specs,
        out_shape=out_shape,
        compiler_params=_params(2),
        name="proj_residual",
    )(*args)
    return res if has_next else (res[0], None)


def _ffn_up_kernel(h_ref, wa_ref, wv_ref, cwa_ref, cwv_ref, cba_ref, cbv_ref, o_ref,
                   wabf, wvbf, ua_scr, uv_scr, sa_scr, sv_scr):
    j, i = pl.program_id(0), pl.program_id(1)
    n_sub, tn = o_ref.shape[0] // TOK_BLOCK, o_ref.shape[1]
    tm = TOK_BLOCK
    grp, rt, halo = FFN_COL_GROUP, FFN_ROW_TILE, SUBLANES

    @pl.when(i == 0)
    def _():
        wabf[...] = wa_ref[...].astype(BF16)
        wvbf[...] = wv_ref[...].astype(BF16)

    @pl.when((i == 0) & (j == 0))
    def _():
        for u_scr in (ua_scr, uv_scr):
            u_scr[0:halo, :] = jnp.zeros((halo, grp), F32)
            u_scr[halo + tm:, :] = jnp.zeros((halo, grp), F32)
        for s_scr in (sa_scr, sv_scr):
            s_scr[:, 0:GRID_W, :] = jnp.zeros((3, GRID_W, grp), F32)
            s_scr[:, GRID_W + tm:, :] = jnp.zeros((3, GRID_W, grp), F32)

    def tile_taps(u_scr, r0, row_len):
        piece = u_scr[r0:r0 + rt + 2 * halo, :]
        t = r0 + lax.broadcasted_iota(jnp.int32, (rt, LANES), 0)
        pos = t & (row_len - 1)
        keep_prev = jnp.concatenate([(pos != 0).astype(F32)] * (grp // LANES), axis=1)
        keep_next = jnp.concatenate([(pos != row_len - 1).astype(F32)] * (grp // LANES), axis=1)
        n = rt + 2 * halo
        um = pltpu.roll(piece, 1, axis=0)[halo:halo + rt] * keep_prev
        up = pltpu.roll(piece, n - 1, axis=0)[halo:halo + rt] * keep_next
        return um, piece[halo:halo + rt], up

    def kernel_row(taps, cw_ref, dr, cols):
        um, u, up = taps
        r = 3 * (dr + 1)
        return um * cw_ref[r:r + 1, cols] + u * cw_ref[r + 1:r + 2, cols] + up * cw_ref[r + 2:r + 3, cols]

    def project(dst, cols, h):
        dst[0][halo:halo + tm, :] = jnp.dot(h, wabf[:, cols], preferred_element_type=F32)
        dst[1][halo:halo + tm, :] = jnp.dot(h, wvbf[:, cols], preferred_element_type=F32)

    def conv_glu(src, params, cols, latent, row_off):
        ua_scr, uv_scr = src
        wa, wv, ba, bv = params
        if not latent:
            for r0 in range(0, tm, rt):
                a = kernel_row(tile_taps(ua_scr, r0, SEQ), wa, 0, cols) + ba[:, cols]
                vv = kernel_row(tile_taps(uv_scr, r0, SEQ), wv, 0, cols) + bv[:, cols]
                o_ref[row_off + r0:row_off + r0 + rt, cols] = (_silu(a) * vv).astype(BF16)
        else:
            for r0 in range(0, tm, rt):
                ta, tv = tile_taps(ua_scr, r0, GRID_W), tile_taps(uv_scr, r0, GRID_W)
                for dr in (-1, 0, 1):
                    sa_scr[dr + 1, GRID_W + r0:GRID_W + r0 + rt, :] = kernel_row(ta, wa, dr, cols)
                    sv_scr[dr + 1, GRID_W + r0:GRID_W + r0 + rt, :] = kernel_row(tv, wv, dr, cols)
            for r0 in range(0, tm, rt):
                def gather(s_scr, b_ref):
                    return (b_ref[:, cols] + s_scr[0, r0:r0 + rt, :]
                            + s_scr[1, GRID_W + r0:GRID_W + r0 + rt, :]
                            + s_scr[2, 2 * GRID_W + r0:2 * GRID_W + r0 + rt, :])
                a, vv = gather(sa_scr, ba), gather(sv_scr, bv)
                o_ref[row_off + r0:row_off + r0 + rt, cols] = (_silu(a) * vv).astype(BF16)

    def block(latent):
        for sub in range(n_sub):
            h = h_ref[sub * tm:(sub + 1) * tm, :]
            for c in range(tn // grp):
                cols = slice(c * grp, (c + 1) * grp)
                project((ua_scr, uv_scr), cols, h)
                conv_glu((ua_scr, uv_scr), (cwa_ref, cwv_ref, cba_ref, cbv_ref), cols, latent,
                         sub * tm)

    @pl.when(i * n_sub < N_CTX_BLOCKS)
    def _():
        block(False)

    @pl.when(i * n_sub >= N_CTX_BLOCKS)
    def _():
        block(True)


def _ffn_up(h, w_up, conv_w, conv_b, layer):
    assert N_CTX_BLOCKS % FFN_TOK_BLOCKS == 0
    tm, tn = FFN_TOK_BLOCKS * TOK_BLOCK, 512
    nbn = D_FF // tn
    conv_w9 = conv_w.reshape(DEPTH, 9, 2 * D_FF)
    conv_b2 = conv_b.reshape(DEPTH, 1, 2 * D_FF)
    u_scratch = pltpu.VMEM((TOK_BLOCK + 2 * SUBLANES, FFN_COL_GROUP), F32)
    s_scratch = pltpu.VMEM((3, TOK_BLOCK + 2 * GRID_W, FFN_COL_GROUP), F32)
    return pl.pallas_call(
        _ffn_up_kernel,
        grid=(nbn, N_TOK // tm),
        in_specs=[pl.BlockSpec((tm, D_MODEL), lambda j, i: (i, 0)),
                  pl.BlockSpec((None, D_MODEL, tn), lambda j, i: (layer, 0, j)),
                  pl.BlockSpec((None, D_MODEL, tn), lambda j, i: (layer, 0, nbn + j)),
                  pl.BlockSpec((None, 9, tn), lambda j, i: (layer, 0, j)),
                  pl.BlockSpec((None, 9, tn), lambda j, i: (layer, 0, nbn + j)),
                  pl.BlockSpec((None, 1, tn), lambda j, i: (layer, 0, j)),
                  pl.BlockSpec((None, 1, tn), lambda j, i: (layer, 0, nbn + j))],
        out_specs=pl.BlockSpec((tm, tn), lambda j, i: (i, j)),
        out_shape=jax.ShapeDtypeStruct((N_TOK, D_FF), BF16),
        scratch_shapes=[pltpu.VMEM((D_MODEL, tn), BF16), pltpu.VMEM((D_MODEL, tn), BF16),
                        u_scratch, u_scratch, s_scratch, s_scratch],
        compiler_params=_params(2),
        name="ffn_up",
    )(h, w_up, w_up, conv_w9, conv_w9, conv_b2, conv_b2)


def kernel(x_prompt, x_sample, state_hgrn, c, c_ctx, w_mod, b_mod, g_pre_mix, g_post_mix, g_pre_ffn,
           g_post_ffn, w_in, hgrn_lower_bounds, hgrn_norm, hy_conv_w, hy_conv_b, hy_w1, hy_b1, hy_w2,
           hy_b2, hy_w3, hy_decay, hy_bias, w_branch_a, w_branch_b, w_out, ffn_w_up, ffn_conv_w,
           ffn_conv_b, ffn_w_down):
    p_lb = jax.nn.softmax(hgrn_lower_bounds.astype(F32), axis=0)
    cs = jnp.cumsum(p_lb, axis=0)
    lbs = cs - cs[:1]

    cond = jnp.concatenate([jnp.broadcast_to(c_ctx[None, :], (N_CTX_BLOCKS, D_MODEL)), c], axis=0)
    mod4 = _modulation(cond, w_mod, b_mod).reshape(DEPTH, N_TOK_BLOCKS, 1, N_MOD)

    g_pre_mix, g_post_mix, g_pre_ffn, g_post_ffn = (
        g.reshape(DEPTH, 1, D_MODEL) for g in (g_pre_mix, g_post_mix, g_pre_ffn, g_post_ffn))

    h, x = _prenorm(x_prompt.reshape(N_CTX, D_MODEL), x_sample.reshape(N_LAT, D_MODEL), g_pre_mix,
                    mod4, 0)
    new_states = []
    for l in range(DEPTH):
        proj = _in_proj(h, w_in, l)
        oa_ctx, s_ctx = _hgrn_mixer(proj, lbs[l], hgrn_norm[l], None, l,
                                    row_block0=0, batch=BATCH, seq_len=SEQ)
        oa_lat, _ = _hgrn_mixer(proj, lbs[l], hgrn_norm[l], state_hgrn, l,
                                row_block0=N_CTX // DEC_SEQ, batch=DEC_BATCH, seq_len=DEC_SEQ)
        new_states.append(s_ctx)
        ob = []
        for row0, n_rows, seq_len in ((0, N_CTX, SEQ), (N_CTX, N_LAT, DEC_SEQ)):
            spectrum = _hyena_spectrum(seq_len, hy_w1[l], hy_b1[l], hy_w2[l], hy_b2[l], hy_w3[l],
                                       hy_decay[l])
            ob.append(_hyena_mixer(proj, hy_conv_w[l], hy_conv_b[l], spectrum, hy_bias[l],
                                   row0=row0, n_rows=n_rows, seq_len=seq_len))
        merged = _merge(oa_ctx, oa_lat, ob[0], ob[1], proj, w_branch_a, w_branch_b, l)
        x, h2 = _proj_residual(merged, w_out, x, g_post_mix, mod4, l, 2, (g_pre_ffn, l, 3, 4))
        g = _ffn_up(h2, ffn_w_up, ffn_conv_w, ffn_conv_b, l)
        if l + 1 < DEPTH:
            x, h = _proj_residual(g, ffn_w_down, x, g_post_ffn, mod4, l, 5, (g_pre_mix, l + 1, 0, 1))
        else:
            y_ctx, _ = _proj_residual(g, ffn_w_down, x, g_post_ffn, mod4, l, 5, None,
                                      blocks=(0, N_CTX_BLOCKS))
            y_lat, _ = _proj_residual(g, ffn_w_down, x, g_post_ffn, mod4, l, 5, None,
                                      blocks=(N_CTX_BLOCKS, N_TOK_BLOCKS - N_CTX_BLOCKS))

    y_prompt = y_ctx.reshape(BATCH, SEQ, D_MODEL)
    y_sample = y_lat.reshape(DEC_BATCH, DEC_SEQ, D_MODEL)
    new_state = jnp.stack(new_states, axis=1).astype(x_prompt.dtype)
    return (y_prompt, y_sample, new_state)
```
